```python
import math
import jax, jax.numpy as jnp
from jax import lax
import numpy as np

D_MODEL = 2048
BATCH = 4
SEQ = 2048
DEPTH = 1

GRID_W = 64
HEAD_DIM = 128
MIX_WIDTH = D_MODEL
NA_WIDTH = MIX_WIDTH // 2
DA_WIDTH = MIX_WIDTH - NA_WIDTH
NA_HEADS = NA_WIDTH // HEAD_DIM
DA_HEADS = DA_WIDTH // HEAD_DIM
DA_QK_DIM = HEAD_DIM // 2
NA_WIN_ROWS_MAX = 8
NA_WIN_COLS = 16
T5_BUCKETS = 32
T5_MAX_DIST = 128
D_FF = -(-8 * D_MODEL // (3 * 256)) * 256
Q_BLOCK = 128
EPS = 1e-6
PROJ_WIDTH = 3 * NA_WIDTH + 2 * DA_HEADS * 2 * DA_QK_DIM + DA_WIDTH

kernel_name = "hybrid_na_diffattn_adaln_block"


def rms_norm(x, g):
    xf = x.astype(jnp.float32)
    y = xf * lax.rsqrt(jnp.mean(xf * xf, axis=-1, keepdims=True) + EPS)
    return (y * g.astype(jnp.float32)).astype(x.dtype)


def neighbourhood_attention(q, k, v, rpb):
    B, S, H, dh = q.shape
    rows = S // GRID_W
    kr = min(NA_WIN_ROWS_MAX, rows)
    kc = NA_WIN_COLS
    qg = q.reshape(B, rows, GRID_W, H, dh)
    kg = k.reshape(B, rows, GRID_W, H, dh)
    vg = v.reshape(B, rows, GRID_W, H, dh)
    r = jnp.arange(rows)
    row_start = jnp.clip(r - kr // 2, 0, rows - kr)
    key_rows = row_start[:, None] + jnp.arange(kr)[None, :]
    k_band = kg[:, key_rows]
    v_band = vg[:, key_rows]
    cidx = jnp.arange(GRID_W)
    col_start = jnp.clip(cidx - kc // 2, 0, GRID_W - kc)
    in_win = (cidx[None, :] >= col_start[:, None]) & (cidx[None, :] < col_start[:, None] + kc)
    rel_r = key_rows - r[:, None] + NA_WIN_ROWS_MAX - 1
    rel_c = jnp.clip(cidx[None, :] - cidx[:, None], -(kc - 1), kc - 1) + NA_WIN_COLS - 1
    bias = rpb.astype(jnp.float32)[:, rel_r[:, None, :, None], rel_c[None, :, None, :]]
    s = jnp.einsum('brqhd,brjkhd->bhrqjk', qg, k_band).astype(jnp.float32) * (dh ** -0.5)
    s = s + bias[None]
    s = jnp.where(in_win[:, None, :], s, -jnp.inf)
    p = jax.nn.softmax(s.reshape(B, H, rows, GRID_W, kr * GRID_W), axis=-1)
    p = p.reshape(B, H, rows, GRID_W, kr, GRID_W).astype(v.dtype)
    o = jnp.einsum('bhrqjk,brjkhd->brqhd', p, v_band)
    return o.reshape(B, S, H * dh)


def t5_bucket(rel):
    nb = T5_BUCKETS // 2
    ret = jnp.where(rel > 0, nb, 0)
    n = jnp.abs(rel)
    max_exact = nb // 2
    nf = jnp.maximum(n, 1).astype(jnp.float32)
    large = max_exact + (jnp.log(nf / max_exact) / math.log(T5_MAX_DIST / max_exact)
                         * (nb - max_exact)).astype(jnp.int32)
    large = jnp.minimum(large, nb - 1)
    return ret + jnp.where(n < max_exact, n, large)


def differential_attention(q, k, v, t5_table, lam):
    B, S, H, _, dq = q.shape
    nblk = S // Q_BLOCK
    kpos = jnp.arange(S)
    q_blocks = jnp.moveaxis(q.reshape(B, nblk, Q_BLOCK, H, 2, dq), 1, 0)
    starts = jnp.arange(nblk) * Q_BLOCK
    table = t5_table.astype(jnp.float32)
    scale = dq ** -0.5

    def block_fn(args):
        qb, start = args
        qpos = start + jnp.arange(Q_BLOCK)
        bias = jnp.transpose(table[t5_bucket(kpos[None, :] - qpos[:, None])], (2, 0, 1))
        s = jnp.einsum('bqhmd,bkhmd->bhmqk', qb, k).astype(jnp.float32) * scale
        s = s + bias[None, :, None]
        p = jax.nn.softmax(s, axis=-1)
        a = (p[:, :, 0] - lam * p[:, :, 1]).astype(v.dtype)
        return jnp.einsum('bhqk,bkhd->bqhd', a, v)

    o = lax.map(block_fn, (q_blocks, starts))
    return jnp.moveaxis(o, 0, 1).reshape(B, S, H, v.shape[-1])


def setup_inputs(seed: int = 0) -> dict:
    key = jax.random.key(seed)
    ks = jax.random.split(key, 20)
    f32 = jnp.float32
    D = D_MODEL
    nrm = lambda k, shape, s: jax.random.normal(k, shape, f32) * s
    return {
        "x": nrm(ks[0], (BATCH, SEQ, D), 1.0),
        "c": nrm(ks[1], (BATCH, D), 1.0),
        "w_ada": nrm(ks[2], (DEPTH, D, 6 * D), 0.5 * D ** -0.5),
        "b_ada": nrm(ks[3], (DEPTH, 6 * D), 0.01),
        "norm1_g": 1.0 + nrm(ks[4], (DEPTH, D), 0.02),
        "w_in": nrm(ks[5], (DEPTH, D, PROJ_WIDTH), D ** -0.5),
        "na_rpb": nrm(ks[6], (DEPTH, NA_HEADS, 2 * NA_WIN_ROWS_MAX - 1, 2 * NA_WIN_COLS - 1), 0.5),
        "na_out_g": 1.0 + nrm(ks[7], (DEPTH, NA_WIDTH), 0.02),
        "da_lambda": nrm(ks[8], (DEPTH, 4, DA_QK_DIM), 0.1),
        "da_subln_g": 1.0 + nrm(ks[9], (DEPTH, HEAD_DIM), 0.02),
        "t5_table": nrm(ks[10], (T5_BUCKETS, DA_HEADS), 0.5),
        "w_out": nrm(ks[11], (DEPTH, MIX_WIDTH, D), MIX_WIDTH ** -0.5),
        "norm2_g": 1.0 + nrm(ks[12], (DEPTH, D), 0.02),
        "w_gate": nrm(ks[13], (DEPTH, D, D_FF), D ** -0.5),
        "w_up": nrm(ks[14], (DEPTH, D, D_FF), D ** -0.5),
        "w_down": nrm(ks[15], (DEPTH, D_FF, D), D_FF ** -0.5),
        "final_g": 1.0 + nrm(ks[16], (D,), 0.02),
    }


def reference(x, c, w_ada, b_ada, norm1_g, w_in, na_rpb, na_out_g, da_lambda, da_subln_g,
              t5_table, w_out, norm2_g, w_gate, w_up, w_down, final_g):
    B, S, D = x.shape
    cs = jax.nn.silu(c)
    o1 = NA_WIDTH
    o2 = o1 + NA_WIDTH
    o3 = o2 + NA_WIDTH
    o4 = o3 + DA_HEADS * 2 * DA_QK_DIM
    o5 = o4 + DA_HEADS * 2 * DA_QK_DIM
    for layer in range(DEPTH):
        mod = cs @ w_ada[layer] + b_ada[layer]
        sh1, sc1, g1, sh2, sc2, g2 = jnp.split(mod, 6, axis=-1)

        h = rms_norm(x, norm1_g[layer]) * (1 + sc1[:, None]) + sh1[:, None]
        proj = h @ w_in[layer]
        na_q = proj[..., :o1].reshape(B, S, NA_HEADS, HEAD_DIM)
        na_k = proj[..., o1:o2].reshape(B, S, NA_HEADS, HEAD_DIM)
        na_v = proj[..., o2:o3].reshape(B, S, NA_HEADS, HEAD_DIM)
        da_q = proj[..., o3:o4].reshape(B, S, DA_HEADS, 2, DA_QK_DIM)
        da_k = proj[..., o4:o5].reshape(B, S, DA_HEADS, 2, DA_QK_DIM)
        da_v = proj[..., o5:].reshape(B, S, DA_HEADS, HEAD_DIM)

        na_o = rms_norm(neighbourhood_attention(na_q, na_k, na_v, na_rpb[layer]), na_out_g[layer])

        lam_init = 0.8 - 0.6 * math.exp(-0.3 * layer)
        lp = da_lambda[layer].astype(jnp.float32)
        lam = jnp.exp(jnp.sum(lp[0] * lp[1])) - jnp.exp(jnp.sum(lp[2] * lp[3])) + lam_init
        da_o = differential_attention(da_q, da_k, da_v, t5_table, lam)
        da_o = (rms_norm(da_o, da_subln_g[layer]) * (1.0 - lam_init)).reshape(B, S, DA_WIDTH)

        mix = jnp.concatenate([na_o, da_o], axis=-1) @ w_out[layer]
        x = x + g1[:, None] * mix

        h2 = rms_norm(x, norm2_g[layer]) * (1 + sc2[:, None]) + sh2[:, None]
        ff = (jax.nn.silu(h2 @ w_gate[layer]) * (h2 @ w_up[layer])) @ w_down[layer]
        x = x + g2[:, None] * ff
    return rms_norm(x, final_g)
```

```python
import math

import jax
import jax.numpy as jnp
import numpy as np
from jax import lax
from jax.experimental import pallas as pl
from jax.experimental.pallas import tpu as pltpu

F32 = jnp.float32
BF16 = jnp.bfloat16

EPS = 1e-6
NEG = -1e30

GRID_W = 64
HEAD_DIM = 128
DA_QK_DIM = 64
NA_WIN_ROWS = 8
NA_WIN_COLS = 16
T5_BUCKETS = 32
T5_MAX_DIST = 128
LAM_INIT = 0.8 - 0.6 * math.exp(-0.3 * 0)

MIB = 1024 * 1024

NA_QROWS = 8
NA_KROWS = 16
NA_TQ = NA_QROWS * GRID_W
NA_TK = NA_KROWS * GRID_W

DA_TQ = 256

FFN_ROW_CHUNK = 256


def _dot(a, b):
    return jnp.dot(a, b, preferred_element_type=F32)


def _dot_nt(a, b):
    return lax.dot_general(a, b, (((1,), (1,)), ((), ())), preferred_element_type=F32)


def _sigmoid(x):
    return 1.0 / (1.0 + jnp.exp(-x))


def _ada_kernel(c_ref, w_ref, b_ref, o_ref):
    c = c_ref[...]
    cs = c * _sigmoid(c)
    o_ref[...] = _dot(cs.astype(BF16), w_ref[...].astype(BF16)) + b_ref[...]


def _ada_mod(c, w_ada, b_ada):
    B, D = c.shape
    N = w_ada.shape[1]
    tn = 1024
    rows = 8
    c_pad = jnp.zeros((rows, D), F32).at[:B].set(c)
    out = pl.pallas_call(
        _ada_kernel,
        out_shape=jax.ShapeDtypeStruct((rows, N), F32),
        grid=(N // tn,),
        in_specs=[
            pl.BlockSpec((rows, D), lambda j: (0, 0)),
            pl.BlockSpec((D, tn), lambda j: (0, j)),
            pl.BlockSpec((1, tn), lambda j: (0, j)),
        ],
        out_specs=pl.BlockSpec((rows, tn), lambda j: (0, j)),
        compiler_params=pltpu.CompilerParams(
            dimension_semantics=("arbitrary",), vmem_limit_bytes=40 * MIB),
        name="ada_mod",
    )(c_pad, w_ada, b_ada.reshape(1, N))
    return out[:B]


def _modulated_norm(x, g, sc, sh):
    ms = jnp.mean(x * x, axis=-1, keepdims=True)
    return (x * lax.rsqrt(ms + EPS) * g) * (1.0 + sc) + sh


def _inproj_kernel(x_ref, sh_ref, sc_ref, g_ref, w_ref, o_ref, h_ref, *, na_q_blk, da_q_blk,
                   na_scale, da_scale):
    j = pl.program_id(1)

    @pl.when(j == 0)
    def _():
        h_ref[...] = _modulated_norm(x_ref[...], g_ref[...], sc_ref[0], sh_ref[0]).astype(BF16)

    acc = _dot(h_ref[...], w_ref[...])
    scale = jnp.where(j == na_q_blk, na_scale, jnp.where(j == da_q_blk, da_scale, 1.0))
    o_ref[...] = (acc * scale).astype(BF16)


def _in_proj(x2d, mod3, norm_g, w_in_bf, *, batch, seq, na_width, da_q_off):
    M, D = x2d.shape
    N = w_in_bf.shape[1]
    tm, tn = 1024, 1024
    per_b = seq // tm
    kern = lambda *a: _inproj_kernel(
        *a, na_q_blk=0, da_q_blk=da_q_off // tn,
        na_scale=HEAD_DIM ** -0.5, da_scale=DA_QK_DIM ** -0.5)
    assert na_width == tn
    return pl.pallas_call(
        kern,
        out_shape=jax.ShapeDtypeStruct((M, N), BF16),
        grid=(M // tm, N // tn),
        in_specs=[
            pl.BlockSpec((tm, D), lambda i, j: (i, 0)),
            pl.BlockSpec((1, 1, D), lambda i, j: ((i // per_b) * 6 + 0, 0, 0)),
            pl.BlockSpec((1, 1, D), lambda i, j: ((i // per_b) * 6 + 1, 0, 0)),
            pl.BlockSpec((1, D), lambda i, j: (0, 0)),
            pl.BlockSpec((D, tn), lambda i, j: (0, j)),
        ],
        out_specs=pl.BlockSpec((tm, tn), lambda i, j: (i, j)),
        scratch_shapes=[pltpu.VMEM((tm, D), BF16)],
        compiler_params=pltpu.CompilerParams(
            dimension_semantics=("arbitrary", "arbitrary"), vmem_limit_bytes=48 * MIB),
        name="in_proj",
    )(x2d, mod3, mod3, norm_g, w_in_bf)


def _na_table_kernel(rpb_ref, o_ref, *, n_rel_rows, n_rel_cols):
    h = pl.program_id(0)
    shape = (GRID_W, 2 * GRID_W)
    qc = lax.broadcasted_iota(jnp.int32, shape, 0)
    lane = lax.broadcasted_iota(jnp.int32, shape, 1)
    upper = lane >= GRID_W
    kc = jnp.where(upper, lane - GRID_W, lane)
    dc = jnp.clip(kc - qc, -(NA_WIN_COLS - 1), NA_WIN_COLS - 1) + NA_WIN_COLS - 1
    c0 = jnp.clip(qc - NA_WIN_COLS // 2, 0, GRID_W - NA_WIN_COLS)
    in_win = (kc >= c0) & (kc < c0 + NA_WIN_COLS)
    base = h * (n_rel_rows * n_rel_cols)
    for e in range(2 * NA_WIN_ROWS):
        val = jnp.full(shape, NEG, F32)
        for d in range(n_rel_cols):
            r_lo, r_hi = e - 1, e
            lo = rpb_ref[base + r_lo * n_rel_cols + d] if 0 <= r_lo < n_rel_rows else NEG
            hi = rpb_ref[base + r_hi * n_rel_cols + d] if 0 <= r_hi < n_rel_rows else NEG
            val = jnp.where(dc == d, jnp.where(upper, hi, lo), val)
        o_ref[0, e] = jnp.where(in_win, val, NEG)


def _na_table(rpb):
    H, nr, nc = rpb.shape
    kern = lambda *a: _na_table_kernel(*a, n_rel_rows=nr, n_rel_cols=nc)
    return pl.pallas_call(
        kern,
        out_shape=jax.ShapeDtypeStruct((H, 2 * NA_WIN_ROWS, GRID_W, 2 * GRID_W), F32),
        grid=(H,),
        in_specs=[pl.BlockSpec(memory_space=pltpu.SMEM)],
        out_specs=pl.BlockSpec((1, 2 * NA_WIN_ROWS, GRID_W, 2 * GRID_W), lambda h: (h, 0, 0, 0)),
        compiler_params=pltpu.CompilerParams(dimension_semantics=("arbitrary",)),
        name="na_table",
    )(rpb.reshape(-1))


def _na_window_start_row(blk, rows):
    return min(max(blk * NA_QROWS - (NA_KROWS - NA_QROWS) // 2, 0), rows - NA_KROWS)


def _na_fill_bias(tp_ref, bm_ref, blk, rows):
    lane = lax.broadcasted_iota(jnp.int32, (GRID_W, 2 * GRID_W), 1)
    kr0 = _na_window_start_row(blk, rows)
    for i in range(NA_QROWS):
        qr = blk * NA_QROWS + i
        rs = min(max(qr - NA_WIN_ROWS // 2, 0), rows - NA_WIN_ROWS)
        for m in range(NA_KROWS // 2):
            ka = kr0 + 2 * m
            va = rs <= ka < rs + NA_WIN_ROWS
            vb = rs <= ka + 1 < rs + NA_WIN_ROWS
            e = ka - qr + NA_WIN_ROWS
            if va and vb:
                tile = tp_ref[0, e]
            elif va:
                tile = jnp.where(lane < GRID_W, tp_ref[0, e], NEG)
            elif vb:
                tile = jnp.where(lane >= GRID_W, tp_ref[0, e], NEG)
            else:
                tile = jnp.full((GRID_W, 2 * GRID_W), NEG, F32)
            bm_ref[i * GRID_W:(i + 1) * GRID_W, m * 2 * GRID_W:(m + 1) * 2 * GRID_W] = tile


def _na_kernel(q_ref, k_ref, v_ref, tp_ref, g_ref, o_ref, bm_ref, oacc_ref, *, rows, heads):
    bi = pl.program_id(1)
    h = pl.program_id(2)
    n_blk = rows // NA_QROWS

    for blk in range(n_blk):
        @pl.when(bi == blk)
        def _(blk=blk):
            _na_fill_bias(tp_ref, bm_ref, blk, rows)

    start_row = jnp.clip(bi * NA_QROWS - (NA_KROWS - NA_QROWS) // 2, 0, rows - NA_KROWS)
    start = pl.multiple_of(start_row * GRID_W, 4 * GRID_W)
    kw = k_ref[pl.ds(start, NA_TK), :]
    vw = v_ref[pl.ds(start, NA_TK), :]
    s = _dot_nt(q_ref[...], kw) + bm_ref[...]
    m = jnp.max(s, axis=-1, keepdims=True)
    p = jnp.exp(s - m)
    l = jnp.sum(p, axis=-1, keepdims=True)
    oacc_ref[h] = _dot(p.astype(BF16), vw) * (1.0 / l)

    @pl.when(h == heads - 1)
    def _():
        ss = jnp.zeros((NA_TQ, 1), F32)
        for k in range(heads):
            o = oacc_ref[k]
            ss = ss + jnp.sum(o * o, axis=-1, keepdims=True)
        rinv = lax.rsqrt(ss / (heads * HEAD_DIM) + EPS)
        for k in range(heads):
            sl = slice(k * HEAD_DIM, (k + 1) * HEAD_DIM)
            o_ref[:, sl] = (oacc_ref[k] * rinv * g_ref[:, sl]).astype(BF16)


def _na_attn(proj, tp, out_g, *, batch, seq, heads, q_blk0, k_blk0, v_blk0):
    rows = seq // GRID_W
    n_blk = rows // NA_QROWS
    kern = lambda *a: _na_kernel(*a, rows=rows, heads=heads)
    return pl.pallas_call(
        kern,
        out_shape=jax.ShapeDtypeStruct((batch * seq, heads * HEAD_DIM), BF16),
        grid=(batch, n_blk, heads),
        in_specs=[
            pl.BlockSpec((NA_TQ, HEAD_DIM), lambda b, bi, h: (b * n_blk + bi, q_blk0 + h)),
            pl.BlockSpec((seq, HEAD_DIM), lambda b, bi, h: (b, k_blk0 + h)),
            pl.BlockSpec((seq, HEAD_DIM), lambda b, bi, h: (b, v_blk0 + h)),
            pl.BlockSpec((1, 2 * NA_WIN_ROWS, GRID_W, 2 * GRID_W), lambda b, bi, h: (h, 0, 0, 0)),
            pl.BlockSpec((1, heads * HEAD_DIM), lambda b, bi, h: (0, 0)),
        ],
        out_specs=pl.BlockSpec((NA_TQ, heads * HEAD_DIM), lambda b, bi, h: (b * n_blk + bi, 0)),
        scratch_shapes=[
            pltpu.VMEM((NA_TQ, NA_TK), F32),
            pltpu.VMEM((heads, NA_TQ, HEAD_DIM), F32),
        ],
        compiler_params=pltpu.CompilerParams(
            dimension_semantics=("arbitrary", "arbitrary", "arbitrary"),
            vmem_limit_bytes=48 * MIB),
        name="na_attn",
    )(proj, proj, proj, tp, out_g)


def _t5_bucket(rel):
    nb = T5_BUCKETS // 2
    ret = jnp.where(rel > 0, nb, 0)
    n = jnp.abs(rel)
    max_exact = nb // 2
    nf = jnp.maximum(n, 1).astype(jnp.float32)
    large = max_exact + (jnp.log(nf / max_exact) / math.log(T5_MAX_DIST / max_exact)
                         * (nb - max_exact)).astype(jnp.int32)
    large = jnp.minimum(large, nb - 1)
    return ret + jnp.where(n < max_exact, n, large)


def _da_table_kernel(bucket_ref, t5_ref, o_ref, *, heads):
    h = pl.program_id(0)
    shape = (DA_TQ, DA_TQ)
    nb = T5_BUCKETS // 2
    o_ref[0, 0] = jnp.full(shape, t5_ref[(nb - 1) * heads + h], F32)
    o_ref[0, 4] = jnp.full(shape, t5_ref[(T5_BUCKETS - 1) * heads + h], F32)
    for s in range(3):
        bk = bucket_ref[s]
        val = jnp.zeros(shape, F32)
        for bkt in range(T5_BUCKETS):
            val = jnp.where(bk == bkt, t5_ref[bkt * heads + h], val)
        o_ref[0, s + 1] = val


def _da_table(t5_table):
    heads = t5_table.shape[1]
    assert DA_TQ >= T5_MAX_DIST
    i = jnp.arange(DA_TQ)[:, None]
    c = jnp.arange(DA_TQ)[None, :]
    bucket = jnp.stack([_t5_bucket((dj * DA_TQ + c) - i) for dj in (-1, 0, 1)]).astype(jnp.int32)
    kern = lambda *a: _da_table_kernel(*a, heads=heads)
    return pl.pallas_call(
        kern,
        out_shape=jax.ShapeDtypeStruct((heads, 5, DA_TQ, DA_TQ), F32),
        grid=(heads,),
        in_specs=[
            pl.BlockSpec((3, DA_TQ, DA_TQ), lambda h: (0, 0, 0)),
            pl.BlockSpec(memory_space=pltpu.SMEM),
        ],
        out_specs=pl.BlockSpec((1, 5, DA_TQ, DA_TQ), lambda h: (h, 0, 0, 0)),
        compiler_params=pltpu.CompilerParams(dimension_semantics=("arbitrary",)),
        name="da_table",
    )(bucket, t5_table.reshape(-1))


def _da_kernel(q_ref, k_ref, v_ref, tb_ref, lam_ref, g_ref, o_ref, *, seq):
    qi = pl.program_id(2)
    n_kt = seq // DA_TQ

    lp = lam_ref[...]
    t1 = jnp.sum(lp[0:1] * lp[1:2], axis=-1, keepdims=True)
    t2 = jnp.sum(lp[2:3] * lp[3:4], axis=-1, keepdims=True)
    lam = jnp.exp(t1) - jnp.exp(t2) + LAM_INIT

    q = q_ref[...]
    lane = lax.broadcasted_iota(jnp.int32, q.shape, 1)
    zero = jnp.zeros_like(q)
    q_maps = (jnp.where(lane < DA_QK_DIM, q, zero), jnp.where(lane >= DA_QK_DIM, q, zero))

    def softmax_map(qz):
        chunks = []
        for j in range(n_kt):
            slab = jnp.clip(j - qi, -2, 2) + 2
            chunks.append(_dot_nt(qz, k_ref[j * DA_TQ:(j + 1) * DA_TQ, :]) + tb_ref[0, slab])
        s = jnp.concatenate(chunks, axis=1)
        m = jnp.max(s, axis=-1, keepdims=True)
        p = jnp.exp(s - m)
        return p, jnp.sum(p, axis=-1, keepdims=True)

    p1, l1 = softmax_map(q_maps[0])
    p2, l2 = softmax_map(q_maps[1])
    a = p1 * (1.0 / l1) - p2 * (lam / l2)
    o = _dot(a.astype(BF16), v_ref[...])
    ms = jnp.mean(o * o, axis=-1, keepdims=True)
    y = (o * lax.rsqrt(ms + EPS) * g_ref[...]) * (1.0 - LAM_INIT)
    o_ref[...] = y.astype(BF16)


def _da_attn(proj, tb, da_lambda, subln_g, *, batch, seq, heads, q_blk0, k_blk0, v_blk0):
    n_qt = seq // DA_TQ
    kern = lambda *a: _da_kernel(*a, seq=seq)
    return pl.pallas_call(
        kern,
        out_shape=jax.ShapeDtypeStruct((batch * seq, heads * HEAD_DIM), BF16),
        grid=(batch, heads, n_qt),
        in_specs=[
            pl.BlockSpec((DA_TQ, HEAD_DIM), lambda b, h, qi: (b * n_qt + qi, q_blk0 + h)),
            pl.BlockSpec((seq, HEAD_DIM), lambda b, h, qi: (b, k_blk0 + h)),
            pl.BlockSpec((seq, HEAD_DIM), lambda b, h, qi: (b, v_blk0 + h)),
            pl.BlockSpec((1, 5, DA_TQ, DA_TQ), lambda b, h, qi: (h, 0, 0, 0)),
            pl.BlockSpec(da_lambda.shape, lambda b, h, qi: (0, 0)),
            pl.BlockSpec((1, HEAD_DIM), lambda b, h, qi: (0, 0)),
        ],
        out_specs=pl.BlockSpec((DA_TQ, HEAD_DIM), lambda b, h, qi: (b * n_qt + qi, h)),
        compiler_params=pltpu.CompilerParams(
            dimension_semantics=("arbitrary", "arbitrary", "arbitrary"),
            vmem_limit_bytes=48 * MIB),
        name="da_attn",
    )(proj, proj, proj, tb, da_lambda, subln_g)


def _outproj_kernel(na_ref, da_ref, wa_ref, wb_ref, x_ref, g1_ref, o_ref):
    mix = _dot(na_ref[...], wa_ref[...]) + _dot(da_ref[...], wb_ref[...])
    o_ref[...] = x_ref[...] + g1_ref[0] * mix


def _out_proj(na_o, da_o, w_out_bf, x2d, mod3, *, seq):
    M, D = x2d.shape
    Kh = na_o.shape[1]
    tm, tn = 1024, 1024
    per_b = seq // tm
    return pl.pallas_call(
        _outproj_kernel,
        out_shape=jax.ShapeDtypeStruct((M, D), F32),
        grid=(M // tm, D // tn),
        in_specs=[
            pl.BlockSpec((tm, Kh), lambda i, j: (i, 0)),
            pl.BlockSpec((tm, Kh), lambda i, j: (i, 0)),
            pl.BlockSpec((Kh, tn), lambda i, j: (0, j)),
            pl.BlockSpec((Kh, tn), lambda i, j: (1, j)),
            pl.BlockSpec((tm, tn), lambda i, j: (i, j)),
            pl.BlockSpec((1, 1, tn), lambda i, j: ((i // per_b) * 6 + 2, 0, j)),
        ],
        out_specs=pl.BlockSpec((tm, tn), lambda i, j: (i, j)),
        compiler_params=pltpu.CompilerParams(
            dimension_semantics=("arbitrary", "arbitrary"), vmem_limit_bytes=48 * MIB),
        name="out_proj",
    )(na_o, da_o, w_out_bf, w_out_bf, x2d, mod3)


def _ffn_kernel(x_ref, sh_ref, sc_ref, g2_ref, ng_ref, fg_ref, wg_ref, wu_ref, wd_ref, o_ref, h_ref,
                *, n_f):
    f = pl.program_id(1)
    n_chunks = x_ref.shape[0] // FFN_ROW_CHUNK

    def chunk_rows(r):
        return pl.ds(pl.multiple_of(r * FFN_ROW_CHUNK, FFN_ROW_CHUNK), FFN_ROW_CHUNK)

    @pl.when(f == 0)
    def _():
        def body(r, carry):
            rows = chunk_rows(r)
            h = _modulated_norm(x_ref[rows, :], ng_ref[...], sc_ref[0], sh_ref[0])
            h_ref[rows, :] = h.astype(BF16)
            o_ref[rows, :] = jnp.zeros((FFN_ROW_CHUNK, o_ref.shape[1]), F32)
            return carry
        lax.fori_loop(0, n_chunks, body, 0)

    def body(r, carry):
        rows = chunk_rows(r)
        h = h_ref[rows, :]
        g = _dot(h, wg_ref[...])
        u = _dot(h, wu_ref[...])
        a = (g * _sigmoid(g) * u).astype(BF16)
        o_ref[rows, :] += _dot(a, wd_ref[...])
        return carry
    lax.fori_loop(0, n_chunks, body, 0)

    @pl.when(f == n_f - 1)
    def _():
        def body(r, carry):
            rows = chunk_rows(r)
            x2 = x_ref[rows, :] + g2_ref[0] * o_ref[rows, :]
            ms = jnp.mean(x2 * x2, axis=-1, keepdims=True)
            o_ref[rows, :] = x2 * lax.rsqrt(ms + EPS) * fg_ref[...]
            return carry
        lax.fori_loop(0, n_chunks, body, 0)


def _ffn(x1, mod3, norm_g, final_g, wg_bf, wu_bf, wd_bf, *, seq):
    M, D = x1.shape
    F = wg_bf.shape[1]
    tm, tf = 1024, 512
    per_b = seq // tm
    n_f = F // tf
    kern = lambda *a: _ffn_kernel(*a, n_f=n_f)
    return pl.pallas_call(
        kern,
        out_shape=jax.ShapeDtypeStruct((M, D), F32),
        grid=(M // tm, n_f),
        in_specs=[
            pl.BlockSpec((tm, D), lambda i, f: (i, 0)),
            pl.BlockSpec((1, 1, D), lambda i, f: ((i // per_b) * 6 + 3, 0, 0)),
            pl.BlockSpec((1, 1, D), lambda i, f: ((i // per_b) * 6 + 4, 0, 0)),
            pl.BlockSpec((1, 1, D), lambda i, f: ((i // per_b) * 6 + 5, 0, 0)),
            pl.BlockSpec((1, D), lambda i, f: (0, 0)),
            pl.BlockSpec((1, D), lambda i, f: (0, 0)),
            pl.BlockSpec((D, tf), lambda i, f: (0, f)),
            pl.BlockSpec((D, tf), lambda i, f: (0, f)),
            pl.BlockSpec((tf, D), lambda i, f: (f, 0)),
        ],
        out_specs=pl.BlockSpec((tm, D), lambda i, f: (i, 0)),
        scratch_shapes=[pltpu.VMEM((tm, D), BF16)],
        compiler_params=pltpu.CompilerParams(
            dimension_semantics=("arbitrary", "arbitrary"), vmem_limit_bytes=60 * MIB),
        name="ffn",
    )(x1, mod3, mod3, mod3, norm_g, final_g, wg_bf, wu_bf, wd_bf)


def kernel(x, c, w_ada, b_ada, norm1_g, w_in, na_rpb, na_out_g, da_lambda, da_subln_g, t5_table,
           w_out, norm2_g, w_gate, w_up, w_down, final_g):
    B, S, D = x.shape
    na_width = na_rpb.shape[1] * HEAD_DIM
    na_heads = na_rpb.shape[1]
    da_heads = t5_table.shape[1]
    assert w_ada.shape[0] == 1, "single layer"

    mod = _ada_mod(c, w_ada[0], b_ada[0])
    mod3 = mod.reshape(B * 6, 1, D)
    x2d = x.reshape(B * S, D)

    proj = _in_proj(x2d, mod3, norm1_g, w_in[0].astype(BF16), batch=B, seq=S,
                    na_width=na_width, da_q_off=3 * na_width)
    nblk = na_width // HEAD_DIM
    na_o = _na_attn(proj, _na_table(na_rpb[0]), na_out_g, batch=B, seq=S, heads=na_heads,
                    q_blk0=0, k_blk0=nblk, v_blk0=2 * nblk)
    da_o = _da_attn(proj, _da_table(t5_table), da_lambda[0], da_subln_g, batch=B, seq=S,
                    heads=da_heads, q_blk0=3 * nblk, k_blk0=3 * nblk + da_heads,
                    v_blk0=3 * nblk + 2 * da_heads)
    x1 = _out_proj(na_o, da_o, w_out[0].astype(BF16), x2d, mod3, seq=S)
    out = _ffn(x1, mod3, norm2_g, final_g.reshape(1, D), w_gate[0].astype(BF16), w_up[0].astype(BF16),
               w_down[0].astype(BF16), seq=S)
    return out.reshape(B, S, D)
```

```python
import math

import jax
import jax.numpy as jnp
from jax import lax
from jax.experimental import pallas as pl
from jax.experimental.pallas import tpu as pltpu

F32 = jnp.float32
BF16 = jnp.bfloat16

EPS = 1e-6
NEG = -1e30

GRID_W = 64
HEAD_DIM = 128
DA_QK_DIM = 64
NA_WIN_ROWS = 8
NA_WIN_COLS = 16
T5_BUCKETS = 32
T5_MAX_DIST = 128
LAM_INIT = 0.8 - 0.6 * math.exp(-0.3 * 0)
LOG2E = math.log2(math.e)

MIB = 1024 * 1024

NA_QROWS = 8
NA_KROWS = 16
NA_TQ = NA_QROWS * GRID_W
NA_TK = NA_KROWS * GRID_W

DA_TQ = 256

FFN_ROW_CHUNK = 256


def _dot(a, b):
    return jnp.dot(a, b, preferred_element_type=F32)


def _dot_nt(a, b):
    return lax.dot_general(a, b, (((1,), (1,)), ((), ())), preferred_element_type=F32)


def _sigmoid(x):
    return 1.0 / (1.0 + jnp.exp(-x))


def _ada_kernel(c_ref, w_ref, b_ref, o_ref):
    c = c_ref[...]
    cs = c * _sigmoid(c)
    o_ref[...] = _dot(cs.astype(BF16), w_ref[...].astype(BF16)) + b_ref[...]


def _ada_mod(c, w_ada, b_ada):
    B, D = c.shape
    N = w_ada.shape[1]
    tn = 1024
    rows = 8
    c_pad = jnp.zeros((rows, D), F32).at[:B].set(c)
    out = pl.pallas_call(
        _ada_kernel,
        out_shape=jax.ShapeDtypeStruct((rows, N), F32),
        grid=(N // tn,),
        in_specs=[
            pl.BlockSpec((rows, D), lambda j: (0, 0)),
            pl.BlockSpec((D, tn), lambda j: (0, j)),
            pl.BlockSpec((1, tn), lambda j: (0, j)),
        ],
        out_specs=pl.BlockSpec((rows, tn), lambda j: (0, j)),
        compiler_params=pltpu.CompilerParams(
            dimension_semantics=("arbitrary",), vmem_limit_bytes=40 * MIB),
        name="ada_mod",
    )(c_pad, w_ada, b_ada.reshape(1, N))
    return out[:B]


def _modulated_norm(x, g, sc, sh):
    ms = jnp.mean(x * x, axis=-1, keepdims=True)
    return (x * lax.rsqrt(ms + EPS) * g) * (1.0 + sc) + sh


def _inproj_kernel(x_ref, sh_ref, sc_ref, g_ref, w_ref, o_ref, h_ref, *, na_q_blk, da_q_blk,
                   na_scale, da_scale):
    j = pl.program_id(1)

    @pl.when(j == 0)
    def _():
        h_ref[...] = _modulated_norm(x_ref[...], g_ref[...], sc_ref[0], sh_ref[0]).astype(BF16)

    acc = _dot(h_ref[...], w_ref[...])
    scale = jnp.where(j == na_q_blk, na_scale, jnp.where(j == da_q_blk, da_scale, 1.0))
    acc = acc * scale
    for k in range(o_ref.shape[0]):
        o_ref[k] = acc[:, k * HEAD_DIM:(k + 1) * HEAD_DIM].astype(BF16)


def _in_proj(x2d, mod3, norm_g, w_in_bf, *, seq, na_width, da_q_off):
    M, D = x2d.shape
    N = w_in_bf.shape[1]
    tm, tn = 1024, 1024
    per_b = seq // tm
    kern = lambda *a: _inproj_kernel(
        *a, na_q_blk=0, da_q_blk=da_q_off // tn,
        na_scale=HEAD_DIM ** -0.5 * LOG2E, da_scale=DA_QK_DIM ** -0.5 * LOG2E)
    assert na_width == tn
    return pl.pallas_call(
        kern,
        out_shape=jax.ShapeDtypeStruct((N // HEAD_DIM, M, HEAD_DIM), BF16),
        grid=(M // tm, N // tn),
        in_specs=[
            pl.BlockSpec((tm, D), lambda i, j: (i, 0)),
            pl.BlockSpec((1, 1, D), lambda i, j: ((i // per_b) * 6 + 0, 0, 0)),
            pl.BlockSpec((1, 1, D), lambda i, j: ((i // per_b) * 6 + 1, 0, 0)),
            pl.BlockSpec((1, D), lambda i, j: (0, 0)),
            pl.BlockSpec((D, tn), lambda i, j: (0, j)),
        ],
        out_specs=pl.BlockSpec((tn // HEAD_DIM, tm, HEAD_DIM), lambda i, j: (j, i, 0)),
        scratch_shapes=[pltpu.VMEM((tm, D), BF16)],
        compiler_params=pltpu.CompilerParams(
            dimension_semantics=("arbitrary", "arbitrary"), vmem_limit_bytes=48 * MIB),
        name="in_proj",
    )(x2d, mod3, mod3, norm_g, w_in_bf)


def _na_table_kernel(rpb_ref, o_ref, *, n_rel_rows, n_rel_cols):
    h = pl.program_id(0)
    shape = (GRID_W, 2 * GRID_W)
    qc = lax.broadcasted_iota(jnp.int32, shape, 0)
    lane = lax.broadcasted_iota(jnp.int32, shape, 1)
    upper = lane >= GRID_W
    kc = jnp.where(upper, lane - GRID_W, lane)
    dc = jnp.clip(kc - qc, -(NA_WIN_COLS - 1), NA_WIN_COLS - 1) + NA_WIN_COLS - 1
    c0 = jnp.clip(qc - NA_WIN_COLS // 2, 0, GRID_W - NA_WIN_COLS)
    in_win = (kc >= c0) & (kc < c0 + NA_WIN_COLS)
    base = h * (n_rel_rows * n_rel_cols)
    for e in range(2 * NA_WIN_ROWS):
        val = jnp.full(shape, NEG, F32)
        for d in range(n_rel_cols):
            r_lo, r_hi = e - 1, e
            lo = rpb_ref[base + r_lo * n_rel_cols + d] * LOG2E if 0 <= r_lo < n_rel_rows else NEG
            hi = rpb_ref[base + r_hi * n_rel_cols + d] * LOG2E if 0 <= r_hi < n_rel_rows else NEG
            val = jnp.where(dc == d, jnp.where(upper, hi, lo), val)
        o_ref[0, e] = jnp.where(in_win, val, NEG)


def _na_table(rpb):
    H, nr, nc = rpb.shape
    kern = lambda *a: _na_table_kernel(*a, n_rel_rows=nr, n_rel_cols=nc)
    return pl.pallas_call(
        kern,
        out_shape=jax.ShapeDtypeStruct((H, 2 * NA_WIN_ROWS, GRID_W, 2 * GRID_W), F32),
        grid=(H,),
        in_specs=[pl.BlockSpec(memory_space=pltpu.SMEM)],
        out_specs=pl.BlockSpec((1, 2 * NA_WIN_ROWS, GRID_W, 2 * GRID_W), lambda h: (h, 0, 0, 0)),
        compiler_params=pltpu.CompilerParams(dimension_semantics=("arbitrary",)),
        name="na_table",
    )(rpb.reshape(-1))


def _na_window_start_row(blk, rows):
    return min(max(blk * NA_QROWS - (NA_KROWS - NA_QROWS) // 2, 0), rows - NA_KROWS)


def _na_block(q, kw, vw, tp_ref, blk, rows):
    lane = lax.broadcasted_iota(jnp.int32, (GRID_W, 2 * GRID_W), 1)
    kr0 = _na_window_start_row(blk, rows)
    s = _dot_nt(q, kw)
    p_rows, inv_l = [], []
    for i in range(NA_QROWS):
        qr = blk * NA_QROWS + i
        rs = min(max(qr - NA_WIN_ROWS // 2, 0), rows - NA_WIN_ROWS)
        tiles = {}
        for m in range(NA_KROWS // 2):
            ka = kr0 + 2 * m
            va = rs <= ka < rs + NA_WIN_ROWS
            vb = rs <= ka + 1 < rs + NA_WIN_ROWS
            if not (va or vb):
                continue
            e = ka - qr + NA_WIN_ROWS
            t = s[i * GRID_W:(i + 1) * GRID_W, m * 2 * GRID_W:(m + 1) * 2 * GRID_W] + tp_ref[0, e]
            if not vb:
                t = jnp.where(lane < GRID_W, t, NEG)
            elif not va:
                t = jnp.where(lane >= GRID_W, t, NEG)
            tiles[m] = t
        ts = list(tiles.values())
        mx = ts[0]
        for t in ts[1:]:
            mx = jnp.maximum(mx, t)
        mx = jnp.max(mx, axis=-1, keepdims=True)
        ps = {m: jnp.exp2(t - mx) for m, t in tiles.items()}
        tot = None
        for pt in ps.values():
            tot = pt if tot is None else tot + pt
        inv_l.append(1.0 / jnp.sum(tot, axis=-1, keepdims=True))
        zero = jnp.zeros((GRID_W, 2 * GRID_W), BF16)
        p_rows.append(jnp.concatenate(
            [ps[m].astype(BF16) if m in ps else zero for m in range(NA_KROWS // 2)], axis=1))
    p = jnp.concatenate(p_rows, axis=0)
    return _dot(p, vw) * jnp.concatenate(inv_l, axis=0)


def _na_kernel(q_ref, k_ref, v_ref, tp_ref, o_ref, *, rows):
    for blk in range(rows // NA_QROWS):
        start = _na_window_start_row(blk, rows) * GRID_W
        o = _na_block(q_ref[blk * NA_TQ:(blk + 1) * NA_TQ, :], k_ref[start:start + NA_TK, :],
                      v_ref[start:start + NA_TK, :], tp_ref, blk, rows)
        o_ref[blk * NA_TQ:(blk + 1) * NA_TQ, :] = o.astype(BF16)


def _na_attn(proj_hm, tp, *, batch, seq, heads, q_blk0, k_blk0, v_blk0):
    rows = seq // GRID_W
    kern = lambda *a: _na_kernel(*a, rows=rows)
    head_blk = lambda blk0: pl.BlockSpec((None, seq, HEAD_DIM), lambda b, h: (blk0 + h, b, 0))
    return pl.pallas_call(
        kern,
        out_shape=jax.ShapeDtypeStruct((batch * seq, heads * HEAD_DIM), BF16),
        grid=(batch, heads),
        in_specs=[
            head_blk(q_blk0), head_blk(k_blk0), head_blk(v_blk0),
            pl.BlockSpec((1, 2 * NA_WIN_ROWS, GRID_W, 2 * GRID_W), lambda b, h: (h, 0, 0, 0)),
        ],
        out_specs=pl.BlockSpec((seq, HEAD_DIM), lambda b, h: (b, h)),
        compiler_params=pltpu.CompilerParams(
            dimension_semantics=("arbitrary", "arbitrary"), vmem_limit_bytes=48 * MIB),
        name="na_attn",
    )(proj_hm, proj_hm, proj_hm, tp)


def _t5_bucket(rel):
    nb = T5_BUCKETS // 2
    ret = jnp.where(rel > 0, nb, 0)
    n = jnp.abs(rel)
    max_exact = nb // 2
    nf = jnp.maximum(n, 1).astype(jnp.float32)
    large = max_exact + (jnp.log(nf / max_exact) / math.log(T5_MAX_DIST / max_exact)
                         * (nb - max_exact)).astype(jnp.int32)
    large = jnp.minimum(large, nb - 1)
    return ret + jnp.where(n < max_exact, n, large)


def _da_table_kernel(bucket_ref, t5_ref, o_ref, *, heads):
    h = pl.program_id(0)
    shape = (DA_TQ, DA_TQ)
    nb = T5_BUCKETS // 2
    o_ref[0, 0] = jnp.full(shape, t5_ref[(nb - 1) * heads + h] * LOG2E, F32)
    o_ref[0, 4] = jnp.full(shape, t5_ref[(T5_BUCKETS - 1) * heads + h] * LOG2E, F32)
    for s in range(3):
        bk = bucket_ref[s]
        val = jnp.zeros(shape, F32)
        for bkt in range(T5_BUCKETS):
            val = jnp.where(bk == bkt, t5_ref[bkt * heads + h] * LOG2E, val)
        o_ref[0, s + 1] = val


def _da_table(t5_table):
    heads = t5_table.shape[1]
    assert DA_TQ >= T5_MAX_DIST
    i = jnp.arange(DA_TQ)[:, None]
    c = jnp.arange(DA_TQ)[None, :]
    bucket = jnp.stack([_t5_bucket((dj * DA_TQ + c) - i) for dj in (-1, 0, 1)]).astype(jnp.int32)
    kern = lambda *a: _da_table_kernel(*a, heads=heads)
    return pl.pallas_call(
        kern,
        out_shape=jax.ShapeDtypeStruct((heads, 5, DA_TQ, DA_TQ), F32),
        grid=(heads,),
        in_specs=[
            pl.BlockSpec((3, DA_TQ, DA_TQ), lambda h: (0, 0, 0)),
            pl.BlockSpec(memory_space=pltpu.SMEM),
        ],
        out_specs=pl.BlockSpec((1, 5, DA_TQ, DA_TQ), lambda h: (h, 0, 0, 0)),
        compiler_params=pltpu.CompilerParams(dimension_semantics=("arbitrary",)),
        name="da_table",
    )(bucket, t5_table.reshape(-1))


def _da_kernel(q_ref, k_ref, v_ref, tb_ref, lam_ref, g_ref, o_ref, vext_ref, *, seq):
    qi = pl.program_id(2)
    n_kt = seq // DA_TQ

    @pl.when(qi == 0)
    def _():
        vext_ref[:, :HEAD_DIM] = v_ref[...]
        lane = lax.broadcasted_iota(jnp.int32, (seq, HEAD_DIM), 1)
        vext_ref[:, HEAD_DIM:] = jnp.where(lane == 0, 1.0, 0.0).astype(BF16)

    lp = lam_ref[...]
    t1 = jnp.sum(lp[0:1] * lp[1:2], axis=-1, keepdims=True)
    t2 = jnp.sum(lp[2:3] * lp[3:4], axis=-1, keepdims=True)
    lam = jnp.exp(t1) - jnp.exp(t2) + LAM_INIT

    q = q_ref[...]
    lane = lax.broadcasted_iota(jnp.int32, q.shape, 1)
    zero = jnp.zeros_like(q)
    bias = jnp.concatenate(
        [tb_ref[0, jnp.clip(j - qi, -2, 2) + 2] for j in range(n_kt)], axis=1)

    def softmax_pv(qz):
        s = _dot_nt(qz, k_ref[...]) + bias
        m = jnp.max(s, axis=-1, keepdims=True)
        p = jnp.exp2(s - m).astype(BF16)
        oe = _dot(p, vext_ref[...])
        return oe[:, :HEAD_DIM], oe[:, HEAD_DIM:HEAD_DIM + 1]

    o1, l1 = softmax_pv(jnp.where(lane < DA_QK_DIM, q, zero))
    o2, l2 = softmax_pv(jnp.where(lane >= DA_QK_DIM, q, zero))
    o = o1 * (1.0 / l1) - o2 * (lam / l2)
    ms = jnp.mean(o * o, axis=-1, keepdims=True)
    y = (o * lax.rsqrt(ms + EPS) * g_ref[...]) * (1.0 - LAM_INIT)
    o_ref[...] = y.astype(BF16)


def _da_attn(proj_hm, tb, da_lambda, subln_g, *, batch, seq, heads, q_blk0, k_blk0, v_blk0):
    n_qt = seq // DA_TQ
    kern = lambda *a: _da_kernel(*a, seq=seq)
    return pl.pallas_call(
        kern,
        out_shape=jax.ShapeDtypeStruct((batch * seq, heads * HEAD_DIM), BF16),
        grid=(batch, heads, n_qt),
        in_specs=[
            pl.BlockSpec((None, DA_TQ, HEAD_DIM), lambda b, h, qi: (q_blk0 + h, b * n_qt + qi, 0)),
            pl.BlockSpec((None, seq, HEAD_DIM), lambda b, h, qi: (k_blk0 + h, b, 0)),
            pl.BlockSpec((None, seq, HEAD_DIM), lambda b, h, qi: (v_blk0 + h, b, 0)),
            pl.BlockSpec((1, 5, DA_TQ, DA_TQ), lambda b, h, qi: (h, 0, 0, 0)),
            pl.BlockSpec(da_lambda.shape, lambda b, h, qi: (0, 0)),
            pl.BlockSpec((1, HEAD_DIM), lambda b, h, qi: (0, 0)),
        ],
        out_specs=pl.BlockSpec((DA_TQ, HEAD_DIM), lambda b, h, qi: (b * n_qt + qi, h)),
        scratch_shapes=[pltpu.VMEM((seq, 2 * HEAD_DIM), BF16)],
        compiler_params=pltpu.CompilerParams(
            dimension_semantics=("arbitrary", "arbitrary", "arbitrary"),
            vmem_limit_bytes=48 * MIB),
        name="da_attn",
    )(proj_hm, proj_hm, proj_hm, tb, da_lambda, subln_g)


def _outproj_kernel(na_ref, da_ref, nag_ref, wa_ref, wb_ref, x_ref, g1_ref, o_ref, nan_ref):
    j = pl.program_id(1)

    @pl.when(j == 0)
    def _():
        o = na_ref[...].astype(F32)
        ms = jnp.mean(o * o, axis=-1, keepdims=True)
        nan_ref[...] = (o * lax.rsqrt(ms + EPS) * nag_ref[...]).astype(BF16)

    mix = _dot(nan_ref[...], wa_ref[...]) + _dot(da_ref[...], wb_ref[...])
    o_ref[...] = x_ref[...] + g1_ref[0] * mix


def _out_proj(na_o, da_o, na_out_g, w_out_bf, x2d, mod3, *, seq):
    M, D = x2d.shape
    Kh = na_o.shape[1]
    tm, tn = 1024, 1024
    per_b = seq // tm
    return pl.pallas_call(
        _outproj_kernel,
        out_shape=jax.ShapeDtypeStruct((M, D), F32),
        grid=(M // tm, D // tn),
        in_specs=[
            pl.BlockSpec((tm, Kh), lambda i, j: (i, 0)),
            pl.BlockSpec((tm, Kh), lambda i, j: (i, 0)),
            pl.BlockSpec((1, Kh), lambda i, j: (0, 0)),
            pl.BlockSpec((Kh, tn), lambda i, j: (0, j)),
            pl.BlockSpec((Kh, tn), lambda i, j: (1, j)),
            pl.BlockSpec((tm, tn), lambda i, j: (i, j)),
            pl.BlockSpec((1, 1, tn), lambda i, j: ((i // per_b) * 6 + 2, 0, j)),
        ],
        out_specs=pl.BlockSpec((tm, tn), lambda i, j: (i, j)),
        scratch_shapes=[pltpu.VMEM((tm, Kh), BF16)],
        compiler_params=pltpu.CompilerParams(
            dimension_semantics=("arbitrary", "arbitrary"), vmem_limit_bytes=48 * MIB),
        name="out_proj",
    )(na_o, da_o, na_out_g, w_out_bf, w_out_bf, x2d, mod3)


def _ffn_kernel(x_ref, sh_ref, sc_ref, g2_ref, ng_ref, fg_ref, wg_ref, wu_ref, wd_ref, o_ref, h_ref,
                *, n_f):
    f = pl.program_id(1)
    n_chunks = x_ref.shape[0] // FFN_ROW_CHUNK

    def chunk_rows(r):
        return pl.ds(pl.multiple_of(r * FFN_ROW_CHUNK, FFN_ROW_CHUNK), FFN_ROW_CHUNK)

    @pl.when(f == 0)
    def _():
        def body(r, carry):
            rows = chunk_rows(r)
            h = _modulated_norm(x_ref[rows, :], ng_ref[...], sc_ref[0], sh_ref[0])
            h_ref[rows, :] = h.astype(BF16)
            o_ref[rows, :] = jnp.zeros((FFN_ROW_CHUNK, o_ref.shape[1]), F32)
            return carry
        lax.fori_loop(0, n_chunks, body, 0)

    def body(r, carry):
        rows = chunk_rows(r)
        h = h_ref[rows, :]
        g = _dot(h, wg_ref[...])
        u = _dot(h, wu_ref[...])
        a = (g * _sigmoid(g) * u).astype(BF16)
        o_ref[rows, :] += _dot(a, wd_ref[...])
        return carry
    lax.fori_loop(0, n_chunks, body, 0)

    @pl.when(f == n_f - 1)
    def _():
        def body(r, carry):
            rows = chunk_rows(r)
            x2 = x_ref[rows, :] + g2_ref[0] * o_ref[rows, :]
            ms = jnp.mean(x2 * x2, axis=-1, keepdims=True)
            o_ref[rows, :] = x2 * lax.rsqrt(ms + EPS) * fg_ref[...]
            return carry
        lax.fori_loop(0, n_chunks, body, 0)


def _ffn(x1, mod3, norm_g, final_g, wg_bf, wu_bf, wd_bf, *, seq):
    M, D = x1.shape
    F = wg_bf.shape[1]
    tm, tf = 1024, 512
    per_b = seq // tm
    n_f = F // tf
    kern = lambda *a: _ffn_kernel(*a, n_f=n_f)
    return pl.pallas_call(
        kern,
        out_shape=jax.ShapeDtypeStruct((M, D), F32),
        grid=(M // tm, n_f),
        in_specs=[
            pl.BlockSpec((tm, D), lambda i, f: (i, 0)),
            pl.BlockSpec((1, 1, D), lambda i, f: ((i // per_b) * 6 + 3, 0, 0)),
            pl.BlockSpec((1, 1, D), lambda i, f: ((i // per_b) * 6 + 4, 0, 0)),
            pl.BlockSpec((1, 1, D), lambda i, f: ((i // per_b) * 6 + 5, 0, 0)),
            pl.BlockSpec((1, D), lambda i, f: (0, 0)),
            pl.BlockSpec((1, D), lambda i, f: (0, 0)),
            pl.BlockSpec((D, tf), lambda i, f: (0, f)),
            pl.BlockSpec((D, tf), lambda i, f: (0, f)),
            pl.BlockSpec((tf, D), lambda i, f: (f, 0)),
        ],
        out_specs=pl.BlockSpec((tm, D), lambda i, f: (i, 0)),
        scratch_shapes=[pltpu.VMEM((tm, D), BF16)],
        compiler_params=pltpu.CompilerParams(
            dimension_semantics=("arbitrary", "arbitrary"), vmem_limit_bytes=60 * MIB),
        name="ffn",
    )(x1, mod3, mod3, mod3, norm_g, final_g, wg_bf, wu_bf, wd_bf)


def kernel(x, c, w_ada, b_ada, norm1_g, w_in, na_rpb, na_out_g, da_lambda, da_subln_g, t5_table,
           w_out, norm2_g, w_gate, w_up, w_down, final_g):
    B, S, D = x.shape
    na_heads = na_rpb.shape[1]
    da_heads = t5_table.shape[1]
    na_width = na_heads * HEAD_DIM
    assert w_ada.shape[0] == 1, "single layer"

    mod = _ada_mod(c, w_ada[0], b_ada[0])
    mod3 = mod.reshape(B * 6, 1, D)
    x2d = x.reshape(B * S, D)

    proj_hm = _in_proj(x2d, mod3, norm1_g, w_in[0].astype(BF16), seq=S,
                       na_width=na_width, da_q_off=3 * na_width)
    na_o = _na_attn(proj_hm, _na_table(na_rpb[0]), batch=B, seq=S, heads=na_heads,
                    q_blk0=0, k_blk0=na_heads, v_blk0=2 * na_heads)
    da_o = _da_attn(proj_hm, _da_table(t5_table), da_lambda[0], da_subln_g, batch=B, seq=S,
                    heads=da_heads, q_blk0=3 * na_heads, k_blk0=3 * na_heads + da_heads,
                    v_blk0=3 * na_heads + 2 * da_heads)
    x1 = _out_proj(na_o, da_o, na_out_g, w_out[0].astype(BF16), x2d, mod3, seq=S)
    out = _ffn(x1, mod3, norm2_g, final_g.reshape(1, D), w_gate[0].astype(BF16),
               w_up[0].astype(BF16), w_down[0].astype(BF16), seq=S)
    return out.reshape(B, S, D)
```

```python
import math

import jax
import jax.numpy as jnp
from jax import lax
from jax.experimental import pallas as pl
from jax.experimental.pallas import tpu as pltpu

F32 = jnp.float32
BF16 = jnp.bfloat16

EPS = 1e-6
NEG = -1e30

GRID_W = 64
HEAD_DIM = 128
DA_QK_DIM = 64
NA_WIN_ROWS = 8
NA_WIN_COLS = 16
T5_BUCKETS = 32
T5_MAX_DIST = 128
LAM_INIT = 0.8 - 0.6 * math.exp(-0.3 * 0)
LOG2E = math.log2(math.e)

MIB = 1024 * 1024

NA_QROWS = 8
NA_KROWS = 16
NA_TQ = NA_QROWS * GRID_W
NA_TK = NA_KROWS * GRID_W

DA_TQ = 256

FFN_ROW_CHUNK = 256


def _dot(a, b):
    return jnp.dot(a, b, preferred_element_type=F32)


def _dot_nt(a, b):
    return lax.dot_general(a, b, (((1,), (1,)), ((), ())), preferred_element_type=F32)


def _sigmoid(x):
    return 1.0 / (1.0 + jnp.exp(-x))


def _ada_kernel(c_ref, w_ref, b_ref, o_ref):
    c = c_ref[...]
    cs = c * _sigmoid(c)
    o_ref[...] = _dot(cs.astype(BF16), w_ref[...].astype(BF16)) + b_ref[...]


def _ada_mod(c, w_ada, b_ada):
    B, D = c.shape
    N = w_ada.shape[1]
    tn = 1024
    rows = 8
    c_pad = jnp.zeros((rows, D), F32).at[:B].set(c)
    out = pl.pallas_call(
        _ada_kernel,
        out_shape=jax.ShapeDtypeStruct((rows, N), F32),
        grid=(N // tn,),
        in_specs=[
            pl.BlockSpec((rows, D), lambda j: (0, 0)),
            pl.BlockSpec((D, tn), lambda j: (0, j)),
            pl.BlockSpec((1, tn), lambda j: (0, j)),
        ],
        out_specs=pl.BlockSpec((rows, tn), lambda j: (0, j)),
        compiler_params=pltpu.CompilerParams(
            dimension_semantics=("arbitrary",), vmem_limit_bytes=40 * MIB),
        name="ada_mod",
    )(c_pad, w_ada, b_ada.reshape(1, N))
    return out[:B]


def _modulated_norm(x, g, sc, sh):
    ms = jnp.mean(x * x, axis=-1, keepdims=True)
    return (x * lax.rsqrt(ms + EPS) * g) * (1.0 + sc) + sh


def _inproj_kernel(x_ref, sh_ref, sc_ref, g_ref, w_ref, o_ref, h_ref, *, na_q_blk, da_q_blk,
                   na_scale, da_scale):
    j = pl.program_id(1)

    @pl.when(j == 0)
    def _():
        h_ref[...] = _modulated_norm(x_ref[...], g_ref[...], sc_ref[0], sh_ref[0]).astype(BF16)

    acc = _dot(h_ref[...], w_ref[...])
    scale = jnp.where(j == na_q_blk, na_scale, jnp.where(j == da_q_blk, da_scale, 1.0))
    acc = acc * scale
    for k in range(o_ref.shape[0]):
        o_ref[k] = acc[:, k * HEAD_DIM:(k + 1) * HEAD_DIM].astype(BF16)


def _in_proj(x2d, mod3, norm_g, w_in_bf, *, seq, na_width, da_q_off):
    M, D = x2d.shape
    N = w_in_bf.shape[1]
    tm, tn = 1024, 1024
    per_b = seq // tm
    kern = lambda *a: _inproj_kernel(
        *a, na_q_blk=0, da_q_blk=da_q_off // tn,
        na_scale=HEAD_DIM ** -0.5 * LOG2E, da_scale=DA_QK_DIM ** -0.5 * LOG2E)
    assert na_width == tn
    return pl.pallas_call(
        kern,
        out_shape=jax.ShapeDtypeStruct((N // HEAD_DIM, M, HEAD_DIM), BF16),
        grid=(M // tm, N // tn),
        in_specs=[
            pl.BlockSpec((tm, D), lambda i, j: (i, 0)),
            pl.BlockSpec((1, 1, D), lambda i, j: ((i // per_b) * 6 + 0, 0, 0)),
            pl.BlockSpec((1, 1, D), lambda i, j: ((i // per_b) * 6 + 1, 0, 0)),
            pl.BlockSpec((1, D), lambda i, j: (0, 0)),
            pl.BlockSpec((D, tn), lambda i, j: (0, j)),
        ],
        out_specs=pl.BlockSpec((tn // HEAD_DIM, tm, HEAD_DIM), lambda i, j: (j, i, 0)),
        scratch_shapes=[pltpu.VMEM((tm, D), BF16)],
        compiler_params=pltpu.CompilerParams(
            dimension_semantics=("arbitrary", "arbitrary"), vmem_limit_bytes=48 * MIB),
        name="in_proj",
    )(x2d, mod3, mod3, norm_g, w_in_bf)


def _na_table_kernel(rpb_ref, o_ref, *, n_rel_rows, n_rel_cols):
    h = pl.program_id(0)
    shape = (GRID_W, 2 * GRID_W)
    qc = lax.broadcasted_iota(jnp.int32, shape, 0)
    lane = lax.broadcasted_iota(jnp.int32, shape, 1)
    upper = lane >= GRID_W
    kc = jnp.where(upper, lane - GRID_W, lane)
    dc = jnp.clip(kc - qc, -(NA_WIN_COLS - 1), NA_WIN_COLS - 1) + NA_WIN_COLS - 1
    c0 = jnp.clip(qc - NA_WIN_COLS // 2, 0, GRID_W - NA_WIN_COLS)
    in_win = (kc >= c0) & (kc < c0 + NA_WIN_COLS)
    base = h * (n_rel_rows * n_rel_cols)
    for e in range(2 * NA_WIN_ROWS):
        val = jnp.full(shape, NEG, F32)
        for d in range(n_rel_cols):
            r_lo, r_hi = e - 1, e
            lo = rpb_ref[base + r_lo * n_rel_cols + d] * LOG2E if 0 <= r_lo < n_rel_rows else NEG
            hi = rpb_ref[base + r_hi * n_rel_cols + d] * LOG2E if 0 <= r_hi < n_rel_rows else NEG
            val = jnp.where(dc == d, jnp.where(upper, hi, lo), val)
        o_ref[0, e] = jnp.where(in_win, val, NEG)


def _na_table(rpb):
    H, nr, nc = rpb.shape
    kern = lambda *a: _na_table_kernel(*a, n_rel_rows=nr, n_rel_cols=nc)
    return pl.pallas_call(
        kern,
        out_shape=jax.ShapeDtypeStruct((H, 2 * NA_WIN_ROWS, GRID_W, 2 * GRID_W), F32),
        grid=(H,),
        in_specs=[pl.BlockSpec(memory_space=pltpu.SMEM)],
        out_specs=pl.BlockSpec((1, 2 * NA_WIN_ROWS, GRID_W, 2 * GRID_W), lambda h: (h, 0, 0, 0)),
        compiler_params=pltpu.CompilerParams(dimension_semantics=("arbitrary",)),
        name="na_table",
    )(rpb.reshape(-1))


def _na_window_start_row(blk, rows):
    return min(max(blk * NA_QROWS - (NA_KROWS - NA_QROWS) // 2, 0), rows - NA_KROWS)


def _na_block(q, kw, vw, tp_ref, blk, rows):
    lane = lax.broadcasted_iota(jnp.int32, (GRID_W, 2 * GRID_W), 1)
    kr0 = _na_window_start_row(blk, rows)
    s = _dot_nt(q, kw)
    p_rows, inv_l = [], []
    for i in range(NA_QROWS):
        qr = blk * NA_QROWS + i
        rs = min(max(qr - NA_WIN_ROWS // 2, 0), rows - NA_WIN_ROWS)
        tiles = {}
        for m in range(NA_KROWS // 2):
            ka = kr0 + 2 * m
            va = rs <= ka < rs + NA_WIN_ROWS
            vb = rs <= ka + 1 < rs + NA_WIN_ROWS
            if not (va or vb):
                continue
            e = ka - qr + NA_WIN_ROWS
            t = s[i * GRID_W:(i + 1) * GRID_W, m * 2 * GRID_W:(m + 1) * 2 * GRID_W] + tp_ref[0, e]
            if not vb:
                t = jnp.where(lane < GRID_W, t, NEG)
            elif not va:
                t = jnp.where(lane >= GRID_W, t, NEG)
            tiles[m] = t
        ts = list(tiles.values())
        mx = ts[0]
        for t in ts[1:]:
            mx = jnp.maximum(mx, t)
        mx = jnp.max(mx, axis=-1, keepdims=True)
        ps = {m: jnp.exp2(t - mx) for m, t in tiles.items()}
        tot = None
        for pt in ps.values():
            tot = pt if tot is None else tot + pt
        inv_l.append(1.0 / jnp.sum(tot, axis=-1, keepdims=True))
        zero = jnp.zeros((GRID_W, 2 * GRID_W), BF16)
        p_rows.append(jnp.concatenate(
            [ps[m].astype(BF16) if m in ps else zero for m in range(NA_KROWS // 2)], axis=1))
    p = jnp.concatenate(p_rows, axis=0)
    return _dot(p, vw) * jnp.concatenate(inv_l, axis=0)


def _na_kernel(q_ref, k_ref, v_ref, tp_ref, o_ref, *, rows):
    for blk in range(rows // NA_QROWS):
        start = _na_window_start_row(blk, rows) * GRID_W
        o = _na_block(q_ref[blk * NA_TQ:(blk + 1) * NA_TQ, :], k_ref[start:start + NA_TK, :],
                      v_ref[start:start + NA_TK, :], tp_ref, blk, rows)
        o_ref[blk * NA_TQ:(blk + 1) * NA_TQ, :] = o.astype(BF16)


def _na_attn(proj_hm, tp, *, batch, seq, heads, q_blk0, k_blk0, v_blk0):
    rows = seq // GRID_W
    kern = lambda *a: _na_kernel(*a, rows=rows)
    head_blk = lambda blk0: pl.BlockSpec((None, seq, HEAD_DIM), lambda b, h: (blk0 + h, b, 0))
    return pl.pallas_call(
        kern,
        out_shape=jax.ShapeDtypeStruct((batch * seq, heads * HEAD_DIM), BF16),
        grid=(batch, heads),
        in_specs=[
            head_blk(q_blk0), head_blk(k_blk0), head_blk(v_blk0),
            pl.BlockSpec((1, 2 * NA_WIN_ROWS, GRID_W, 2 * GRID_W), lambda b, h: (h, 0, 0, 0)),
        ],
        out_specs=pl.BlockSpec((seq, HEAD_DIM), lambda b, h: (b, h)),
        compiler_params=pltpu.CompilerParams(
            dimension_semantics=("arbitrary", "arbitrary"), vmem_limit_bytes=48 * MIB),
        name="na_attn",
    )(proj_hm, proj_hm, proj_hm, tp)


def _t5_bucket(rel):
    nb = T5_BUCKETS // 2
    ret = jnp.where(rel > 0, nb, 0)
    n = jnp.abs(rel)
    max_exact = nb // 2
    nf = jnp.maximum(n, 1).astype(jnp.float32)
    large = max_exact + (jnp.log(nf / max_exact) / math.log(T5_MAX_DIST / max_exact)
                         * (nb - max_exact)).astype(jnp.int32)
    large = jnp.minimum(large, nb - 1)
    return ret + jnp.where(n < max_exact, n, large)


def _da_table_kernel(bucket_ref, t5_ref, o_ref, *, heads):
    h = pl.program_id(0)
    shape = (DA_TQ, DA_TQ)
    nb = T5_BUCKETS // 2
    o_ref[0, 0] = jnp.full(shape, t5_ref[(nb - 1) * heads + h] * LOG2E, F32)
    o_ref[0, 4] = jnp.full(shape, t5_ref[(T5_BUCKETS - 1) * heads + h] * LOG2E, F32)
    for s in range(3):
        bk = bucket_ref[s]
        val = jnp.zeros(shape, F32)
        for bkt in range(T5_BUCKETS):
            val = jnp.where(bk == bkt, t5_ref[bkt * heads + h] * LOG2E, val)
        o_ref[0, s + 1] = val


def _da_table(t5_table):
    heads = t5_table.shape[1]
    assert DA_TQ >= T5_MAX_DIST
    i = jnp.arange(DA_TQ)[:, None]
    c = jnp.arange(DA_TQ)[None, :]
    bucket = jnp.stack([_t5_bucket((dj * DA_TQ + c) - i) for dj in (-1, 0, 1)]).astype(jnp.int32)
    kern = lambda *a: _da_table_kernel(*a, heads=heads)
    return pl.pallas_call(
        kern,
        out_shape=jax.ShapeDtypeStruct((heads, 5, DA_TQ, DA_TQ), F32),
        grid=(heads,),
        in_specs=[
            pl.BlockSpec((3, DA_TQ, DA_TQ), lambda h: (0, 0, 0)),
            pl.BlockSpec(memory_space=pltpu.SMEM),
        ],
        out_specs=pl.BlockSpec((1, 5, DA_TQ, DA_TQ), lambda h: (h, 0, 0, 0)),
        compiler_params=pltpu.CompilerParams(dimension_semantics=("arbitrary",)),
        name="da_table",
    )(bucket, t5_table.reshape(-1))


def _da_kernel(q_ref, k_ref, v_ref, tb_ref, lam_ref, g_ref, o_ref, vext_ref, s_ref, p_ref, *, seq):
    n_kt = seq // DA_TQ

    vext_ref[:, :HEAD_DIM] = v_ref[...]
    lane_v = lax.broadcasted_iota(jnp.int32, (seq, HEAD_DIM), 1)
    vext_ref[:, HEAD_DIM:] = jnp.where(lane_v == 0, 1.0, 0.0).astype(BF16)

    lp = lam_ref[...]
    t1 = jnp.sum(lp[0:1] * lp[1:2], axis=-1, keepdims=True)
    t2 = jnp.sum(lp[2:3] * lp[3:4], axis=-1, keepdims=True)
    lam = jnp.exp(t1) - jnp.exp(t2) + LAM_INIT

    c_left = tb_ref[0, 0, 0:1, 0:1]
    c_right = tb_ref[0, 4, 0:1, 0:1]
    lane = lax.broadcasted_iota(jnp.int32, (DA_TQ, HEAD_DIM), 1)

    def key_cols(j):
        return slice(j * DA_TQ, (j + 1) * DA_TQ)

    def lane_max(parts):
        acc = parts[0]
        for part in parts[1:]:
            acc = jnp.maximum(acc, part)
        return jnp.max(acc, axis=-1, keepdims=True)

    def qk_scores(t):
        q = q_ref[t * DA_TQ:(t + 1) * DA_TQ, :]
        zero = jnp.zeros_like(q)
        q12 = jnp.concatenate(
            [jnp.where(lane < DA_QK_DIM, q, zero), jnp.where(lane >= DA_QK_DIM, q, zero)], axis=0)
        for j in range(n_kt):
            s_ref[t % 2, :, key_cols(j)] = _dot_nt(q12, k_ref[key_cols(j), :])

    qk_scores(0)
    for t in range(n_kt):
        buf = t % 2
        if t + 1 < n_kt:
            qk_scores(t + 1)

        def scores(j):
            s = s_ref[buf, :, key_cols(j)]
            if abs(j - t) <= 1:
                bias = tb_ref[0, j - t + 2]
                s = s + jnp.concatenate([bias, bias], axis=0)
            return s

        def halves(j):
            s = scores(j)
            return [s[:, i * 128:(i + 1) * 128] for i in range(DA_TQ // 128)]

        near = [j for j in range(n_kt) if abs(j - t) <= 1]
        left = [j for j in range(n_kt) if j < t - 1]
        right = [j for j in range(n_kt) if j > t + 1]
        m = lane_max([part for j in near for part in halves(j)])
        if left:
            m = jnp.maximum(m, lane_max([part for j in left for part in halves(j)]) + c_left)
        if right:
            m = jnp.maximum(m, lane_max([part for j in right for part in halves(j)]) + c_right)

        for j in range(n_kt):
            shift = m if abs(j - t) <= 1 else (m - c_left if j < t else m - c_right)
            p_ref[buf, :, key_cols(j)] = jnp.exp2(scores(j) - shift).astype(BF16)
        oe1 = _dot(p_ref[buf, :DA_TQ, :], vext_ref[...])
        oe2 = _dot(p_ref[buf, DA_TQ:, :], vext_ref[...])
        o1, l1 = oe1[:, :HEAD_DIM], oe1[:, HEAD_DIM:HEAD_DIM + 1]
        o2, l2 = oe2[:, :HEAD_DIM], oe2[:, HEAD_DIM:HEAD_DIM + 1]
        o = o1 * (1.0 / l1) - o2 * (lam / l2)
        ms = jnp.mean(o * o, axis=-1, keepdims=True)
        y = (o * lax.rsqrt(ms + EPS) * g_ref[...]) * (1.0 - LAM_INIT)
        o_ref[t * DA_TQ:(t + 1) * DA_TQ, :] = y.astype(BF16)


def _da_attn(proj_hm, tb, da_lambda, subln_g, *, batch, seq, heads, q_blk0, k_blk0, v_blk0):
    kern = lambda *a: _da_kernel(*a, seq=seq)
    head_blk = lambda blk0: pl.BlockSpec((None, seq, HEAD_DIM), lambda b, h: (blk0 + h, b, 0))
    return pl.pallas_call(
        kern,
        out_shape=jax.ShapeDtypeStruct((batch * seq, heads * HEAD_DIM), BF16),
        grid=(batch, heads),
        in_specs=[
            head_blk(q_blk0), head_blk(k_blk0), head_blk(v_blk0),
            pl.BlockSpec((1, 5, DA_TQ, DA_TQ), lambda b, h: (h, 0, 0, 0)),
            pl.BlockSpec(da_lambda.shape, lambda b, h: (0, 0)),
            pl.BlockSpec((1, HEAD_DIM), lambda b, h: (0, 0)),
        ],
        out_specs=pl.BlockSpec((seq, HEAD_DIM), lambda b, h: (b, h)),
        scratch_shapes=[
            pltpu.VMEM((seq, 2 * HEAD_DIM), BF16),
            pltpu.VMEM((2, 2 * DA_TQ, seq), F32),
            pltpu.VMEM((2, 2 * DA_TQ, seq), BF16),
        ],
        compiler_params=pltpu.CompilerParams(
            dimension_semantics=("arbitrary", "arbitrary"), vmem_limit_bytes=48 * MIB),
        name="da_attn",
    )(proj_hm, proj_hm, proj_hm, tb, da_lambda, subln_g)


def _outproj_kernel(na_ref, da_ref, nag_ref, wa_ref, wb_ref, x_ref, g1_ref, o_ref, nan_ref):
    j = pl.program_id(1)

    @pl.when(j == 0)
    def _():
        o = na_ref[...].astype(F32)
        ms = jnp.mean(o * o, axis=-1, keepdims=True)
        nan_ref[...] = (o * lax.rsqrt(ms + EPS) * nag_ref[...]).astype(BF16)

    mix = _dot(nan_ref[...], wa_ref[...]) + _dot(da_ref[...], wb_ref[...])
    o_ref[...] = x_ref[...] + g1_ref[0] * mix


def _out_proj(na_o, da_o, na_out_g, w_out_bf, x2d, mod3, *, seq):
    M, D = x2d.shape
    Kh = na_o.shape[1]
    tm, tn = 1024, 1024
    per_b = seq // tm
    return pl.pallas_call(
        _outproj_kernel,
        out_shape=jax.ShapeDtypeStruct((M, D), F32),
        grid=(M // tm, D // tn),
        in_specs=[
            pl.BlockSpec((tm, Kh), lambda i, j: (i, 0)),
            pl.BlockSpec((tm, Kh), lambda i, j: (i, 0)),
            pl.BlockSpec((1, Kh), lambda i, j: (0, 0)),
            pl.BlockSpec((Kh, tn), lambda i, j: (0, j)),
            pl.BlockSpec((Kh, tn), lambda i, j: (1, j)),
            pl.BlockSpec((tm, tn), lambda i, j: (i, j)),
            pl.BlockSpec((1, 1, tn), lambda i, j: ((i // per_b) * 6 + 2, 0, j)),
        ],
        out_specs=pl.BlockSpec((tm, tn), lambda i, j: (i, j)),
        scratch_shapes=[pltpu.VMEM((tm, Kh), BF16)],
        compiler_params=pltpu.CompilerParams(
            dimension_semantics=("arbitrary", "arbitrary"), vmem_limit_bytes=48 * MIB),
        name="out_proj",
    )(na_o, da_o, na_out_g, w_out_bf, w_out_bf, x2d, mod3)


def _ffn_kernel(x_ref, sh_ref, sc_ref, g2_ref, ng_ref, fg_ref, wg_ref, wu_ref, wd_ref, o_ref, h_ref,
                *, n_f):
    f = pl.program_id(1)
    n_chunks = x_ref.shape[0] // FFN_ROW_CHUNK

    def chunk_rows(r):
        return pl.ds(pl.multiple_of(r * FFN_ROW_CHUNK, FFN_ROW_CHUNK), FFN_ROW_CHUNK)

    @pl.when(f == 0)
    def _():
        def body(r, carry):
            rows = chunk_rows(r)
            h = _modulated_norm(x_ref[rows, :], ng_ref[...], sc_ref[0], sh_ref[0])
            h_ref[rows, :] = h.astype(BF16)
            o_ref[rows, :] = jnp.zeros((FFN_ROW_CHUNK, o_ref.shape[1]), F32)
            return carry
        lax.fori_loop(0, n_chunks, body, 0)

    def body(r, carry):
        rows = chunk_rows(r)
        h = h_ref[rows, :]
        g = _dot(h, wg_ref[...])
        u = _dot(h, wu_ref[...])
        a = (g * _sigmoid(g) * u).astype(BF16)
        o_ref[rows, :] += _dot(a, wd_ref[...])
        return carry
    lax.fori_loop(0, n_chunks, body, 0)

    @pl.when(f == n_f - 1)
    def _():
        def body(r, carry):
            rows = chunk_rows(r)
            x2 = x_ref[rows, :] + g2_ref[0] * o_ref[rows, :]
            ms = jnp.mean(x2 * x2, axis=-1, keepdims=True)
            o_ref[rows, :] = x2 * lax.rsqrt(ms + EPS) * fg_ref[...]
            return carry
        lax.fori_loop(0, n_chunks, body, 0)


def _ffn(x1, mod3, norm_g, final_g, wg_bf, wu_bf, wd_bf, *, seq):
    M, D = x1.shape
    F = wg_bf.shape[1]
    tm, tf = 1024, 512
    per_b = seq // tm
    n_f = F // tf
    kern = lambda *a: _ffn_kernel(*a, n_f=n_f)
    return pl.pallas_call(
        kern,
        out_shape=jax.ShapeDtypeStruct((M, D), F32),
        grid=(M // tm, n_f),
        in_specs=[
            pl.BlockSpec((tm, D), lambda i, f: (i, 0)),
            pl.BlockSpec((1, 1, D), lambda i, f: ((i // per_b) * 6 + 3, 0, 0)),
            pl.BlockSpec((1, 1, D), lambda i, f: ((i // per_b) * 6 + 4, 0, 0)),
            pl.BlockSpec((1, 1, D), lambda i, f: ((i // per_b) * 6 + 5, 0, 0)),
            pl.BlockSpec((1, D), lambda i, f: (0, 0)),
            pl.BlockSpec((1, D), lambda i, f: (0, 0)),
            pl.BlockSpec((D, tf), lambda i, f: (0, f)),
            pl.BlockSpec((D, tf), lambda i, f: (0, f)),
            pl.BlockSpec((tf, D), lambda i, f: (f, 0)),
        ],
        out_specs=pl.BlockSpec((tm, D), lambda i, f: (i, 0)),
        scratch_shapes=[pltpu.VMEM((tm, D), BF16)],
        compiler_params=pltpu.CompilerParams(
            dimension_semantics=("arbitrary", "arbitrary"), vmem_limit_bytes=60 * MIB),
        name="ffn",
    )(x1, mod3, mod3, mod3, norm_g, final_g, wg_bf, wu_bf, wd_bf)


def kernel(x, c, w_ada, b_ada, norm1_g, w_in, na_rpb, na_out_g, da_lambda, da_subln_g, t5_table,
           w_out, norm2_g, w_gate, w_up, w_down, final_g):
    B, S, D = x.shape
    na_heads = na_rpb.shape[1]
    da_heads = t5_table.shape[1]
    na_width = na_heads * HEAD_DIM
    assert w_ada.shape[0] == 1, "single layer"

    mod = _ada_mod(c, w_ada[0], b_ada[0])
    mod3 = mod.reshape(B * 6, 1, D)
    x2d = x.reshape(B * S, D)

    proj_hm = _in_proj(x2d, mod3, norm1_g, w_in[0].astype(BF16), seq=S,
                       na_width=na_width, da_q_off=3 * na_width)
    na_o = _na_attn(proj_hm, _na_table(na_rpb[0]), batch=B, seq=S, heads=na_heads,
                    q_blk0=0, k_blk0=na_heads, v_blk0=2 * na_heads)
    da_o = _da_attn(proj_hm, _da_table(t5_table), da_lambda[0], da_subln_g, batch=B, seq=S,
                    heads=da_heads, q_blk0=3 * na_heads, k_blk0=3 * na_heads + da_heads,
                    v_blk0=3 * na_heads + 2 * da_heads)
    x1 = _out_proj(na_o, da_o, na_out_g, w_out[0].astype(BF16), x2d, mod3, seq=S)
    out = _ffn(x1, mod3, norm2_g, final_g.reshape(1, D), w_gate[0].astype(BF16),
               w_up[0].astype(BF16), w_down[0].astype(BF16), seq=S)
    return out.reshape(B, S, D)
```

```python
import math

import jax
import jax.numpy as jnp
from jax import lax
from jax.experimental import pallas as pl
from jax.experimental.pallas import tpu as pltpu

F32 = jnp.float32
BF16 = jnp.bfloat16

EPS = 1e-6
NEG = -1e30

GRID_W = 64
HEAD_DIM = 128
DA_QK_DIM = 64
NA_WIN_ROWS = 8
NA_WIN_COLS = 16
T5_BUCKETS = 32
T5_MAX_DIST = 128
LAM_INIT = 0.8 - 0.6 * math.exp(-0.3 * 0)
LOG2E = math.log2(math.e)

MIB = 1024 * 1024

NA_QROWS = 8
NA_KROWS = 16
NA_TQ = NA_QROWS * GRID_W
NA_TK = NA_KROWS * GRID_W

DA_TQ = 256

FFN_ROW_CHUNK = 256
NORM_ROW_CHUNK = 128


def _dot(a, b):
    return jnp.dot(a, b, preferred_element_type=F32)


def _dot_nt(a, b):
    return lax.dot_general(a, b, (((1,), (1,)), ((), ())), preferred_element_type=F32)


def _sigmoid(x):
    return 1.0 / (1.0 + jnp.exp(-x))


def _ada_kernel(c_ref, w_ref, b_ref, o_ref):
    c = c_ref[...]
    cs = c * _sigmoid(c)
    o_ref[...] = _dot(cs.astype(BF16), w_ref[...].astype(BF16)) + b_ref[...]


def _ada_mod(c, w_ada, b_ada):
    B, D = c.shape
    N = w_ada.shape[1]
    tn = 1024
    rows = 8
    c_pad = jnp.zeros((rows, D), F32).at[:B].set(c)
    out = pl.pallas_call(
        _ada_kernel,
        out_shape=jax.ShapeDtypeStruct((rows, N), F32),
        grid=(N // tn,),
        in_specs=[
            pl.BlockSpec((rows, D), lambda j: (0, 0)),
            pl.BlockSpec((D, tn), lambda j: (0, j)),
            pl.BlockSpec((1, tn), lambda j: (0, j)),
        ],
        out_specs=pl.BlockSpec((rows, tn), lambda j: (0, j)),
        compiler_params=pltpu.CompilerParams(
            dimension_semantics=("arbitrary",), vmem_limit_bytes=40 * MIB),
        name="ada_mod",
    )(c_pad, w_ada, b_ada.reshape(1, N))
    return out[:B]


def _modulated_norm(x, g, sc, sh):
    ms = jnp.mean(x * x, axis=-1, keepdims=True)
    return (x * lax.rsqrt(ms + EPS) * g) * (1.0 + sc) + sh


def _inproj_kernel(x_ref, sh_ref, sc_ref, g_ref, w_ref, o_ref, h_ref, *, na_q_blk, da_q_blk,
                   na_scale, da_scale):
    j = pl.program_id(1)

    @pl.when(j == 0)
    def _():
        gain = g_ref[...] * (1.0 + sc_ref[0])
        shift = sh_ref[0]

        def body(r, carry):
            rows = pl.ds(pl.multiple_of(r * NORM_ROW_CHUNK, NORM_ROW_CHUNK), NORM_ROW_CHUNK)
            x = x_ref[rows, :]
            ms = jnp.mean(x * x, axis=-1, keepdims=True)
            h_ref[rows, :] = (x * lax.rsqrt(ms + EPS) * gain + shift).astype(BF16)
            return carry
        lax.fori_loop(0, x_ref.shape[0] // NORM_ROW_CHUNK, body, 0)

    acc = _dot(h_ref[...], w_ref[...])
    scale = jnp.where(j == na_q_blk, na_scale, jnp.where(j == da_q_blk, da_scale, 1.0))
    acc = acc * scale
    for k in range(o_ref.shape[0]):
        o_ref[k] = acc[:, k * HEAD_DIM:(k + 1) * HEAD_DIM].astype(BF16)


def _in_proj(x2d, mod3, norm_g, w_in_bf, *, seq, na_width, da_q_off):
    M, D = x2d.shape
    N = w_in_bf.shape[1]
    tm, tn = 1024, 1024
    per_b = seq // tm
    kern = lambda *a: _inproj_kernel(
        *a, na_q_blk=0, da_q_blk=da_q_off // tn,
        na_scale=HEAD_DIM ** -0.5 * LOG2E, da_scale=DA_QK_DIM ** -0.5 * LOG2E)
    assert na_width == tn
    return pl.pallas_call(
        kern,
        out_shape=jax.ShapeDtypeStruct((N // HEAD_DIM, M, HEAD_DIM), BF16),
        grid=(M // tm, N // tn),
        in_specs=[
            pl.BlockSpec((tm, D), lambda i, j: (i, 0)),
            pl.BlockSpec((1, 1, D), lambda i, j: ((i // per_b) * 6 + 0, 0, 0)),
            pl.BlockSpec((1, 1, D), lambda i, j: ((i // per_b) * 6 + 1, 0, 0)),
            pl.BlockSpec((1, D), lambda i, j: (0, 0)),
            pl.BlockSpec((D, tn), lambda i, j: (0, j)),
        ],
        out_specs=pl.BlockSpec((tn // HEAD_DIM, tm, HEAD_DIM), lambda i, j: (j, i, 0)),
        scratch_shapes=[pltpu.VMEM((tm, D), BF16)],
        compiler_params=pltpu.CompilerParams(
            dimension_semantics=("arbitrary", "arbitrary"), vmem_limit_bytes=48 * MIB),
        name="in_proj",
    )(x2d, mod3, mod3, norm_g, w_in_bf)


def _na_table_kernel(rpb_ref, o_ref, *, n_rel_rows, n_rel_cols):
    h = pl.program_id(0)
    shape = (GRID_W, 2 * GRID_W)
    qc = lax.broadcasted_iota(jnp.int32, shape, 0)
    lane = lax.broadcasted_iota(jnp.int32, shape, 1)
    upper = lane >= GRID_W
    kc = jnp.where(upper, lane - GRID_W, lane)
    dc = jnp.clip(kc - qc, -(NA_WIN_COLS - 1), NA_WIN_COLS - 1) + NA_WIN_COLS - 1
    c0 = jnp.clip(qc - NA_WIN_COLS // 2, 0, GRID_W - NA_WIN_COLS)
    in_win = (kc >= c0) & (kc < c0 + NA_WIN_COLS)
    base = h * (n_rel_rows * n_rel_cols)
    for e in range(2 * NA_WIN_ROWS):
        val = jnp.full(shape, NEG, F32)
        for d in range(n_rel_cols):
            r_lo, r_hi = e - 1, e
            lo = rpb_ref[base + r_lo * n_rel_cols + d] * LOG2E if 0 <= r_lo < n_rel_rows else NEG
            hi = rpb_ref[base + r_hi * n_rel_cols + d] * LOG2E if 0 <= r_hi < n_rel_rows else NEG
            val = jnp.where(dc == d, jnp.where(upper, hi, lo), val)
        o_ref[0, e] = jnp.where(in_win, val, NEG)


def _na_table(rpb):
    H, nr, nc = rpb.shape
    kern = lambda *a: _na_table_kernel(*a, n_rel_rows=nr, n_rel_cols=nc)
    return pl.pallas_call(
        kern,
        out_shape=jax.ShapeDtypeStruct((H, 2 * NA_WIN_ROWS, GRID_W, 2 * GRID_W), F32),
        grid=(H,),
        in_specs=[pl.BlockSpec(memory_space=pltpu.SMEM)],
        out_specs=pl.BlockSpec((1, 2 * NA_WIN_ROWS, GRID_W, 2 * GRID_W), lambda h: (h, 0, 0, 0)),
        compiler_params=pltpu.CompilerParams(dimension_semantics=("arbitrary",)),
        name="na_table",
    )(rpb.reshape(-1))


def _na_window_start_row(blk, rows):
    return min(max(blk * NA_QROWS - (NA_KROWS - NA_QROWS) // 2, 0), rows - NA_KROWS)


def _na_block(q, kw, vw, tp_ref, blk, rows):
    lane = lax.broadcasted_iota(jnp.int32, (GRID_W, 2 * GRID_W), 1)
    kr0 = _na_window_start_row(blk, rows)
    s = _dot_nt(q, kw)
    p_rows, inv_l = [], []
    for i in range(NA_QROWS):
        qr = blk * NA_QROWS + i
        rs = min(max(qr - NA_WIN_ROWS // 2, 0), rows - NA_WIN_ROWS)
        tiles = {}
        for m in range(NA_KROWS // 2):
            ka = kr0 + 2 * m
            va = rs <= ka < rs + NA_WIN_ROWS
            vb = rs <= ka + 1 < rs + NA_WIN_ROWS
            if not (va or vb):
                continue
            e = ka - qr + NA_WIN_ROWS
            t = s[i * GRID_W:(i + 1) * GRID_W, m * 2 * GRID_W:(m + 1) * 2 * GRID_W] + tp_ref[0, e]
            if not vb:
                t = jnp.where(lane < GRID_W, t, NEG)
            elif not va:
                t = jnp.where(lane >= GRID_W, t, NEG)
            tiles[m] = t
        ts = list(tiles.values())
        mx = ts[0]
        for t in ts[1:]:
            mx = jnp.maximum(mx, t)
        mx = jnp.max(mx, axis=-1, keepdims=True)
        ps = {m: jnp.exp2(t - mx) for m, t in tiles.items()}
        tot = None
        for pt in ps.values():
            tot = pt if tot is None else tot + pt
        inv_l.append(1.0 / jnp.sum(tot, axis=-1, keepdims=True))
        zero = jnp.zeros((GRID_W, 2 * GRID_W), BF16)
        p_rows.append(jnp.concatenate(
            [ps[m].astype(BF16) if m in ps else zero for m in range(NA_KROWS // 2)], axis=1))
    p = jnp.concatenate(p_rows, axis=0)
    return _dot(p, vw) * jnp.concatenate(inv_l, axis=0)


def _na_kernel(q_ref, k_ref, v_ref, tp_ref, o_ref, *, rows):
    for blk in range(rows // NA_QROWS):
        start = _na_window_start_row(blk, rows) * GRID_W
        o = _na_block(q_ref[blk * NA_TQ:(blk + 1) * NA_TQ, :], k_ref[start:start + NA_TK, :],
                      v_ref[start:start + NA_TK, :], tp_ref, blk, rows)
        o_ref[blk * NA_TQ:(blk + 1) * NA_TQ, :] = o.astype(BF16)


def _na_attn(proj_hm, tp, *, batch, seq, heads, q_blk0, k_blk0, v_blk0):
    rows = seq // GRID_W
    kern = lambda *a: _na_kernel(*a, rows=rows)
    head_blk = lambda blk0: pl.BlockSpec((None, seq, HEAD_DIM), lambda b, h: (blk0 + h, b, 0))
    return pl.pallas_call(
        kern,
        out_shape=jax.ShapeDtypeStruct((batch * seq, heads * HEAD_DIM), BF16),
        grid=(batch, heads),
        in_specs=[
            head_blk(q_blk0), head_blk(k_blk0), head_blk(v_blk0),
            pl.BlockSpec((1, 2 * NA_WIN_ROWS, GRID_W, 2 * GRID_W), lambda b, h: (h, 0, 0, 0)),
        ],
        out_specs=pl.BlockSpec((seq, HEAD_DIM), lambda b, h: (b, h)),
        compiler_params=pltpu.CompilerParams(
            dimension_semantics=("arbitrary", "arbitrary"), vmem_limit_bytes=48 * MIB),
        name="na_attn",
    )(proj_hm, proj_hm, proj_hm, tp)


def _t5_bucket(rel):
    nb = T5_BUCKETS // 2
    ret = jnp.where(rel > 0, nb, 0)
    n = jnp.abs(rel)
    max_exact = nb // 2
    nf = jnp.maximum(n, 1).astype(jnp.float32)
    large = max_exact + (jnp.log(nf / max_exact) / math.log(T5_MAX_DIST / max_exact)
                         * (nb - max_exact)).astype(jnp.int32)
    large = jnp.minimum(large, nb - 1)
    return ret + jnp.where(n < max_exact, n, large)


def _da_table_kernel(bucket_ref, t5_ref, o_ref, *, heads):
    h = pl.program_id(0)
    shape = (DA_TQ, DA_TQ)
    nb = T5_BUCKETS // 2
    o_ref[0, 0] = jnp.full(shape, t5_ref[(nb - 1) * heads + h] * LOG2E, F32)
    o_ref[0, 4] = jnp.full(shape, t5_ref[(T5_BUCKETS - 1) * heads + h] * LOG2E, F32)
    for s in range(3):
        bk = bucket_ref[s]
        val = jnp.zeros(shape, F32)
        for bkt in range(T5_BUCKETS):
            val = jnp.where(bk == bkt, t5_ref[bkt * heads + h] * LOG2E, val)
        o_ref[0, s + 1] = val


def _da_table(t5_table):
    heads = t5_table.shape[1]
    assert DA_TQ >= T5_MAX_DIST
    i = jnp.arange(DA_TQ)[:, None]
    c = jnp.arange(DA_TQ)[None, :]
    bucket = jnp.stack([_t5_bucket((dj * DA_TQ + c) - i) for dj in (-1, 0, 1)]).astype(jnp.int32)
    kern = lambda *a: _da_table_kernel(*a, heads=heads)
    return pl.pallas_call(
        kern,
        out_shape=jax.ShapeDtypeStruct((heads, 5, DA_TQ, DA_TQ), F32),
        grid=(heads,),
        in_specs=[
            pl.BlockSpec((3, DA_TQ, DA_TQ), lambda h: (0, 0, 0)),
            pl.BlockSpec(memory_space=pltpu.SMEM),
        ],
        out_specs=pl.BlockSpec((1, 5, DA_TQ, DA_TQ), lambda h: (h, 0, 0, 0)),
        compiler_params=pltpu.CompilerParams(dimension_semantics=("arbitrary",)),
        name="da_table",
    )(bucket, t5_table.reshape(-1))


def _da_kernel(q_ref, k_ref, v_ref, tb_ref, lam_ref, g_ref, o_ref, vext_ref, s_ref, p_ref, *, seq):
    n_kt = seq // DA_TQ

    vext_ref[:, :HEAD_DIM] = v_ref[...]
    lane_v = lax.broadcasted_iota(jnp.int32, (seq, HEAD_DIM), 1)
    vext_ref[:, HEAD_DIM:] = jnp.where(lane_v == 0, 1.0, 0.0).astype(BF16)

    lp = lam_ref[...]
    t1 = jnp.sum(lp[0:1] * lp[1:2], axis=-1, keepdims=True)
    t2 = jnp.sum(lp[2:3] * lp[3:4], axis=-1, keepdims=True)
    lam = jnp.exp(t1) - jnp.exp(t2) + LAM_INIT

    c_left = tb_ref[0, 0, 0:1, 0:1]
    c_right = tb_ref[0, 4, 0:1, 0:1]
    lane = lax.broadcasted_iota(jnp.int32, (DA_TQ, HEAD_DIM), 1)

    def key_cols(j):
        return slice(j * DA_TQ, (j + 1) * DA_TQ)

    def lane_max(parts):
        acc = parts[0]
        for part in parts[1:]:
            acc = jnp.maximum(acc, part)
        return jnp.max(acc, axis=-1, keepdims=True)

    def qk_scores(t):
        q = q_ref[t * DA_TQ:(t + 1) * DA_TQ, :]
        zero = jnp.zeros_like(q)
        q12 = jnp.concatenate(
            [jnp.where(lane < DA_QK_DIM, q, zero), jnp.where(lane >= DA_QK_DIM, q, zero)], axis=0)
        for j in range(n_kt):
            s_ref[t % 2, :, key_cols(j)] = _dot_nt(q12, k_ref[key_cols(j), :])

    qk_scores(0)
    for t in range(n_kt):
        buf = t % 2
        if t + 1 < n_kt:
            qk_scores(t + 1)

        def scores(j):
            s = s_ref[buf, :, key_cols(j)]
            if abs(j - t) <= 1:
                bias = tb_ref[0, j - t + 2]
                s = s + jnp.concatenate([bias, bias], axis=0)
            return s

        def halves(j):
            s = scores(j)
            return [s[:, i * 128:(i + 1) * 128] for i in range(DA_TQ // 128)]

        near = [j for j in range(n_kt) if abs(j - t) <= 1]
        left = [j for j in range(n_kt) if j < t - 1]
        right = [j for j in range(n_kt) if j > t + 1]
        m = lane_max([part for j in near for part in halves(j)])
        if left:
            m = jnp.maximum(m, lane_max([part for j in left for part in halves(j)]) + c_left)
        if right:
            m = jnp.maximum(m, lane_max([part for j in right for part in halves(j)]) + c_right)

        for j in range(n_kt):
            shift = m if abs(j - t) <= 1 else (m - c_left if j < t else m - c_right)
            p_ref[buf, :, key_cols(j)] = jnp.exp2(scores(j) - shift).astype(BF16)
        oe1 = _dot(p_ref[buf, :DA_TQ, :], vext_ref[...])
        oe2 = _dot(p_ref[buf, DA_TQ:, :], vext_ref[...])
        o1, l1 = oe1[:, :HEAD_DIM], oe1[:, HEAD_DIM:HEAD_DIM + 1]
        o2, l2 = oe2[:, :HEAD_DIM], oe2[:, HEAD_DIM:HEAD_DIM + 1]
        o = o1 * (1.0 / l1) - o2 * (lam / l2)
        ms = jnp.mean(o * o, axis=-1, keepdims=True)
        y = (o * lax.rsqrt(ms + EPS) * g_ref[...]) * (1.0 - LAM_INIT)
        o_ref[t * DA_TQ:(t + 1) * DA_TQ, :] = y.astype(BF16)


def _da_attn(proj_hm, tb, da_lambda, subln_g, *, batch, seq, heads, q_blk0, k_blk0, v_blk0):
    kern = lambda *a: _da_kernel(*a, seq=seq)
    head_blk = lambda blk0: pl.BlockSpec((None, seq, HEAD_DIM), lambda b, h: (blk0 + h, b, 0))
    return pl.pallas_call(
        kern,
        out_shape=jax.ShapeDtypeStruct((batch * seq, heads * HEAD_DIM), BF16),
        grid=(batch, heads),
        in_specs=[
            head_blk(q_blk0), head_blk(k_blk0), head_blk(v_blk0),
            pl.BlockSpec((1, 5, DA_TQ, DA_TQ), lambda b, h: (h, 0, 0, 0)),
            pl.BlockSpec(da_lambda.shape, lambda b, h: (0, 0)),
            pl.BlockSpec((1, HEAD_DIM), lambda b, h: (0, 0)),
        ],
        out_specs=pl.BlockSpec((seq, HEAD_DIM), lambda b, h: (b, h)),
        scratch_shapes=[
            pltpu.VMEM((seq, 2 * HEAD_DIM), BF16),
            pltpu.VMEM((2, 2 * DA_TQ, seq), F32),
            pltpu.VMEM((2, 2 * DA_TQ, seq), BF16),
        ],
        compiler_params=pltpu.CompilerParams(
            dimension_semantics=("arbitrary", "arbitrary"), vmem_limit_bytes=48 * MIB),
        name="da_attn",
    )(proj_hm, proj_hm, proj_hm, tb, da_lambda, subln_g)


def _outproj_kernel(na_ref, da_ref, nag_ref, wa_ref, wb_ref, x_ref, g1_ref, o_ref, nan_ref):
    j = pl.program_id(1)

    @pl.when(j == 0)
    def _():
        gain = nag_ref[...]

        def body(r, carry):
            rows = pl.ds(pl.multiple_of(r * NORM_ROW_CHUNK, NORM_ROW_CHUNK), NORM_ROW_CHUNK)
            o = na_ref[rows, :].astype(F32)
            ms = jnp.mean(o * o, axis=-1, keepdims=True)
            nan_ref[rows, :] = (o * lax.rsqrt(ms + EPS) * gain).astype(BF16)
            return carry
        lax.fori_loop(0, na_ref.shape[0] // NORM_ROW_CHUNK, body, 0)

    mix = _dot(nan_ref[...], wa_ref[...]) + _dot(da_ref[...], wb_ref[...])
    o_ref[...] = x_ref[...] + g1_ref[0] * mix


def _out_proj(na_o, da_o, na_out_g, w_out_bf, x2d, mod3, *, seq):
    M, D = x2d.shape
    Kh = na_o.shape[1]
    tm, tn = 1024, 1024
    per_b = seq // tm
    return pl.pallas_call(
        _outproj_kernel,
        out_shape=jax.ShapeDtypeStruct((M, D), F32),
        grid=(M // tm, D // tn),
        in_specs=[
            pl.BlockSpec((tm, Kh), lambda i, j: (i, 0)),
            pl.BlockSpec((tm, Kh), lambda i, j: (i, 0)),
            pl.BlockSpec((1, Kh), lambda i, j: (0, 0)),
            pl.BlockSpec((Kh, tn), lambda i, j: (0, j)),
            pl.BlockSpec((Kh, tn), lambda i, j: (1, j)),
            pl.BlockSpec((tm, tn), lambda i, j: (i, j)),
            pl.BlockSpec((1, 1, tn), lambda i, j: ((i // per_b) * 6 + 2, 0, j)),
        ],
        out_specs=pl.BlockSpec((tm, tn), lambda i, j: (i, j)),
        scratch_shapes=[pltpu.VMEM((tm, Kh), BF16)],
        compiler_params=pltpu.CompilerParams(
            dimension_semantics=("arbitrary", "arbitrary"), vmem_limit_bytes=48 * MIB),
        name="out_proj",
    )(na_o, da_o, na_out_g, w_out_bf, w_out_bf, x2d, mod3)


def _ffn_kernel(x_ref, sh_ref, sc_ref, g2_ref, ng_ref, fg_ref, wg_ref, wu_ref, wd_ref, o_ref, h_ref,
                *, n_f):
    f = pl.program_id(1)
    n_chunks = x_ref.shape[0] // FFN_ROW_CHUNK

    def chunk_rows(r):
        return pl.ds(pl.multiple_of(r * FFN_ROW_CHUNK, FFN_ROW_CHUNK), FFN_ROW_CHUNK)

    def norm_rows(r):
        return pl.ds(pl.multiple_of(r * NORM_ROW_CHUNK, NORM_ROW_CHUNK), NORM_ROW_CHUNK)

    n_norm_chunks = x_ref.shape[0] // NORM_ROW_CHUNK

    @pl.when(f == 0)
    def _():
        gain = ng_ref[...] * (1.0 + sc_ref[0])
        shift = sh_ref[0]

        def body(r, carry):
            rows = norm_rows(r)
            x = x_ref[rows, :]
            ms = jnp.mean(x * x, axis=-1, keepdims=True)
            h_ref[rows, :] = (x * lax.rsqrt(ms + EPS) * gain + shift).astype(BF16)
            o_ref[rows, :] = jnp.zeros((NORM_ROW_CHUNK, o_ref.shape[1]), F32)
            return carry
        lax.fori_loop(0, n_norm_chunks, body, 0)

    def body(r, carry):
        rows = chunk_rows(r)
        h = h_ref[rows, :]
        g = _dot(h, wg_ref[...])
        u = _dot(h, wu_ref[...])
        a = (g * _sigmoid(g) * u).astype(BF16)
        o_ref[rows, :] += _dot(a, wd_ref[...])
        return carry
    lax.fori_loop(0, n_chunks, body, 0, unroll=True)

    @pl.when(f == n_f - 1)
    def _():
        gate = g2_ref[0]
        final_gain = fg_ref[...]

        def body(r, carry):
            rows = norm_rows(r)
            x2 = x_ref[rows, :] + gate * o_ref[rows, :]
            ms = jnp.mean(x2 * x2, axis=-1, keepdims=True)
            o_ref[rows, :] = x2 * lax.rsqrt(ms + EPS) * final_gain
            return carry
        lax.fori_loop(0, n_norm_chunks, body, 0)


def _ffn(x1, mod3, norm_g, final_g, wg_bf, wu_bf, wd_bf, *, seq):
    M, D = x1.shape
    F = wg_bf.shape[1]
    tm, tf = 1024, 512
    per_b = seq // tm
    n_f = F // tf
    kern = lambda *a: _ffn_kernel(*a, n_f=n_f)
    return pl.pallas_call(
        kern,
        out_shape=jax.ShapeDtypeStruct((M, D), F32),
        grid=(M // tm, n_f),
        in_specs=[
            pl.BlockSpec((tm, D), lambda i, f: (i, 0)),
            pl.BlockSpec((1, 1, D), lambda i, f: ((i // per_b) * 6 + 3, 0, 0)),
            pl.BlockSpec((1, 1, D), lambda i, f: ((i // per_b) * 6 + 4, 0, 0)),
            pl.BlockSpec((1, 1, D), lambda i, f: ((i // per_b) * 6 + 5, 0, 0)),
            pl.BlockSpec((1, D), lambda i, f: (0, 0)),
            pl.BlockSpec((1, D), lambda i, f: (0, 0)),
            pl.BlockSpec((D, tf), lambda i, f: (0, f)),
            pl.BlockSpec((D, tf), lambda i, f: (0, f)),
            pl.BlockSpec((tf, D), lambda i, f: (f, 0)),
        ],
        out_specs=pl.BlockSpec((tm, D), lambda i, f: (i, 0)),
        scratch_shapes=[pltpu.VMEM((tm, D), BF16)],
        compiler_params=pltpu.CompilerParams(
            dimension_semantics=("arbitrary", "arbitrary"), vmem_limit_bytes=60 * MIB),
        name="ffn",
    )(x1, mod3, mod3, mod3, norm_g, final_g, wg_bf, wu_bf, wd_bf)


def kernel(x, c, w_ada, b_ada, norm1_g, w_in, na_rpb, na_out_g, da_lambda, da_subln_g, t5_table,
           w_out, norm2_g, w_gate, w_up, w_down, final_g):
    B, S, D = x.shape
    na_heads = na_rpb.shape[1]
    da_heads = t5_table.shape[1]
    na_width = na_heads * HEAD_DIM
    assert w_ada.shape[0] == 1, "single layer"

    mod = _ada_mod(c, w_ada[0], b_ada[0])
    mod3 = mod.reshape(B * 6, 1, D)
    x2d = x.reshape(B * S, D)

    proj_hm = _in_proj(x2d, mod3, norm1_g, w_in[0].astype(BF16), seq=S,
                       na_width=na_width, da_q_off=3 * na_width)
    na_o = _na_attn(proj_hm, _na_table(na_rpb[0]), batch=B, seq=S, heads=na_heads,
                    q_blk0=0, k_blk0=na_heads, v_blk0=2 * na_heads)
    da_o = _da_attn(proj_hm, _da_table(t5_table), da_lambda[0], da_subln_g, batch=B, seq=S,
                    heads=da_heads, q_blk0=3 * na_heads, k_blk0=3 * na_heads + da_heads,
                    v_blk0=3 * na_heads + 2 * da_heads)
    x1 = _out_proj(na_o, da_o, na_out_g, w_out[0].astype(BF16), x2d, mod3, seq=S)
    out = _ffn(x1, mod3, norm2_g, final_g.reshape(1, D), w_gate[0].astype(BF16),
               w_up[0].astype(BF16), w_down[0].astype(BF16), seq=S)
    return out.reshape(B, S, D)
```

```python
import math

import jax
import jax.numpy as jnp
from jax import lax
from jax.experimental import pallas as pl
from jax.experimental.pallas import tpu as pltpu

F32 = jnp.float32
BF16 = jnp.bfloat16

EPS = 1e-6
NEG = -1e30

GRID_W = 64
HEAD_DIM = 128
DA_QK_DIM = 64
NA_WIN_ROWS = 8
NA_WIN_COLS = 16
T5_BUCKETS = 32
T5_MAX_DIST = 128
LAM_INIT = 0.8 - 0.6 * math.exp(-0.3 * 0)
LOG2E = math.log2(math.e)

MIB = 1024 * 1024

NA_QROWS = 8
NA_KROWS = 16
NA_TQ = NA_QROWS * GRID_W
NA_TK = NA_KROWS * GRID_W

DA_TQ = 256

FFN_ROW_CHUNK = 256
NORM_ROW_CHUNK = 128
WCAST_ROW_CHUNK = 128


def _dot(a, b):
    return jnp.dot(a, b, preferred_element_type=F32)


def _dot_nt(a, b):
    return lax.dot_general(a, b, (((1,), (1,)), ((), ())), preferred_element_type=F32)


def _sigmoid(x):
    return 1.0 / (1.0 + jnp.exp(-x))


def _ada_kernel(c_ref, w_ref, b_ref, o_ref):
    c = c_ref[...]
    cs = c * _sigmoid(c)
    o_ref[...] = _dot(cs.astype(BF16), w_ref[...].astype(BF16)) + b_ref[...]


def _ada_mod(c, w_ada, b_ada):
    B, D = c.shape
    N = w_ada.shape[1]
    tn = 1024
    rows = 8
    c_pad = jnp.zeros((rows, D), F32).at[:B].set(c)
    out = pl.pallas_call(
        _ada_kernel,
        out_shape=jax.ShapeDtypeStruct((rows, N), F32),
        grid=(N // tn,),
        in_specs=[
            pl.BlockSpec((rows, D), lambda j: (0, 0)),
            pl.BlockSpec((D, tn), lambda j: (0, j)),
            pl.BlockSpec((1, tn), lambda j: (0, j)),
        ],
        out_specs=pl.BlockSpec((rows, tn), lambda j: (0, j)),
        compiler_params=pltpu.CompilerParams(
            dimension_semantics=("arbitrary",), vmem_limit_bytes=40 * MIB),
        name="ada_mod",
    )(c_pad, w_ada, b_ada.reshape(1, N))
    return out[:B]


def _modulated_norm(x, g, sc, sh):
    ms = jnp.mean(x * x, axis=-1, keepdims=True)
    return (x * lax.rsqrt(ms + EPS) * g) * (1.0 + sc) + sh


def _inproj_kernel(x_ref, sh_ref, sc_ref, g_ref, w_ref, o_ref, h_ref, *, na_q_blk, da_q_blk,
                   na_scale, da_scale):
    j = pl.program_id(1)

    @pl.when(j == 0)
    def _():
        gain = g_ref[...] * (1.0 + sc_ref[0])
        shift = sh_ref[0]

        def body(r, carry):
            rows = pl.ds(pl.multiple_of(r * NORM_ROW_CHUNK, NORM_ROW_CHUNK), NORM_ROW_CHUNK)
            x = x_ref[rows, :]
            ms = jnp.mean(x * x, axis=-1, keepdims=True)
            h_ref[rows, :] = (x * lax.rsqrt(ms + EPS) * gain + shift).astype(BF16)
            return carry
        lax.fori_loop(0, x_ref.shape[0] // NORM_ROW_CHUNK, body, 0)

    acc = _dot(h_ref[...], w_ref[...].astype(BF16))
    scale = jnp.where(j == na_q_blk, na_scale, jnp.where(j == da_q_blk, da_scale, 1.0))
    acc = acc * scale
    for k in range(o_ref.shape[0]):
        o_ref[k] = acc[:, k * HEAD_DIM:(k + 1) * HEAD_DIM].astype(BF16)


def _in_proj(x2d, mod3, norm_g, w_in, *, seq, na_width, da_q_off):
    M, D = x2d.shape
    N = w_in.shape[1]
    tm, tn = 1024, 1024
    per_b = seq // tm
    kern = lambda *a: _inproj_kernel(
        *a, na_q_blk=0, da_q_blk=da_q_off // tn,
        na_scale=HEAD_DIM ** -0.5 * LOG2E, da_scale=DA_QK_DIM ** -0.5 * LOG2E)
    assert na_width == tn
    return pl.pallas_call(
        kern,
        out_shape=jax.ShapeDtypeStruct((N // HEAD_DIM, M, HEAD_DIM), BF16),
        grid=(M // tm, N // tn),
        in_specs=[
            pl.BlockSpec((tm, D), lambda i, j: (i, 0)),
            pl.BlockSpec((1, 1, D), lambda i, j: ((i // per_b) * 6 + 0, 0, 0)),
            pl.BlockSpec((1, 1, D), lambda i, j: ((i // per_b) * 6 + 1, 0, 0)),
            pl.BlockSpec((1, D), lambda i, j: (0, 0)),
            pl.BlockSpec((D, tn), lambda i, j: (0, j)),
        ],
        out_specs=pl.BlockSpec((tn // HEAD_DIM, tm, HEAD_DIM), lambda i, j: (j, i, 0)),
        scratch_shapes=[pltpu.VMEM((tm, D), BF16)],
        compiler_params=pltpu.CompilerParams(
            dimension_semantics=("arbitrary", "arbitrary"), vmem_limit_bytes=48 * MIB),
        name="in_proj",
    )(x2d, mod3, mod3, norm_g, w_in)


def _na_table_kernel(rpb_ref, o_ref, *, n_rel_rows, n_rel_cols):
    h = pl.program_id(0)
    shape = (GRID_W, 2 * GRID_W)
    qc = lax.broadcasted_iota(jnp.int32, shape, 0)
    lane = lax.broadcasted_iota(jnp.int32, shape, 1)
    upper = lane >= GRID_W
    kc = jnp.where(upper, lane - GRID_W, lane)
    dc = jnp.clip(kc - qc, -(NA_WIN_COLS - 1), NA_WIN_COLS - 1) + NA_WIN_COLS - 1
    c0 = jnp.clip(qc - NA_WIN_COLS // 2, 0, GRID_W - NA_WIN_COLS)
    in_win = (kc >= c0) & (kc < c0 + NA_WIN_COLS)
    base = h * (n_rel_rows * n_rel_cols)
    for e in range(2 * NA_WIN_ROWS):
        val = jnp.full(shape, NEG, F32)
        for d in range(n_rel_cols):
            r_lo, r_hi = e - 1, e
            lo = rpb_ref[base + r_lo * n_rel_cols + d] * LOG2E if 0 <= r_lo < n_rel_rows else NEG
            hi = rpb_ref[base + r_hi * n_rel_cols + d] * LOG2E if 0 <= r_hi < n_rel_rows else NEG
            val = jnp.where(dc == d, jnp.where(upper, hi, lo), val)
        o_ref[0, e] = jnp.where(in_win, val, NEG)


def _na_table(rpb):
    H, nr, nc = rpb.shape
    kern = lambda *a: _na_table_kernel(*a, n_rel_rows=nr, n_rel_cols=nc)
    return pl.pallas_call(
        kern,
        out_shape=jax.ShapeDtypeStruct((H, 2 * NA_WIN_ROWS, GRID_W, 2 * GRID_W), F32),
        grid=(H,),
        in_specs=[pl.BlockSpec(memory_space=pltpu.SMEM)],
        out_specs=pl.BlockSpec((1, 2 * NA_WIN_ROWS, GRID_W, 2 * GRID_W), lambda h: (h, 0, 0, 0)),
        compiler_params=pltpu.CompilerParams(dimension_semantics=("arbitrary",)),
        name="na_table",
    )(rpb.reshape(-1))


def _na_window_start_row(blk, rows):
    return min(max(blk * NA_QROWS - (NA_KROWS - NA_QROWS) // 2, 0), rows - NA_KROWS)


def _na_block(q, kw, vw, tp_ref, blk, rows):
    lane = lax.broadcasted_iota(jnp.int32, (GRID_W, 2 * GRID_W), 1)
    kr0 = _na_window_start_row(blk, rows)
    s = _dot_nt(q, kw)
    p_rows, inv_l = [], []
    for i in range(NA_QROWS):
        qr = blk * NA_QROWS + i
        rs = min(max(qr - NA_WIN_ROWS // 2, 0), rows - NA_WIN_ROWS)
        tiles = {}
        for m in range(NA_KROWS // 2):
            ka = kr0 + 2 * m
            va = rs <= ka < rs + NA_WIN_ROWS
            vb = rs <= ka + 1 < rs + NA_WIN_ROWS
            if not (va or vb):
                continue
            e = ka - qr + NA_WIN_ROWS
            t = s[i * GRID_W:(i + 1) * GRID_W, m * 2 * GRID_W:(m + 1) * 2 * GRID_W] + tp_ref[0, e]
            if not vb:
                t = jnp.where(lane < GRID_W, t, NEG)
            elif not va:
                t = jnp.where(lane >= GRID_W, t, NEG)
            tiles[m] = t
        ts = list(tiles.values())
        mx = ts[0]
        for t in ts[1:]:
            mx = jnp.maximum(mx, t)
        mx = jnp.max(mx, axis=-1, keepdims=True)
        ps = {m: jnp.exp2(t - mx) for m, t in tiles.items()}
        tot = None
        for pt in ps.values():
            tot = pt if tot is None else tot + pt
        inv_l.append(1.0 / jnp.sum(tot, axis=-1, keepdims=True))
        zero = jnp.zeros((GRID_W, 2 * GRID_W), BF16)
        p_rows.append(jnp.concatenate(
            [ps[m].astype(BF16) if m in ps else zero for m in range(NA_KROWS // 2)], axis=1))
    p = jnp.concatenate(p_rows, axis=0)
    return _dot(p, vw) * jnp.concatenate(inv_l, axis=0)


def _na_kernel(q_ref, k_ref, v_ref, tp_ref, o_ref, *, rows):
    for blk in range(rows // NA_QROWS):
        start = _na_window_start_row(blk, rows) * GRID_W
        o = _na_block(q_ref[blk * NA_TQ:(blk + 1) * NA_TQ, :], k_ref[start:start + NA_TK, :],
                      v_ref[start:start + NA_TK, :], tp_ref, blk, rows)
        o_ref[blk * NA_TQ:(blk + 1) * NA_TQ, :] = o.astype(BF16)


def _na_attn(proj_hm, tp, *, batch, seq, heads, q_blk0, k_blk0, v_blk0):
    rows = seq // GRID_W
    kern = lambda *a: _na_kernel(*a, rows=rows)
    head_blk = lambda blk0: pl.BlockSpec((None, seq, HEAD_DIM), lambda b, h: (blk0 + h, b, 0))
    return pl.pallas_call(
        kern,
        out_shape=jax.ShapeDtypeStruct((batch * seq, heads * HEAD_DIM), BF16),
        grid=(batch, heads),
        in_specs=[
            head_blk(q_blk0), head_blk(k_blk0), head_blk(v_blk0),
            pl.BlockSpec((1, 2 * NA_WIN_ROWS, GRID_W, 2 * GRID_W), lambda b, h: (h, 0, 0, 0)),
        ],
        out_specs=pl.BlockSpec((seq, HEAD_DIM), lambda b, h: (b, h)),
        compiler_params=pltpu.CompilerParams(
            dimension_semantics=("arbitrary", "arbitrary"), vmem_limit_bytes=48 * MIB),
        name="na_attn",
    )(proj_hm, proj_hm, proj_hm, tp)


def _t5_bucket(rel):
    nb = T5_BUCKETS // 2
    ret = jnp.where(rel > 0, nb, 0)
    n = jnp.abs(rel)
    max_exact = nb // 2
    nf = jnp.maximum(n, 1).astype(jnp.float32)
    large = max_exact + (jnp.log(nf / max_exact) / math.log(T5_MAX_DIST / max_exact)
                         * (nb - max_exact)).astype(jnp.int32)
    large = jnp.minimum(large, nb - 1)
    return ret + jnp.where(n < max_exact, n, large)


def _da_table_kernel(bucket_ref, t5_ref, o_ref, *, heads):
    h = pl.program_id(0)
    shape = (DA_TQ, DA_TQ)
    nb = T5_BUCKETS // 2
    o_ref[0, 0] = jnp.full(shape, t5_ref[(nb - 1) * heads + h] * LOG2E, F32)
    o_ref[0, 4] = jnp.full(shape, t5_ref[(T5_BUCKETS - 1) * heads + h] * LOG2E, F32)
    for s in range(3):
        bk = bucket_ref[s]
        val = jnp.zeros(shape, F32)
        for bkt in range(T5_BUCKETS):
            val = jnp.where(bk == bkt, t5_ref[bkt * heads + h] * LOG2E, val)
        o_ref[0, s + 1] = val


def _da_table(t5_table):
    heads = t5_table.shape[1]
    assert DA_TQ >= T5_MAX_DIST
    i = jnp.arange(DA_TQ)[:, None]
    c = jnp.arange(DA_TQ)[None, :]
    bucket = jnp.stack([_t5_bucket((dj * DA_TQ + c) - i) for dj in (-1, 0, 1)]).astype(jnp.int32)
    kern = lambda *a: _da_table_kernel(*a, heads=heads)
    return pl.pallas_call(
        kern,
        out_shape=jax.ShapeDtypeStruct((heads, 5, DA_TQ, DA_TQ), F32),
        grid=(heads,),
        in_specs=[
            pl.BlockSpec((3, DA_TQ, DA_TQ), lambda h: (0, 0, 0)),
            pl.BlockSpec(memory_space=pltpu.SMEM),
        ],
        out_specs=pl.BlockSpec((1, 5, DA_TQ, DA_TQ), lambda h: (h, 0, 0, 0)),
        compiler_params=pltpu.CompilerParams(dimension_semantics=("arbitrary",)),
        name="da_table",
    )(bucket, t5_table.reshape(-1))


def _da_kernel(q_ref, k_ref, v_ref, tb_ref, lam_ref, g_ref, o_ref, vext_ref, s_ref, p_ref, *, seq):
    n_kt = seq // DA_TQ

    vext_ref[:, :HEAD_DIM] = v_ref[...]
    lane_v = lax.broadcasted_iota(jnp.int32, (seq, HEAD_DIM), 1)
    vext_ref[:, HEAD_DIM:] = jnp.where(lane_v == 0, 1.0, 0.0).astype(BF16)

    lp = lam_ref[...]
    t1 = jnp.sum(lp[0:1] * lp[1:2], axis=-1, keepdims=True)
    t2 = jnp.sum(lp[2:3] * lp[3:4], axis=-1, keepdims=True)
    lam = jnp.exp(t1) - jnp.exp(t2) + LAM_INIT

    c_left = tb_ref[0, 0, 0:1, 0:1]
    c_right = tb_ref[0, 4, 0:1, 0:1]
    lane = lax.broadcasted_iota(jnp.int32, (DA_TQ, HEAD_DIM), 1)

    def key_cols(j):
        return slice(j * DA_TQ, (j + 1) * DA_TQ)

    def lane_max(parts):
        acc = parts[0]
        for part in parts[1:]:
            acc = jnp.maximum(acc, part)
        return jnp.max(acc, axis=-1, keepdims=True)

    def qk_scores(t):
        q = q_ref[t * DA_TQ:(t + 1) * DA_TQ, :]
        zero = jnp.zeros_like(q)
        q12 = jnp.concatenate(
            [jnp.where(lane < DA_QK_DIM, q, zero), jnp.where(lane >= DA_QK_DIM, q, zero)], axis=0)
        for j in range(n_kt):
            s_ref[t % 2, :, key_cols(j)] = _dot_nt(q12, k_ref[key_cols(j), :])

    qk_scores(0)
    for t in range(n_kt):
        buf = t % 2
        if t + 1 < n_kt:
            qk_scores(t + 1)

        def scores(j):
            s = s_ref[buf, :, key_cols(j)]
            if abs(j - t) <= 1:
                bias = tb_ref[0, j - t + 2]
                s = s + jnp.concatenate([bias, bias], axis=0)
            return s

        def halves(j):
            s = scores(j)
            return [s[:, i * 128:(i + 1) * 128] for i in range(DA_TQ // 128)]

        near = [j for j in range(n_kt) if abs(j - t) <= 1]
        left = [j for j in range(n_kt) if j < t - 1]
        right = [j for j in range(n_kt) if j > t + 1]
        m = lane_max([part for j in near for part in halves(j)])
        if left:
            m = jnp.maximum(m, lane_max([part for j in left for part in halves(j)]) + c_left)
        if right:
            m = jnp.maximum(m, lane_max([part for j in right for part in halves(j)]) + c_right)

        for j in range(n_kt):
            shift = m if abs(j - t) <= 1 else (m - c_left if j < t else m - c_right)
            p_ref[buf, :, key_cols(j)] = jnp.exp2(scores(j) - shift).astype(BF16)
        oe1 = _dot(p_ref[buf, :DA_TQ, :], vext_ref[...])
        oe2 = _dot(p_ref[buf, DA_TQ:, :], vext_ref[...])
        o1, l1 = oe1[:, :HEAD_DIM], oe1[:, HEAD_DIM:HEAD_DIM + 1]
        o2, l2 = oe2[:, :HEAD_DIM], oe2[:, HEAD_DIM:HEAD_DIM + 1]
        o = o1 * (1.0 / l1) - o2 * (lam / l2)
        ms = jnp.mean(o * o, axis=-1, keepdims=True)
        y = (o * lax.rsqrt(ms + EPS) * g_ref[...]) * (1.0 - LAM_INIT)
        o_ref[t * DA_TQ:(t + 1) * DA_TQ, :] = y.astype(BF16)


def _da_attn(proj_hm, tb, da_lambda, subln_g, *, batch, seq, heads, q_blk0, k_blk0, v_blk0):
    kern = lambda *a: _da_kernel(*a, seq=seq)
    head_blk = lambda blk0: pl.BlockSpec((None, seq, HEAD_DIM), lambda b, h: (blk0 + h, b, 0))
    return pl.pallas_call(
        kern,
        out_shape=jax.ShapeDtypeStruct((batch * seq, heads * HEAD_DIM), BF16),
        grid=(batch, heads),
        in_specs=[
            head_blk(q_blk0), head_blk(k_blk0), head_blk(v_blk0),
            pl.BlockSpec((1, 5, DA_TQ, DA_TQ), lambda b, h: (h, 0, 0, 0)),
            pl.BlockSpec(da_lambda.shape, lambda b, h: (0, 0)),
            pl.BlockSpec((1, HEAD_DIM), lambda b, h: (0, 0)),
        ],
        out_specs=pl.BlockSpec((seq, HEAD_DIM), lambda b, h: (b, h)),
        scratch_shapes=[
            pltpu.VMEM((seq, 2 * HEAD_DIM), BF16),
            pltpu.VMEM((2, 2 * DA_TQ, seq), F32),
            pltpu.VMEM((2, 2 * DA_TQ, seq), BF16),
        ],
        compiler_params=pltpu.CompilerParams(
            dimension_semantics=("arbitrary", "arbitrary"), vmem_limit_bytes=48 * MIB),
        name="da_attn",
    )(proj_hm, proj_hm, proj_hm, tb, da_lambda, subln_g)


def _outproj_kernel(na_ref, da_ref, nag_ref, w_ref, x_ref, g1_ref, o_ref, wbf_ref, nan_ref):
    kh = na_ref.shape[1]

    @pl.when(pl.program_id(0) == 0)
    def _():
        def body(r, carry):
            rows = pl.ds(pl.multiple_of(r * WCAST_ROW_CHUNK, WCAST_ROW_CHUNK), WCAST_ROW_CHUNK)
            wbf_ref[rows, :] = w_ref[rows, :].astype(BF16)
            return carry
        lax.fori_loop(0, w_ref.shape[0] // WCAST_ROW_CHUNK, body, 0)

    gain = nag_ref[...]

    def body(r, carry):
        rows = pl.ds(pl.multiple_of(r * NORM_ROW_CHUNK, NORM_ROW_CHUNK), NORM_ROW_CHUNK)
        o = na_ref[rows, :].astype(F32)
        ms = jnp.mean(o * o, axis=-1, keepdims=True)
        nan_ref[rows, :] = (o * lax.rsqrt(ms + EPS) * gain).astype(BF16)
        return carry
    lax.fori_loop(0, na_ref.shape[0] // NORM_ROW_CHUNK, body, 0)

    mix = _dot(nan_ref[...], wbf_ref[:kh, :]) + _dot(da_ref[...], wbf_ref[kh:, :])
    o_ref[...] = x_ref[...] + g1_ref[0] * mix


def _out_proj(na_o, da_o, na_out_g, w_out, x2d, mod3, *, seq):
    M, D = x2d.shape
    Kh = na_o.shape[1]
    tm = 512
    per_b = seq // tm
    return pl.pallas_call(
        _outproj_kernel,
        out_shape=jax.ShapeDtypeStruct((M, D), F32),
        grid=(M // tm,),
        in_specs=[
            pl.BlockSpec((tm, Kh), lambda i: (i, 0)),
            pl.BlockSpec((tm, Kh), lambda i: (i, 0)),
            pl.BlockSpec((1, Kh), lambda i: (0, 0)),
            pl.BlockSpec(w_out.shape, lambda i: (0, 0), pipeline_mode=pl.Buffered(1)),
            pl.BlockSpec((tm, D), lambda i: (i, 0)),
            pl.BlockSpec((1, 1, D), lambda i: ((i // per_b) * 6 + 2, 0, 0)),
        ],
        out_specs=pl.BlockSpec((tm, D), lambda i: (i, 0)),
        scratch_shapes=[pltpu.VMEM(w_out.shape, BF16), pltpu.VMEM((tm, Kh), BF16)],
        compiler_params=pltpu.CompilerParams(
            dimension_semantics=("arbitrary",), vmem_limit_bytes=56 * MIB),
        name="out_proj",
    )(na_o, da_o, na_out_g, w_out, x2d, mod3)


def _ffn_kernel(x_ref, sh_ref, sc_ref, g2_ref, ng_ref, fg_ref, wg_ref, wu_ref, wd_ref, o_ref, h_ref,
                *, n_f):
    f = pl.program_id(1)
    n_chunks = x_ref.shape[0] // FFN_ROW_CHUNK

    def chunk_rows(r):
        return pl.ds(pl.multiple_of(r * FFN_ROW_CHUNK, FFN_ROW_CHUNK), FFN_ROW_CHUNK)

    def norm_rows(r):
        return pl.ds(pl.multiple_of(r * NORM_ROW_CHUNK, NORM_ROW_CHUNK), NORM_ROW_CHUNK)

    n_norm_chunks = x_ref.shape[0] // NORM_ROW_CHUNK

    @pl.when(f == 0)
    def _():
        gain = ng_ref[...] * (1.0 + sc_ref[0])
        shift = sh_ref[0]

        def body(r, carry):
            rows = norm_rows(r)
            x = x_ref[rows, :]
            ms = jnp.mean(x * x, axis=-1, keepdims=True)
            h_ref[rows, :] = (x * lax.rsqrt(ms + EPS) * gain + shift).astype(BF16)
            o_ref[rows, :] = jnp.zeros((NORM_ROW_CHUNK, o_ref.shape[1]), F32)
            return carry
        lax.fori_loop(0, n_norm_chunks, body, 0)

    def body(r, carry):
        rows = chunk_rows(r)
        h = h_ref[rows, :]
        g = _dot(h, wg_ref[...].astype(BF16))
        u = _dot(h, wu_ref[...].astype(BF16))
        a = (g * _sigmoid(g) * u).astype(BF16)
        o_ref[rows, :] += _dot(a, wd_ref[...].astype(BF16))
        return carry
    lax.fori_loop(0, n_chunks, body, 0, unroll=True)

    @pl.when(f == n_f - 1)
    def _():
        gate = g2_ref[0]
        final_gain = fg_ref[...]

        def body(r, carry):
            rows = norm_rows(r)
            x2 = x_ref[rows, :] + gate * o_ref[rows, :]
            ms = jnp.mean(x2 * x2, axis=-1, keepdims=True)
            o_ref[rows, :] = x2 * lax.rsqrt(ms + EPS) * final_gain
            return carry
        lax.fori_loop(0, n_norm_chunks, body, 0)


def _ffn(x1, mod3, norm_g, final_g, w_gate, w_up, w_down, *, seq):
    M, D = x1.shape
    F = w_gate.shape[1]
    tm, tf = 1024, 512
    per_b = seq // tm
    n_f = F // tf
    kern = lambda *a: _ffn_kernel(*a, n_f=n_f)
    return pl.pallas_call(
        kern,
        out_shape=jax.ShapeDtypeStruct((M, D), F32),
        grid=(M // tm, n_f),
        in_specs=[
            pl.BlockSpec((tm, D), lambda i, f: (i, 0), pipeline_mode=pl.Buffered(1)),
            pl.BlockSpec((1, 1, D), lambda i, f: ((i // per_b) * 6 + 3, 0, 0)),
            pl.BlockSpec((1, 1, D), lambda i, f: ((i // per_b) * 6 + 4, 0, 0)),
            pl.BlockSpec((1, 1, D), lambda i, f: ((i // per_b) * 6 + 5, 0, 0)),
            pl.BlockSpec((1, D), lambda i, f: (0, 0)),
            pl.BlockSpec((1, D), lambda i, f: (0, 0)),
            pl.BlockSpec((D, tf), lambda i, f: (0, f)),
            pl.BlockSpec((D, tf), lambda i, f: (0, f)),
            pl.BlockSpec((tf, D), lambda i, f: (f, 0)),
        ],
        out_specs=pl.BlockSpec((tm, D), lambda i, f: (i, 0)),
        scratch_shapes=[pltpu.VMEM((tm, D), BF16)],
        compiler_params=pltpu.CompilerParams(
            dimension_semantics=("arbitrary", "arbitrary"), vmem_limit_bytes=60 * MIB),
        name="ffn",
    )(x1, mod3, mod3, mod3, norm_g, final_g, w_gate, w_up, w_down)


def kernel(x, c, w_ada, b_ada, norm1_g, w_in, na_rpb, na_out_g, da_lambda, da_subln_g, t5_table,
           w_out, norm2_g, w_gate, w_up, w_down, final_g):
    B, S, D = x.shape
    na_heads = na_rpb.shape[1]
    da_heads = t5_table.shape[1]
    na_width = na_heads * HEAD_DIM
    assert w_ada.shape[0] == 1, "single layer"

    mod = _ada_mod(c, w_ada[0], b_ada[0])
    mod3 = mod.reshape(B * 6, 1, D)
    x2d = x.reshape(B * S, D)

    proj_hm = _in_proj(x2d, mod3, norm1_g, w_in[0], seq=S,
                       na_width=na_width, da_q_off=3 * na_width)
    na_o = _na_attn(proj_hm, _na_table(na_rpb[0]), batch=B, seq=S, heads=na_heads,
                    q_blk0=0, k_blk0=na_heads, v_blk0=2 * na_heads)
    da_o = _da_attn(proj_hm, _da_table(t5_table), da_lambda[0], da_subln_g, batch=B, seq=S,
                    heads=da_heads, q_blk0=3 * na_heads, k_blk0=3 * na_heads + da_heads,
                    v_blk0=3 * na_heads + 2 * da_heads)
    x1 = _out_proj(na_o, da_o, na_out_g, w_out[0], x2d, mod3, seq=S)
    out = _ffn(x1, mod3, norm2_g, final_g.reshape(1, D), w_gate[0], w_up[0], w_down[0], seq=S)
    return out.reshape(B, S, D)
```

```python
import math

import jax
import jax.numpy as jnp
from jax import lax
from jax.experimental import pallas as pl
from jax.experimental.pallas import tpu as pltpu

F32 = jnp.float32
BF16 = jnp.bfloat16

EPS = 1e-6
NEG = -1e30

GRID_W = 64
HEAD_DIM = 128
DA_QK_DIM = 64
NA_WIN_ROWS = 8
NA_WIN_COLS = 16
T5_BUCKETS = 32
T5_MAX_DIST = 128
LAM_INIT = 0.8 - 0.6 * math.exp(-0.3 * 0)
LOG2E = math.log2(math.e)

MIB = 1024 * 1024

NA_QROWS = 8
NA_KROWS = 16
NA_TQ = NA_QROWS * GRID_W
NA_TK = NA_KROWS * GRID_W

DA_TQ = 256

FFN_ROW_CHUNK = 256
NORM_ROW_CHUNK = 128
WCAST_ROW_CHUNK = 128


def _dot(a, b):
    return jnp.dot(a, b, preferred_element_type=F32)


def _dot_nt(a, b):
    return lax.dot_general(a, b, (((1,), (1,)), ((), ())), preferred_element_type=F32)


def _sigmoid(x):
    return 1.0 / (1.0 + jnp.exp(-x))


def _ada_kernel(c_ref, w_ref, b_ref, o_ref):
    c = c_ref[...]
    cs = c * _sigmoid(c)
    o_ref[...] = _dot(cs.astype(BF16), w_ref[...].astype(BF16)) + b_ref[...]


def _ada_mod(c, w_ada, b_ada):
    B, D = c.shape
    N = w_ada.shape[1]
    tn = 1024
    rows = 8
    c_pad = jnp.zeros((rows, D), F32).at[:B].set(c)
    out = pl.pallas_call(
        _ada_kernel,
        out_shape=jax.ShapeDtypeStruct((rows, N), F32),
        grid=(N // tn,),
        in_specs=[
            pl.BlockSpec((rows, D), lambda j: (0, 0)),
            pl.BlockSpec((D, tn), lambda j: (0, j)),
            pl.BlockSpec((1, tn), lambda j: (0, j)),
        ],
        out_specs=pl.BlockSpec((rows, tn), lambda j: (0, j)),
        compiler_params=pltpu.CompilerParams(
            dimension_semantics=("arbitrary",), vmem_limit_bytes=40 * MIB),
        name="ada_mod",
    )(c_pad, w_ada, b_ada.reshape(1, N))
    return out[:B]


def _modulated_norm(x, g, sc, sh):
    ms = jnp.mean(x * x, axis=-1, keepdims=True)
    return (x * lax.rsqrt(ms + EPS) * g) * (1.0 + sc) + sh


def _inproj_kernel(x_ref, sh_ref, sc_ref, g_ref, w_ref, o_ref, h_ref, *, na_q_blk, da_q_blk,
                   na_scale, da_scale):
    j = pl.program_id(1)

    @pl.when(j == 0)
    def _():
        gain = g_ref[...] * (1.0 + sc_ref[0])
        shift = sh_ref[0]

        def body(r, carry):
            rows = pl.ds(pl.multiple_of(r * NORM_ROW_CHUNK, NORM_ROW_CHUNK), NORM_ROW_CHUNK)
            x = x_ref[rows, :]
            ms = jnp.mean(x * x, axis=-1, keepdims=True)
            h_ref[rows, :] = (x * lax.rsqrt(ms + EPS) * gain + shift).astype(BF16)
            return carry
        lax.fori_loop(0, x_ref.shape[0] // NORM_ROW_CHUNK, body, 0)

    acc = _dot(h_ref[...], w_ref[...].astype(BF16))
    scale = jnp.where(j == na_q_blk, na_scale, jnp.where(j == da_q_blk, da_scale, 1.0))
    acc = acc * scale
    for k in range(o_ref.shape[0]):
        o_ref[k] = acc[:, k * HEAD_DIM:(k + 1) * HEAD_DIM].astype(BF16)


def _in_proj(x2d, mod3, norm_g, w_in, *, seq, na_width, da_q_off):
    M, D = x2d.shape
    N = w_in.shape[1]
    tm, tn = 1024, 1024
    per_b = seq // tm
    kern = lambda *a: _inproj_kernel(
        *a, na_q_blk=0, da_q_blk=da_q_off // tn,
        na_scale=HEAD_DIM ** -0.5 * LOG2E, da_scale=DA_QK_DIM ** -0.5 * LOG2E)
    assert na_width == tn
    return pl.pallas_call(
        kern,
        out_shape=jax.ShapeDtypeStruct((N // HEAD_DIM, M, HEAD_DIM), BF16),
        grid=(M // tm, N // tn),
        in_specs=[
            pl.BlockSpec((tm, D), lambda i, j: (i, 0)),
            pl.BlockSpec((1, 1, D), lambda i, j: ((i // per_b) * 6 + 0, 0, 0)),
            pl.BlockSpec((1, 1, D), lambda i, j: ((i // per_b) * 6 + 1, 0, 0)),
            pl.BlockSpec((1, D), lambda i, j: (0, 0)),
            pl.BlockSpec((D, tn), lambda i, j: (0, j)),
        ],
        out_specs=pl.BlockSpec((tn // HEAD_DIM, tm, HEAD_DIM), lambda i, j: (j, i, 0)),
        scratch_shapes=[pltpu.VMEM((tm, D), BF16)],
        compiler_params=pltpu.CompilerParams(
            dimension_semantics=("arbitrary", "arbitrary"), vmem_limit_bytes=48 * MIB),
        name="in_proj",
    )(x2d, mod3, mod3, norm_g, w_in)


def _na_table_kernel(rpb_ref, o_ref, *, n_rel_rows, n_rel_cols):
    h = pl.program_id(0)
    shape = (GRID_W, 2 * GRID_W)
    qc = lax.broadcasted_iota(jnp.int32, shape, 0)
    lane = lax.broadcasted_iota(jnp.int32, shape, 1)
    upper = lane >= GRID_W
    kc = jnp.where(upper, lane - GRID_W, lane)
    dc = jnp.clip(kc - qc, -(NA_WIN_COLS - 1), NA_WIN_COLS - 1) + NA_WIN_COLS - 1
    c0 = jnp.clip(qc - NA_WIN_COLS // 2, 0, GRID_W - NA_WIN_COLS)
    in_win = (kc >= c0) & (kc < c0 + NA_WIN_COLS)
    base = h * (n_rel_rows * n_rel_cols)
    for e in range(2 * NA_WIN_ROWS):
        val = jnp.full(shape, NEG, F32)
        for d in range(n_rel_cols):
            r_lo, r_hi = e - 1, e
            lo = rpb_ref[base + r_lo * n_rel_cols + d] * LOG2E if 0 <= r_lo < n_rel_rows else NEG
            hi = rpb_ref[base + r_hi * n_rel_cols + d] * LOG2E if 0 <= r_hi < n_rel_rows else NEG
            val = jnp.where(dc == d, jnp.where(upper, hi, lo), val)
        o_ref[0, e] = jnp.where(in_win, val, NEG)


def _na_table(rpb):
    H, nr, nc = rpb.shape
    kern = lambda *a: _na_table_kernel(*a, n_rel_rows=nr, n_rel_cols=nc)
    return pl.pallas_call(
        kern,
        out_shape=jax.ShapeDtypeStruct((H, 2 * NA_WIN_ROWS, GRID_W, 2 * GRID_W), F32),
        grid=(H,),
        in_specs=[pl.BlockSpec(memory_space=pltpu.SMEM)],
        out_specs=pl.BlockSpec((1, 2 * NA_WIN_ROWS, GRID_W, 2 * GRID_W), lambda h: (h, 0, 0, 0)),
        compiler_params=pltpu.CompilerParams(dimension_semantics=("arbitrary",)),
        name="na_table",
    )(rpb.reshape(-1))


def _na_window_start_row(blk, rows):
    return min(max(blk * NA_QROWS - (NA_KROWS - NA_QROWS) // 2, 0), rows - NA_KROWS)


def _na_block(q, kw, vw, tp_ref, blk, rows):
    lane = lax.broadcasted_iota(jnp.int32, (GRID_W, 2 * GRID_W), 1)
    kr0 = _na_window_start_row(blk, rows)
    s = _dot_nt(q, kw)
    p_rows, inv_l = [], []
    for i in range(NA_QROWS):
        qr = blk * NA_QROWS + i
        rs = min(max(qr - NA_WIN_ROWS // 2, 0), rows - NA_WIN_ROWS)
        tiles = {}
        for m in range(NA_KROWS // 2):
            ka = kr0 + 2 * m
            va = rs <= ka < rs + NA_WIN_ROWS
            vb = rs <= ka + 1 < rs + NA_WIN_ROWS
            if not (va or vb):
                continue
            e = ka - qr + NA_WIN_ROWS
            t = s[i * GRID_W:(i + 1) * GRID_W, m * 2 * GRID_W:(m + 1) * 2 * GRID_W] + tp_ref[0, e]
            if not vb:
                t = jnp.where(lane < GRID_W, t, NEG)
            elif not va:
                t = jnp.where(lane >= GRID_W, t, NEG)
            tiles[m] = t
        ts = list(tiles.values())
        mx = ts[0]
        for t in ts[1:]:
            mx = jnp.maximum(mx, t)
        mx = jnp.max(mx, axis=-1, keepdims=True)
        ps = {m: jnp.exp2(t - mx) for m, t in tiles.items()}
        tot = None
        for pt in ps.values():
            tot = pt if tot is None else tot + pt
        inv_l.append(1.0 / jnp.sum(tot, axis=-1, keepdims=True))
        zero = jnp.zeros((GRID_W, 2 * GRID_W), BF16)
        p_rows.append(jnp.concatenate(
            [ps[m].astype(BF16) if m in ps else zero for m in range(NA_KROWS // 2)], axis=1))
    p = jnp.concatenate(p_rows, axis=0)
    return _dot(p, vw) * jnp.concatenate(inv_l, axis=0)


def _na_kernel(q_ref, k_ref, v_ref, tp_ref, o_ref, *, rows):
    for blk in range(rows // NA_QROWS):
        start = _na_window_start_row(blk, rows) * GRID_W
        o = _na_block(q_ref[blk * NA_TQ:(blk + 1) * NA_TQ, :], k_ref[start:start + NA_TK, :],
                      v_ref[start:start + NA_TK, :], tp_ref, blk, rows)
        o_ref[blk * NA_TQ:(blk + 1) * NA_TQ, :] = o.astype(BF16)


def _na_attn(proj_hm, tp, *, batch, seq, heads, q_blk0, k_blk0, v_blk0):
    rows = seq // GRID_W
    kern = lambda *a: _na_kernel(*a, rows=rows)
    head_blk = lambda blk0: pl.BlockSpec((None, seq, HEAD_DIM), lambda b, h: (blk0 + h, b, 0))
    return pl.pallas_call(
        kern,
        out_shape=jax.ShapeDtypeStruct((batch * seq, heads * HEAD_DIM), BF16),
        grid=(batch, heads),
        in_specs=[
            head_blk(q_blk0), head_blk(k_blk0), head_blk(v_blk0),
            pl.BlockSpec((1, 2 * NA_WIN_ROWS, GRID_W, 2 * GRID_W), lambda b, h: (h, 0, 0, 0)),
        ],
        out_specs=pl.BlockSpec((seq, HEAD_DIM), lambda b, h: (b, h)),
        compiler_params=pltpu.CompilerParams(
            dimension_semantics=("arbitrary", "arbitrary"), vmem_limit_bytes=48 * MIB),
        name="na_attn",
    )(proj_hm, proj_hm, proj_hm, tp)


def _t5_bucket(rel):
    nb = T5_BUCKETS // 2
    ret = jnp.where(rel > 0, nb, 0)
    n = jnp.abs(rel)
    max_exact = nb // 2
    nf = jnp.maximum(n, 1).astype(jnp.float32)
    large = max_exact + (jnp.log(nf / max_exact) / math.log(T5_MAX_DIST / max_exact)
                         * (nb - max_exact)).astype(jnp.int32)
    large = jnp.minimum(large, nb - 1)
    return ret + jnp.where(n < max_exact, n, large)


def _da_table_kernel(bucket_ref, t5_ref, o_ref, *, heads):
    h = pl.program_id(0)
    shape = (DA_TQ, DA_TQ)
    nb = T5_BUCKETS // 2
    o_ref[0, 0] = jnp.full(shape, t5_ref[(nb - 1) * heads + h] * LOG2E, F32)
    o_ref[0, 4] = jnp.full(shape, t5_ref[(T5_BUCKETS - 1) * heads + h] * LOG2E, F32)
    for s in range(3):
        bk = bucket_ref[s]
        val = jnp.zeros(shape, F32)
        for bkt in range(T5_BUCKETS):
            val = jnp.where(bk == bkt, t5_ref[bkt * heads + h] * LOG2E, val)
        o_ref[0, s + 1] = val


def _da_table(t5_table):
    heads = t5_table.shape[1]
    assert DA_TQ >= T5_MAX_DIST
    i = jnp.arange(DA_TQ)[:, None]
    c = jnp.arange(DA_TQ)[None, :]
    bucket = jnp.stack([_t5_bucket((dj * DA_TQ + c) - i) for dj in (-1, 0, 1)]).astype(jnp.int32)
    kern = lambda *a: _da_table_kernel(*a, heads=heads)
    return pl.pallas_call(
        kern,
        out_shape=jax.ShapeDtypeStruct((heads, 5, DA_TQ, DA_TQ), F32),
        grid=(heads,),
        in_specs=[
            pl.BlockSpec((3, DA_TQ, DA_TQ), lambda h: (0, 0, 0)),
            pl.BlockSpec(memory_space=pltpu.SMEM),
        ],
        out_specs=pl.BlockSpec((1, 5, DA_TQ, DA_TQ), lambda h: (h, 0, 0, 0)),
        compiler_params=pltpu.CompilerParams(dimension_semantics=("arbitrary",)),
        name="da_table",
    )(bucket, t5_table.reshape(-1))


def _da_kernel(q_ref, k_ref, v_ref, tb_ref, lam_ref, g_ref, wg_ref, wu_ref, wd_ref,
               o_ref, wg_bf_ref, wu_bf_ref, wd_bf_ref, vext_ref, s_ref, p_ref, *, seq):
    n_kt = seq // DA_TQ

    wg_bf_ref[...] = wg_ref[...].astype(BF16)
    wu_bf_ref[...] = wu_ref[...].astype(BF16)
    wd_bf_ref[...] = wd_ref[...].astype(BF16)

    vext_ref[:, :HEAD_DIM] = v_ref[...]
    lane_v = lax.broadcasted_iota(jnp.int32, (seq, HEAD_DIM), 1)
    vext_ref[:, HEAD_DIM:] = jnp.where(lane_v == 0, 1.0, 0.0).astype(BF16)

    lp = lam_ref[...]
    t1 = jnp.sum(lp[0:1] * lp[1:2], axis=-1, keepdims=True)
    t2 = jnp.sum(lp[2:3] * lp[3:4], axis=-1, keepdims=True)
    lam = jnp.exp(t1) - jnp.exp(t2) + LAM_INIT

    c_left = tb_ref[0, 0, 0:1, 0:1]
    c_right = tb_ref[0, 4, 0:1, 0:1]
    lane = lax.broadcasted_iota(jnp.int32, (DA_TQ, HEAD_DIM), 1)

    def key_cols(j):
        return slice(j * DA_TQ, (j + 1) * DA_TQ)

    def lane_max(parts):
        acc = parts[0]
        for part in parts[1:]:
            acc = jnp.maximum(acc, part)
        return jnp.max(acc, axis=-1, keepdims=True)

    def qk_scores(t):
        q = q_ref[t * DA_TQ:(t + 1) * DA_TQ, :]
        zero = jnp.zeros_like(q)
        q12 = jnp.concatenate(
            [jnp.where(lane < DA_QK_DIM, q, zero), jnp.where(lane >= DA_QK_DIM, q, zero)], axis=0)
        for j in range(n_kt):
            s_ref[t % 2, :, key_cols(j)] = _dot_nt(q12, k_ref[key_cols(j), :])

    qk_scores(0)
    for t in range(n_kt):
        buf = t % 2
        if t + 1 < n_kt:
            qk_scores(t + 1)

        def scores(j):
            s = s_ref[buf, :, key_cols(j)]
            if abs(j - t) <= 1:
                bias = tb_ref[0, j - t + 2]
                s = s + jnp.concatenate([bias, bias], axis=0)
            return s

        def halves(j):
            s = scores(j)
            return [s[:, i * 128:(i + 1) * 128] for i in range(DA_TQ // 128)]

        near = [j for j in range(n_kt) if abs(j - t) <= 1]
        left = [j for j in range(n_kt) if j < t - 1]
        right = [j for j in range(n_kt) if j > t + 1]
        m = lane_max([part for j in near for part in halves(j)])
        if left:
            m = jnp.maximum(m, lane_max([part for j in left for part in halves(j)]) + c_left)
        if right:
            m = jnp.maximum(m, lane_max([part for j in right for part in halves(j)]) + c_right)

        for j in range(n_kt):
            shift = m if abs(j - t) <= 1 else (m - c_left if j < t else m - c_right)
            p_ref[buf, :, key_cols(j)] = jnp.exp2(scores(j) - shift).astype(BF16)
        oe1 = _dot(p_ref[buf, :DA_TQ, :], vext_ref[...])
        oe2 = _dot(p_ref[buf, DA_TQ:, :], vext_ref[...])
        o1, l1 = oe1[:, :HEAD_DIM], oe1[:, HEAD_DIM:HEAD_DIM + 1]
        o2, l2 = oe2[:, :HEAD_DIM], oe2[:, HEAD_DIM:HEAD_DIM + 1]
        o = o1 * (1.0 / l1) - o2 * (lam / l2)
        ms = jnp.mean(o * o, axis=-1, keepdims=True)
        y = (o * lax.rsqrt(ms + EPS) * g_ref[...]) * (1.0 - LAM_INIT)
        o_ref[t * DA_TQ:(t + 1) * DA_TQ, :] = y.astype(BF16)


def _da_attn(proj_hm, tb, da_lambda, subln_g, w_gate, w_up, w_down, *, batch, seq, heads,
             q_blk0, k_blk0, v_blk0):
    kern = lambda *a: _da_kernel(*a, seq=seq)
    head_blk = lambda blk0: pl.BlockSpec((None, seq, HEAD_DIM), lambda b, h: (blk0 + h, b, 0))
    n_steps = batch * heads

    def row_slab(w):
        assert w.shape[0] % (16 * n_steps) == 0
        return pl.BlockSpec((w.shape[0] // n_steps, w.shape[1]), lambda b, h: (b * heads + h, 0))

    weights = (w_gate, w_up, w_down)
    return pl.pallas_call(
        kern,
        out_shape=(jax.ShapeDtypeStruct((batch * seq, heads * HEAD_DIM), BF16),
                   *[jax.ShapeDtypeStruct(w.shape, BF16) for w in weights]),
        grid=(batch, heads),
        in_specs=[
            head_blk(q_blk0), head_blk(k_blk0), head_blk(v_blk0),
            pl.BlockSpec((1, 5, DA_TQ, DA_TQ), lambda b, h: (h, 0, 0, 0)),
            pl.BlockSpec(da_lambda.shape, lambda b, h: (0, 0)),
            pl.BlockSpec((1, HEAD_DIM), lambda b, h: (0, 0)),
            *[row_slab(w) for w in weights],
        ],
        out_specs=(pl.BlockSpec((seq, HEAD_DIM), lambda b, h: (b, h)),
                   *[row_slab(w) for w in weights]),
        scratch_shapes=[
            pltpu.VMEM((seq, 2 * HEAD_DIM), BF16),
            pltpu.VMEM((2, 2 * DA_TQ, seq), F32),
            pltpu.VMEM((2, 2 * DA_TQ, seq), BF16),
        ],
        compiler_params=pltpu.CompilerParams(
            dimension_semantics=("arbitrary", "arbitrary"), vmem_limit_bytes=48 * MIB),
        name="da_attn",
    )(proj_hm, proj_hm, proj_hm, tb, da_lambda, subln_g, *weights)


def _outproj_kernel(na_ref, da_ref, nag_ref, w_ref, x_ref, g1_ref, o_ref, wbf_ref, nan_ref):
    kh = na_ref.shape[1]

    @pl.when(pl.program_id(0) == 0)
    def _():
        def body(r, carry):
            rows = pl.ds(pl.multiple_of(r * WCAST_ROW_CHUNK, WCAST_ROW_CHUNK), WCAST_ROW_CHUNK)
            wbf_ref[rows, :] = w_ref[rows, :].astype(BF16)
            return carry
        lax.fori_loop(0, w_ref.shape[0] // WCAST_ROW_CHUNK, body, 0)

    gain = nag_ref[...]

    def body(r, carry):
        rows = pl.ds(pl.multiple_of(r * NORM_ROW_CHUNK, NORM_ROW_CHUNK), NORM_ROW_CHUNK)
        o = na_ref[rows, :].astype(F32)
        ms = jnp.mean(o * o, axis=-1, keepdims=True)
        nan_ref[rows, :] = (o * lax.rsqrt(ms + EPS) * gain).astype(BF16)
        return carry
    lax.fori_loop(0, na_ref.shape[0] // NORM_ROW_CHUNK, body, 0)

    mix = _dot(nan_ref[...], wbf_ref[:kh, :]) + _dot(da_ref[...], wbf_ref[kh:, :])
    o_ref[...] = x_ref[...] + g1_ref[0] * mix


def _out_proj(na_o, da_o, na_out_g, w_out, x2d, mod3, *, seq):
    M, D = x2d.shape
    Kh = na_o.shape[1]
    tm = 512
    per_b = seq // tm
    return pl.pallas_call(
        _outproj_kernel,
        out_shape=jax.ShapeDtypeStruct((M, D), F32),
        grid=(M // tm,),
        in_specs=[
            pl.BlockSpec((tm, Kh), lambda i: (i, 0)),
            pl.BlockSpec((tm, Kh), lambda i: (i, 0)),
            pl.BlockSpec((1, Kh), lambda i: (0, 0)),
            pl.BlockSpec(w_out.shape, lambda i: (0, 0), pipeline_mode=pl.Buffered(1)),
            pl.BlockSpec((tm, D), lambda i: (i, 0)),
            pl.BlockSpec((1, 1, D), lambda i: ((i // per_b) * 6 + 2, 0, 0)),
        ],
        out_specs=pl.BlockSpec((tm, D), lambda i: (i, 0)),
        scratch_shapes=[pltpu.VMEM(w_out.shape, BF16), pltpu.VMEM((tm, Kh), BF16)],
        compiler_params=pltpu.CompilerParams(
            dimension_semantics=("arbitrary",), vmem_limit_bytes=56 * MIB),
        name="out_proj",
    )(na_o, da_o, na_out_g, w_out, x2d, mod3)


def _ffn_kernel(x_ref, sh_ref, sc_ref, g2_ref, ng_ref, fg_ref, wg_ref, wu_ref, wd_ref, o_ref, h_ref,
                *, n_f):
    f = pl.program_id(1)
    n_chunks = x_ref.shape[0] // FFN_ROW_CHUNK

    def chunk_rows(r):
        return pl.ds(pl.multiple_of(r * FFN_ROW_CHUNK, FFN_ROW_CHUNK), FFN_ROW_CHUNK)

    def norm_rows(r):
        return pl.ds(pl.multiple_of(r * NORM_ROW_CHUNK, NORM_ROW_CHUNK), NORM_ROW_CHUNK)

    n_norm_chunks = x_ref.shape[0] // NORM_ROW_CHUNK

    @pl.when(f == 0)
    def _():
        gain = ng_ref[...] * (1.0 + sc_ref[0])
        shift = sh_ref[0]

        def body(r, carry):
            rows = norm_rows(r)
            x = x_ref[rows, :]
            ms = jnp.mean(x * x, axis=-1, keepdims=True)
            h_ref[rows, :] = (x * lax.rsqrt(ms + EPS) * gain + shift).astype(BF16)
            o_ref[rows, :] = jnp.zeros((NORM_ROW_CHUNK, o_ref.shape[1]), F32)
            return carry
        lax.fori_loop(0, n_norm_chunks, body, 0)

    def body(r, carry):
        rows = chunk_rows(r)
        h = h_ref[rows, :]
        g = _dot(h, wg_ref[...])
        u = _dot(h, wu_ref[...])
        a = (g * _sigmoid(g) * u).astype(BF16)
        o_ref[rows, :] += _dot(a, wd_ref[...])
        return carry
    lax.fori_loop(0, n_chunks, body, 0, unroll=True)

    @pl.when(f == n_f - 1)
    def _():
        gate = g2_ref[0]
        final_gain = fg_ref[...]

        def body(r, carry):
            rows = norm_rows(r)
            x2 = x_ref[rows, :] + gate * o_ref[rows, :]
            ms = jnp.mean(x2 * x2, axis=-1, keepdims=True)
            o_ref[rows, :] = x2 * lax.rsqrt(ms + EPS) * final_gain
            return carry
        lax.fori_loop(0, n_norm_chunks, body, 0)


def _ffn(x1, mod3, norm_g, final_g, w_gate, w_up, w_down, *, seq):
    M, D = x1.shape
    F = w_gate.shape[1]
    tm, tf = 1024, 512
    per_b = seq // tm
    n_f = F // tf
    kern = lambda *a: _ffn_kernel(*a, n_f=n_f)
    return pl.pallas_call(
        kern,
        out_shape=jax.ShapeDtypeStruct((M, D), F32),
        grid=(M // tm, n_f),
        in_specs=[
            pl.BlockSpec((tm, D), lambda i, f: (i, 0)),
            pl.BlockSpec((1, 1, D), lambda i, f: ((i // per_b) * 6 + 3, 0, 0)),
            pl.BlockSpec((1, 1, D), lambda i, f: ((i // per_b) * 6 + 4, 0, 0)),
            pl.BlockSpec((1, 1, D), lambda i, f: ((i // per_b) * 6 + 5, 0, 0)),
            pl.BlockSpec((1, D), lambda i, f: (0, 0)),
            pl.BlockSpec((1, D), lambda i, f: (0, 0)),
            pl.BlockSpec((D, tf), lambda i, f: (0, f)),
            pl.BlockSpec((D, tf), lambda i, f: (0, f)),
            pl.BlockSpec((tf, D), lambda i, f: (f, 0)),
        ],
        out_specs=pl.BlockSpec((tm, D), lambda i, f: (i, 0)),
        scratch_shapes=[pltpu.VMEM((tm, D), BF16)],
        compiler_params=pltpu.CompilerParams(
            dimension_semantics=("arbitrary", "arbitrary"), vmem_limit_bytes=60 * MIB),
        name="ffn",
    )(x1, mod3, mod3, mod3, norm_g, final_g, w_gate, w_up, w_down)


def kernel(x, c, w_ada, b_ada, norm1_g, w_in, na_rpb, na_out_g, da_lambda, da_subln_g, t5_table,
           w_out, norm2_g, w_gate, w_up, w_down, final_g):
    B, S, D = x.shape
    na_heads = na_rpb.shape[1]
    da_heads = t5_table.shape[1]
    na_width = na_heads * HEAD_DIM
    assert w_ada.shape[0] == 1, "single layer"

    mod = _ada_mod(c, w_ada[0], b_ada[0])
    mod3 = mod.reshape(B * 6, 1, D)
    x2d = x.reshape(B * S, D)

    proj_hm = _in_proj(x2d, mod3, norm1_g, w_in[0], seq=S,
                       na_width=na_width, da_q_off=3 * na_width)
    na_o = _na_attn(proj_hm, _na_table(na_rpb[0]), batch=B, seq=S, heads=na_heads,
                    q_blk0=0, k_blk0=na_heads, v_blk0=2 * na_heads)
    da_o, wg_bf, wu_bf, wd_bf = _da_attn(
        proj_hm, _da_table(t5_table), da_lambda[0], da_subln_g, w_gate[0], w_up[0], w_down[0],
        batch=B, seq=S, heads=da_heads, q_blk0=3 * na_heads, k_blk0=3 * na_heads + da_heads,
        v_blk0=3 * na_heads + 2 * da_heads)
    x1 = _out_proj(na_o, da_o, na_out_g, w_out[0], x2d, mod3, seq=S)
    out = _ffn(x1, mod3, norm2_g, final_g.reshape(1, D), wg_bf, wu_bf, wd_bf, seq=S)
    return out.reshape(B, S, D)
```

```python
import math

import jax
import jax.numpy as jnp
from jax import lax
from jax.experimental import pallas as pl
from jax.experimental.pallas import tpu as pltpu

F32 = jnp.float32
BF16 = jnp.bfloat16

EPS = 1e-6
NEG = -1e30

GRID_W = 64
HEAD_DIM = 128
DA_QK_DIM = 64
NA_WIN_ROWS = 8
NA_WIN_COLS = 16
T5_BUCKETS = 32
T5_MAX_DIST = 128
LAM_INIT = 0.8 - 0.6 * math.exp(-0.3 * 0)
LOG2E = math.log2(math.e)

MIB = 1024 * 1024

NA_QROWS = 8
NA_KROWS = 16
NA_TQ = NA_QROWS * GRID_W
NA_TK = NA_KROWS * GRID_W

DA_TQ = 256
DA_VT_ROWS = HEAD_DIM + 16

FFN_ROW_CHUNK = 256
NORM_ROW_CHUNK = 128
WCAST_ROW_CHUNK = 128


def _dot(a, b):
    return jnp.dot(a, b, preferred_element_type=F32)


def _dot_nt(a, b):
    return lax.dot_general(a, b, (((1,), (1,)), ((), ())), preferred_element_type=F32)


def _sigmoid(x):
    return 1.0 / (1.0 + jnp.exp(-x))


def _ada_kernel(c_ref, w_ref, b_ref, o_ref):
    c = c_ref[...]
    cs = c * _sigmoid(c)
    o_ref[...] = _dot(cs.astype(BF16), w_ref[...].astype(BF16)) + b_ref[...]


def _ada_mod(c, w_ada, b_ada):
    B, D = c.shape
    N = w_ada.shape[1]
    tn = 1024
    rows = 8
    c_pad = jnp.zeros((rows, D), F32).at[:B].set(c)
    out = pl.pallas_call(
        _ada_kernel,
        out_shape=jax.ShapeDtypeStruct((rows, N), F32),
        grid=(N // tn,),
        in_specs=[
            pl.BlockSpec((rows, D), lambda j: (0, 0)),
            pl.BlockSpec((D, tn), lambda j: (0, j)),
            pl.BlockSpec((1, tn), lambda j: (0, j)),
        ],
        out_specs=pl.BlockSpec((rows, tn), lambda j: (0, j)),
        compiler_params=pltpu.CompilerParams(
            dimension_semantics=("arbitrary",), vmem_limit_bytes=40 * MIB),
        name="ada_mod",
    )(c_pad, w_ada, b_ada.reshape(1, N))
    return out[:B]


def _modulated_norm(x, g, sc, sh):
    ms = jnp.mean(x * x, axis=-1, keepdims=True)
    return (x * lax.rsqrt(ms + EPS) * g) * (1.0 + sc) + sh


def _inproj_kernel(x_ref, sh_ref, sc_ref, g_ref, w_ref, o_ref, h_ref, *, na_q_blk, da_q_blk,
                   na_scale, da_scale):
    j = pl.program_id(1)

    @pl.when(j == 0)
    def _():
        gain = g_ref[...] * (1.0 + sc_ref[0])
        shift = sh_ref[0]

        def body(r, carry):
            rows = pl.ds(pl.multiple_of(r * NORM_ROW_CHUNK, NORM_ROW_CHUNK), NORM_ROW_CHUNK)
            x = x_ref[rows, :]
            ms = jnp.mean(x * x, axis=-1, keepdims=True)
            h_ref[rows, :] = (x * lax.rsqrt(ms + EPS) * gain + shift).astype(BF16)
            return carry
        lax.fori_loop(0, x_ref.shape[0] // NORM_ROW_CHUNK, body, 0)

    acc = _dot(h_ref[...], w_ref[...].astype(BF16))
    scale = jnp.where(j == na_q_blk, na_scale, jnp.where(j == da_q_blk, da_scale, 1.0))
    acc = acc * scale
    for k in range(o_ref.shape[0]):
        o_ref[k] = acc[:, k * HEAD_DIM:(k + 1) * HEAD_DIM].astype(BF16)


def _in_proj(x2d, mod3, norm_g, w_in, *, seq, na_width, da_q_off):
    M, D = x2d.shape
    N = w_in.shape[1]
    tm, tn = 1024, 1024
    per_b = seq // tm
    kern = lambda *a: _inproj_kernel(
        *a, na_q_blk=0, da_q_blk=da_q_off // tn,
        na_scale=HEAD_DIM ** -0.5 * LOG2E, da_scale=DA_QK_DIM ** -0.5 * LOG2E)
    assert na_width == tn
    return pl.pallas_call(
        kern,
        out_shape=jax.ShapeDtypeStruct((N // HEAD_DIM, M, HEAD_DIM), BF16),
        grid=(M // tm, N // tn),
        in_specs=[
            pl.BlockSpec((tm, D), lambda i, j: (i, 0)),
            pl.BlockSpec((1, 1, D), lambda i, j: ((i // per_b) * 6 + 0, 0, 0)),
            pl.BlockSpec((1, 1, D), lambda i, j: ((i // per_b) * 6 + 1, 0, 0)),
            pl.BlockSpec((1, D), lambda i, j: (0, 0)),
            pl.BlockSpec((D, tn), lambda i, j: (0, j)),
        ],
        out_specs=pl.BlockSpec((tn // HEAD_DIM, tm, HEAD_DIM), lambda i, j: (j, i, 0)),
        scratch_shapes=[pltpu.VMEM((tm, D), BF16)],
        compiler_params=pltpu.CompilerParams(
            dimension_semantics=("arbitrary", "arbitrary"), vmem_limit_bytes=48 * MIB),
        name="in_proj",
    )(x2d, mod3, mod3, norm_g, w_in)


def _na_table_kernel(rpb_ref, o_ref, *, n_rel_rows, n_rel_cols):
    h = pl.program_id(0)
    shape = (GRID_W, 2 * GRID_W)
    qc = lax.broadcasted_iota(jnp.int32, shape, 0)
    lane = lax.broadcasted_iota(jnp.int32, shape, 1)
    upper = lane >= GRID_W
    kc = jnp.where(upper, lane - GRID_W, lane)
    dc = jnp.clip(kc - qc, -(NA_WIN_COLS - 1), NA_WIN_COLS - 1) + NA_WIN_COLS - 1
    c0 = jnp.clip(qc - NA_WIN_COLS // 2, 0, GRID_W - NA_WIN_COLS)
    in_win = (kc >= c0) & (kc < c0 + NA_WIN_COLS)
    base = h * (n_rel_rows * n_rel_cols)
    for e in range(2 * NA_WIN_ROWS):
        val = jnp.full(shape, NEG, F32)
        for d in range(n_rel_cols):
            r_lo, r_hi = e - 1, e
            lo = rpb_ref[base + r_lo * n_rel_cols + d] * LOG2E if 0 <= r_lo < n_rel_rows else NEG
            hi = rpb_ref[base + r_hi * n_rel_cols + d] * LOG2E if 0 <= r_hi < n_rel_rows else NEG
            val = jnp.where(dc == d, jnp.where(upper, hi, lo), val)
        o_ref[0, e] = jnp.where(in_win, val, NEG)


def _na_table(rpb):
    H, nr, nc = rpb.shape
    kern = lambda *a: _na_table_kernel(*a, n_rel_rows=nr, n_rel_cols=nc)
    return pl.pallas_call(
        kern,
        out_shape=jax.ShapeDtypeStruct((H, 2 * NA_WIN_ROWS, GRID_W, 2 * GRID_W), F32),
        grid=(H,),
        in_specs=[pl.BlockSpec(memory_space=pltpu.SMEM)],
        out_specs=pl.BlockSpec((1, 2 * NA_WIN_ROWS, GRID_W, 2 * GRID_W), lambda h: (h, 0, 0, 0)),
        compiler_params=pltpu.CompilerParams(dimension_semantics=("arbitrary",)),
        name="na_table",
    )(rpb.reshape(-1))


def _na_window_start_row(blk, rows):
    return min(max(blk * NA_QROWS - (NA_KROWS - NA_QROWS) // 2, 0), rows - NA_KROWS)


def _na_block(q, kw, vw, tp_ref, blk, rows):
    lane = lax.broadcasted_iota(jnp.int32, (GRID_W, 2 * GRID_W), 1)
    kr0 = _na_window_start_row(blk, rows)
    s = _dot_nt(q, kw)
    p_rows, inv_l = [], []
    for i in range(NA_QROWS):
        qr = blk * NA_QROWS + i
        rs = min(max(qr - NA_WIN_ROWS // 2, 0), rows - NA_WIN_ROWS)
        tiles = {}
        for m in range(NA_KROWS // 2):
            ka = kr0 + 2 * m
            va = rs <= ka < rs + NA_WIN_ROWS
            vb = rs <= ka + 1 < rs + NA_WIN_ROWS
            if not (va or vb):
                continue
            e = ka - qr + NA_WIN_ROWS
            t = s[i * GRID_W:(i + 1) * GRID_W, m * 2 * GRID_W:(m + 1) * 2 * GRID_W] + tp_ref[0, e]
            if not vb:
                t = jnp.where(lane < GRID_W, t, NEG)
            elif not va:
                t = jnp.where(lane >= GRID_W, t, NEG)
            tiles[m] = t
        ts = list(tiles.values())
        mx = ts[0]
        for t in ts[1:]:
            mx = jnp.maximum(mx, t)
        mx = jnp.max(mx, axis=-1, keepdims=True)
        ps = {m: jnp.exp2(t - mx) for m, t in tiles.items()}
        tot = None
        for pt in ps.values():
            tot = pt if tot is None else tot + pt
        inv_l.append(1.0 / jnp.sum(tot, axis=-1, keepdims=True))
        zero = jnp.zeros((GRID_W, 2 * GRID_W), BF16)
        p_rows.append(jnp.concatenate(
            [ps[m].astype(BF16) if m in ps else zero for m in range(NA_KROWS // 2)], axis=1))
    p = jnp.concatenate(p_rows, axis=0)
    return _dot(p, vw) * jnp.concatenate(inv_l, axis=0)


def _na_kernel(q_ref, k_ref, v_ref, tp_ref, o_ref, *, rows):
    for blk in range(rows // NA_QROWS):
        start = _na_window_start_row(blk, rows) * GRID_W
        o = _na_block(q_ref[blk * NA_TQ:(blk + 1) * NA_TQ, :], k_ref[start:start + NA_TK, :],
                      v_ref[start:start + NA_TK, :], tp_ref, blk, rows)
        o_ref[blk * NA_TQ:(blk + 1) * NA_TQ, :] = o.astype(BF16)


def _na_attn(proj_hm, tp, *, batch, seq, heads, q_blk0, k_blk0, v_blk0):
    rows = seq // GRID_W
    kern = lambda *a: _na_kernel(*a, rows=rows)
    head_blk = lambda blk0: pl.BlockSpec((None, seq, HEAD_DIM), lambda b, h: (blk0 + h, b, 0))
    return pl.pallas_call(
        kern,
        out_shape=jax.ShapeDtypeStruct((batch * seq, heads * HEAD_DIM), BF16),
        grid=(batch, heads),
        in_specs=[
            head_blk(q_blk0), head_blk(k_blk0), head_blk(v_blk0),
            pl.BlockSpec((1, 2 * NA_WIN_ROWS, GRID_W, 2 * GRID_W), lambda b, h: (h, 0, 0, 0)),
        ],
        out_specs=pl.BlockSpec((seq, HEAD_DIM), lambda b, h: (b, h)),
        compiler_params=pltpu.CompilerParams(
            dimension_semantics=("arbitrary", "arbitrary"), vmem_limit_bytes=48 * MIB),
        name="na_attn",
    )(proj_hm, proj_hm, proj_hm, tp)


def _t5_bucket(rel):
    nb = T5_BUCKETS // 2
    ret = jnp.where(rel > 0, nb, 0)
    n = jnp.abs(rel)
    max_exact = nb // 2
    nf = jnp.maximum(n, 1).astype(jnp.float32)
    large = max_exact + (jnp.log(nf / max_exact) / math.log(T5_MAX_DIST / max_exact)
                         * (nb - max_exact)).astype(jnp.int32)
    large = jnp.minimum(large, nb - 1)
    return ret + jnp.where(n < max_exact, n, large)


def _da_table_kernel(bucket_ref, t5_ref, o_ref, *, heads):
    h = pl.program_id(0)
    shape = (DA_TQ, DA_TQ)
    nb = T5_BUCKETS // 2
    o_ref[0, 0] = jnp.full(shape, t5_ref[(nb - 1) * heads + h] * LOG2E, F32)
    o_ref[0, 4] = jnp.full(shape, t5_ref[(T5_BUCKETS - 1) * heads + h] * LOG2E, F32)
    for s in range(3):
        bk = bucket_ref[s]
        val = jnp.zeros(shape, F32)
        for bkt in range(T5_BUCKETS):
            val = jnp.where(bk == bkt, t5_ref[bkt * heads + h] * LOG2E, val)
        o_ref[0, s + 1] = val


def _da_table(t5_table):
    heads = t5_table.shape[1]
    assert DA_TQ >= T5_MAX_DIST
    kpos = jnp.arange(DA_TQ)[:, None]
    qpos = jnp.arange(DA_TQ)[None, :]
    bucket = jnp.stack(
        [_t5_bucket((dj * DA_TQ + kpos) - qpos) for dj in (-1, 0, 1)]).astype(jnp.int32)
    kern = lambda *a: _da_table_kernel(*a, heads=heads)
    return pl.pallas_call(
        kern,
        out_shape=jax.ShapeDtypeStruct((heads, 5, DA_TQ, DA_TQ), F32),
        grid=(heads,),
        in_specs=[
            pl.BlockSpec((3, DA_TQ, DA_TQ), lambda h: (0, 0, 0)),
            pl.BlockSpec(memory_space=pltpu.SMEM),
        ],
        out_specs=pl.BlockSpec((1, 5, DA_TQ, DA_TQ), lambda h: (h, 0, 0, 0)),
        compiler_params=pltpu.CompilerParams(dimension_semantics=("arbitrary",)),
        name="da_table",
    )(bucket, t5_table.reshape(-1))


def _da_kernel(q_ref, k_ref, v_ref, tb_ref, lam_ref, g_ref, wg_ref, wu_ref, wd_ref,
               o_ref, wg_bf_ref, wu_bf_ref, wd_bf_ref, vt_ref, s_ref, p_ref, *, seq):
    n_kt = seq // DA_TQ

    wg_bf_ref[...] = wg_ref[...].astype(BF16)
    wu_bf_ref[...] = wu_ref[...].astype(BF16)
    wd_bf_ref[...] = wd_ref[...].astype(BF16)

    vt_ref[:HEAD_DIM, :] = v_ref[...].astype(F32).T.astype(BF16)
    row = lax.broadcasted_iota(jnp.int32, (DA_VT_ROWS - HEAD_DIM, seq), 0)
    vt_ref[HEAD_DIM:, :] = jnp.where(row == 0, 1.0, 0.0).astype(BF16)

    lp = lam_ref[...]
    t1 = jnp.sum(lp[0:1] * lp[1:2], axis=-1, keepdims=True)
    t2 = jnp.sum(lp[2:3] * lp[3:4], axis=-1, keepdims=True)
    lam = jnp.exp(t1) - jnp.exp(t2) + LAM_INIT

    c_left = tb_ref[0, 0, 0:1, 0:1]
    c_right = tb_ref[0, 4, 0:1, 0:1]
    lane = lax.broadcasted_iota(jnp.int32, (DA_TQ, HEAD_DIM), 1)

    def key_rows(j):
        return slice(j * DA_TQ, (j + 1) * DA_TQ)

    def col_max(parts):
        acc = parts[0]
        for part in parts[1:]:
            acc = jnp.maximum(acc, part)
        return jnp.max(acc, axis=0, keepdims=True)

    def qk_scores(t):
        q = q_ref[t * DA_TQ:(t + 1) * DA_TQ, :]
        zero = jnp.zeros_like(q)
        q12 = jnp.concatenate(
            [jnp.where(lane < DA_QK_DIM, q, zero), jnp.where(lane >= DA_QK_DIM, q, zero)], axis=0)
        s_ref[t % 2] = _dot_nt(k_ref[...], q12)

    qk_scores(0)
    for t in range(n_kt):
        buf = t % 2
        if t + 1 < n_kt:
            qk_scores(t + 1)

        def scores(j):
            s = s_ref[buf, key_rows(j), :]
            if abs(j - t) <= 1:
                bias = tb_ref[0, j - t + 2]
                s = s + jnp.concatenate([bias, bias], axis=1)
            return s

        near = [j for j in range(n_kt) if abs(j - t) <= 1]
        left = [j for j in range(n_kt) if j < t - 1]
        right = [j for j in range(n_kt) if j > t + 1]
        m = col_max([scores(j) for j in near])
        if left:
            m = jnp.maximum(m, col_max([scores(j) for j in left]) + c_left)
        if right:
            m = jnp.maximum(m, col_max([scores(j) for j in right]) + c_right)

        for j in range(n_kt):
            shift = m if abs(j - t) <= 1 else (m - c_left if j < t else m - c_right)
            p_ref[buf, key_rows(j), :] = jnp.exp2(scores(j) - shift).astype(BF16)
        oe1 = _dot(vt_ref[...], p_ref[buf, :, :DA_TQ])
        oe2 = _dot(vt_ref[...], p_ref[buf, :, DA_TQ:])
        o1, l1 = oe1[:HEAD_DIM], oe1[HEAD_DIM:HEAD_DIM + 1]
        o2, l2 = oe2[:HEAD_DIM], oe2[HEAD_DIM:HEAD_DIM + 1]
        o = o1 * (1.0 / l1) - o2 * (lam / l2)
        ms = jnp.mean(o * o, axis=0, keepdims=True)
        y = (o * lax.rsqrt(ms + EPS) * g_ref[...]) * (1.0 - LAM_INIT)
        o_ref[t * DA_TQ:(t + 1) * DA_TQ, :] = y.T.astype(BF16)


def _da_attn(proj_hm, tb, da_lambda, subln_g, w_gate, w_up, w_down, *, batch, seq, heads,
             q_blk0, k_blk0, v_blk0):
    kern = lambda *a: _da_kernel(*a, seq=seq)
    head_blk = lambda blk0: pl.BlockSpec((None, seq, HEAD_DIM), lambda b, h: (blk0 + h, b, 0))
    n_steps = batch * heads

    def row_slab(w):
        assert w.shape[0] % (16 * n_steps) == 0
        return pl.BlockSpec((w.shape[0] // n_steps, w.shape[1]), lambda b, h: (b * heads + h, 0))

    weights = (w_gate, w_up, w_down)
    return pl.pallas_call(
        kern,
        out_shape=(jax.ShapeDtypeStruct((batch * seq, heads * HEAD_DIM), BF16),
                   *[jax.ShapeDtypeStruct(w.shape, BF16) for w in weights]),
        grid=(batch, heads),
        in_specs=[
            head_blk(q_blk0), head_blk(k_blk0), head_blk(v_blk0),
            pl.BlockSpec((1, 5, DA_TQ, DA_TQ), lambda b, h: (h, 0, 0, 0)),
            pl.BlockSpec(da_lambda.shape, lambda b, h: (0, 0)),
            pl.BlockSpec((HEAD_DIM, 1), lambda b, h: (0, 0)),
            *[row_slab(w) for w in weights],
        ],
        out_specs=(pl.BlockSpec((seq, HEAD_DIM), lambda b, h: (b, h)),
                   *[row_slab(w) for w in weights]),
        scratch_shapes=[
            pltpu.VMEM((DA_VT_ROWS, seq), BF16),
            pltpu.VMEM((2, seq, 2 * DA_TQ), F32),
            pltpu.VMEM((2, seq, 2 * DA_TQ), BF16),
        ],
        compiler_params=pltpu.CompilerParams(
            dimension_semantics=("arbitrary", "arbitrary"), vmem_limit_bytes=48 * MIB),
        name="da_attn",
    )(proj_hm, proj_hm, proj_hm, tb, da_lambda, subln_g.reshape(HEAD_DIM, 1), *weights)


def _outproj_kernel(na_ref, da_ref, nag_ref, w_ref, x_ref, g1_ref, o_ref, wbf_ref, nan_ref):
    kh = na_ref.shape[1]

    @pl.when(pl.program_id(0) == 0)
    def _():
        def body(r, carry):
            rows = pl.ds(pl.multiple_of(r * WCAST_ROW_CHUNK, WCAST_ROW_CHUNK), WCAST_ROW_CHUNK)
            wbf_ref[rows, :] = w_ref[rows, :].astype(BF16)
            return carry
        lax.fori_loop(0, w_ref.shape[0] // WCAST_ROW_CHUNK, body, 0)

    gain = nag_ref[...]

    def body(r, carry):
        rows = pl.ds(pl.multiple_of(r * NORM_ROW_CHUNK, NORM_ROW_CHUNK), NORM_ROW_CHUNK)
        o = na_ref[rows, :].astype(F32)
        ms = jnp.mean(o * o, axis=-1, keepdims=True)
        nan_ref[rows, :] = (o * lax.rsqrt(ms + EPS) * gain).astype(BF16)
        return carry
    lax.fori_loop(0, na_ref.shape[0] // NORM_ROW_CHUNK, body, 0)

    mix = _dot(nan_ref[...], wbf_ref[:kh, :]) + _dot(da_ref[...], wbf_ref[kh:, :])
    o_ref[...] = x_ref[...] + g1_ref[0] * mix


def _out_proj(na_o, da_o, na_out_g, w_out, x2d, mod3, *, seq):
    M, D = x2d.shape
    Kh = na_o.shape[1]
    tm = 512
    per_b = seq // tm
    return pl.pallas_call(
        _outproj_kernel,
        out_shape=jax.ShapeDtypeStruct((M, D), F32),
        grid=(M // tm,),
        in_specs=[
            pl.BlockSpec((tm, Kh), lambda i: (i, 0)),
            pl.BlockSpec((tm, Kh), lambda i: (i, 0)),
            pl.BlockSpec((1, Kh), lambda i: (0, 0)),
            pl.BlockSpec(w_out.shape, lambda i: (0, 0), pipeline_mode=pl.Buffered(1)),
            pl.BlockSpec((tm, D), lambda i: (i, 0)),
            pl.BlockSpec((1, 1, D), lambda i: ((i // per_b) * 6 + 2, 0, 0)),
        ],
        out_specs=pl.BlockSpec((tm, D), lambda i: (i, 0)),
        scratch_shapes=[pltpu.VMEM(w_out.shape, BF16), pltpu.VMEM((tm, Kh), BF16)],
        compiler_params=pltpu.CompilerParams(
            dimension_semantics=("arbitrary",), vmem_limit_bytes=56 * MIB),
        name="out_proj",
    )(na_o, da_o, na_out_g, w_out, x2d, mod3)


def _ffn_kernel(x_ref, sh_ref, sc_ref, g2_ref, ng_ref, fg_ref, wg_ref, wu_ref, wd_ref, o_ref, h_ref,
                *, n_f):
    f = pl.program_id(1)
    n_chunks = x_ref.shape[0] // FFN_ROW_CHUNK

    def chunk_rows(r):
        return pl.ds(pl.multiple_of(r * FFN_ROW_CHUNK, FFN_ROW_CHUNK), FFN_ROW_CHUNK)

    def norm_rows(r):
        return pl.ds(pl.multiple_of(r * NORM_ROW_CHUNK, NORM_ROW_CHUNK), NORM_ROW_CHUNK)

    n_norm_chunks = x_ref.shape[0] // NORM_ROW_CHUNK

    @pl.when(f == 0)
    def _():
        gain = ng_ref[...] * (1.0 + sc_ref[0])
        shift = sh_ref[0]

        def body(r, carry):
            rows = norm_rows(r)
            x = x_ref[rows, :]
            ms = jnp.mean(x * x, axis=-1, keepdims=True)
            h_ref[rows, :] = (x * lax.rsqrt(ms + EPS) * gain + shift).astype(BF16)
            o_ref[rows, :] = jnp.zeros((NORM_ROW_CHUNK, o_ref.shape[1]), F32)
            return carry
        lax.fori_loop(0, n_norm_chunks, body, 0)

    def body(r, carry):
        rows = chunk_rows(r)
        h = h_ref[rows, :]
        g = _dot(h, wg_ref[...])
        u = _dot(h, wu_ref[...])
        a = (g * _sigmoid(g) * u).astype(BF16)
        o_ref[rows, :] += _dot(a, wd_ref[...])
        return carry
    lax.fori_loop(0, n_chunks, body, 0, unroll=True)

    @pl.when(f == n_f - 1)
    def _():
        gate = g2_ref[0]
        final_gain = fg_ref[...]

        def body(r, carry):
            rows = norm_rows(r)
            x2 = x_ref[rows, :] + gate * o_ref[rows, :]
            ms = jnp.mean(x2 * x2, axis=-1, keepdims=True)
            o_ref[rows, :] = x2 * lax.rsqrt(ms + EPS) * final_gain
            return carry
        lax.fori_loop(0, n_norm_chunks, body, 0)


def _ffn(x1, mod3, norm_g, final_g, w_gate, w_up, w_down, *, seq):
    M, D = x1.shape
    F = w_gate.shape[1]
    tm, tf = 1024, 512
    per_b = seq // tm
    n_f = F // tf
    kern = lambda *a: _ffn_kernel(*a, n_f=n_f)
    return pl.pallas_call(
        kern,
        out_shape=jax.ShapeDtypeStruct((M, D), F32),
        grid=(M // tm, n_f),
        in_specs=[
            pl.BlockSpec((tm, D), lambda i, f: (i, 0)),
            pl.BlockSpec((1, 1, D), lambda i, f: ((i // per_b) * 6 + 3, 0, 0)),
            pl.BlockSpec((1, 1, D), lambda i, f: ((i // per_b) * 6 + 4, 0, 0)),
            pl.BlockSpec((1, 1, D), lambda i, f: ((i // per_b) * 6 + 5, 0, 0)),
            pl.BlockSpec((1, D), lambda i, f: (0, 0)),
            pl.BlockSpec((1, D), lambda i, f: (0, 0)),
            pl.BlockSpec((D, tf), lambda i, f: (0, f)),
            pl.BlockSpec((D, tf), lambda i, f: (0, f)),
            pl.BlockSpec((tf, D), lambda i, f: (f, 0)),
        ],
        out_specs=pl.BlockSpec((tm, D), lambda i, f: (i, 0)),
        scratch_shapes=[pltpu.VMEM((tm, D), BF16)],
        compiler_params=pltpu.CompilerParams(
            dimension_semantics=("arbitrary", "arbitrary"), vmem_limit_bytes=60 * MIB),
        name="ffn",
    )(x1, mod3, mod3, mod3, norm_g, final_g, w_gate, w_up, w_down)


def kernel(x, c, w_ada, b_ada, norm1_g, w_in, na_rpb, na_out_g, da_lambda, da_subln_g, t5_table,
           w_out, norm2_g, w_gate, w_up, w_down, final_g):
    B, S, D = x.shape
    na_heads = na_rpb.shape[1]
    da_heads = t5_table.shape[1]
    na_width = na_heads * HEAD_DIM
    assert w_ada.shape[0] == 1, "single layer"

    mod = _ada_mod(c, w_ada[0], b_ada[0])
    mod3 = mod.reshape(B * 6, 1, D)
    x2d = x.reshape(B * S, D)

    proj_hm = _in_proj(x2d, mod3, norm1_g, w_in[0], seq=S,
                       na_width=na_width, da_q_off=3 * na_width)
    na_o = _na_attn(proj_hm, _na_table(na_rpb[0]), batch=B, seq=S, heads=na_heads,
                    q_blk0=0, k_blk0=na_heads, v_blk0=2 * na_heads)
    da_o, wg_bf, wu_bf, wd_bf = _da_attn(
        proj_hm, _da_table(t5_table), da_lambda[0], da_subln_g, w_gate[0], w_up[0], w_down[0],
        batch=B, seq=S, heads=da_heads, q_blk0=3 * na_heads, k_blk0=3 * na_heads + da_heads,
        v_blk0=3 * na_heads + 2 * da_heads)
    x1 = _out_proj(na_o, da_o, na_out_g, w_out[0], x2d, mod3, seq=S)
    out = _ffn(x1, mod3, norm2_g, final_g.reshape(1, D), wg_bf, wu_bf, wd_bf, seq=S)
    return out.reshape(B, S, D)
```

```python
import math

import jax
import jax.numpy as jnp
from jax import lax
from jax.experimental import pallas as pl
from jax.experimental.pallas import tpu as pltpu

F32 = jnp.float32
BF16 = jnp.bfloat16

EPS = 1e-6
NEG = -1e30

GRID_W = 64
HEAD_DIM = 128
DA_QK_DIM = 64
NA_WIN_ROWS = 8
NA_WIN_COLS = 16
T5_BUCKETS = 32
T5_MAX_DIST = 128
LAM_INIT = 0.8 - 0.6 * math.exp(-0.3 * 0)
LOG2E = math.log2(math.e)

MIB = 1024 * 1024

NA_QROWS = 8
NA_KROWS = 16
NA_TQ = NA_QROWS * GRID_W
NA_TK = NA_KROWS * GRID_W

DA_TQ = 256
DA_VT_ROWS = HEAD_DIM + 16

FFN_ROW_CHUNK = 256
NORM_ROW_CHUNK = 128
WCAST_ROW_CHUNK = 128


def _dot(a, b):
    return jnp.dot(a, b, preferred_element_type=F32)


def _dot_nt(a, b):
    return lax.dot_general(a, b, (((1,), (1,)), ((), ())), preferred_element_type=F32)


def _sigmoid(x):
    return 1.0 / (1.0 + jnp.exp(-x))


def _ada_columns(c_ref, w_ref, b_ref):
    c = c_ref[...]
    cs = c * _sigmoid(c)
    return _dot(cs.astype(BF16), w_ref[...].astype(BF16)) + b_ref[...]


def _ada_kernel(c_ref, w_ref, b_ref, o_ref):
    o_ref[...] = _ada_columns(c_ref, w_ref, b_ref)


def _ada_mod(c_pad, w_ada, b_ada, n_cols):
    rows, D = c_pad.shape
    tn = 1024
    return pl.pallas_call(
        _ada_kernel,
        out_shape=jax.ShapeDtypeStruct((rows, n_cols), F32),
        grid=(n_cols // tn,),
        in_specs=[
            pl.BlockSpec((rows, D), lambda j: (0, 0)),
            pl.BlockSpec((D, tn), lambda j: (0, j)),
            pl.BlockSpec((1, tn), lambda j: (0, j)),
        ],
        out_specs=pl.BlockSpec((rows, tn), lambda j: (0, j)),
        compiler_params=pltpu.CompilerParams(
            dimension_semantics=("arbitrary",), vmem_limit_bytes=40 * MIB),
        name="ada_mod",
    )(c_pad, w_ada, b_ada)


def _inproj_kernel(x_ref, sh_ref, sc_ref, g_ref, w_ref, o_ref, h_ref, *, na_q_blk, da_q_blk,
                   na_scale, da_scale):
    j = pl.program_id(1)

    @pl.when(j == 0)
    def _():
        gain = g_ref[...] * (1.0 + sc_ref[0])
        shift = sh_ref[0]

        def body(r, carry):
            rows = pl.ds(pl.multiple_of(r * NORM_ROW_CHUNK, NORM_ROW_CHUNK), NORM_ROW_CHUNK)
            x = x_ref[rows, :]
            ms = jnp.mean(x * x, axis=-1, keepdims=True)
            h_ref[rows, :] = (x * lax.rsqrt(ms + EPS) * gain + shift).astype(BF16)
            return carry
        lax.fori_loop(0, x_ref.shape[0] // NORM_ROW_CHUNK, body, 0)

    acc = _dot(h_ref[...], w_ref[...].astype(BF16))
    scale = jnp.where(j == na_q_blk, na_scale, jnp.where(j == da_q_blk, da_scale, 1.0))
    acc = acc * scale
    for k in range(o_ref.shape[0]):
        o_ref[k] = acc[:, k * HEAD_DIM:(k + 1) * HEAD_DIM].astype(BF16)


def _in_proj(x2d, mod3, norm_g, w_in, *, seq, na_width, da_q_off):
    M, D = x2d.shape
    N = w_in.shape[1]
    tm, tn = 1024, 1024
    per_b = seq // tm
    kern = lambda *a: _inproj_kernel(
        *a, na_q_blk=0, da_q_blk=da_q_off // tn,
        na_scale=HEAD_DIM ** -0.5 * LOG2E, da_scale=DA_QK_DIM ** -0.5 * LOG2E)
    assert na_width == tn
    return pl.pallas_call(
        kern,
        out_shape=jax.ShapeDtypeStruct((N // HEAD_DIM, M, HEAD_DIM), BF16),
        grid=(M // tm, N // tn),
        in_specs=[
            pl.BlockSpec((tm, D), lambda i, j: (i, 0)),
            pl.BlockSpec((1, 1, D), lambda i, j: ((i // per_b) * 2 + 0, 0, 0)),
            pl.BlockSpec((1, 1, D), lambda i, j: ((i // per_b) * 2 + 1, 0, 0)),
            pl.BlockSpec((1, D), lambda i, j: (0, 0)),
            pl.BlockSpec((D, tn), lambda i, j: (0, j)),
        ],
        out_specs=pl.BlockSpec((tn // HEAD_DIM, tm, HEAD_DIM), lambda i, j: (j, i, 0)),
        scratch_shapes=[pltpu.VMEM((tm, D), BF16)],
        compiler_params=pltpu.CompilerParams(
            dimension_semantics=("arbitrary", "arbitrary"), vmem_limit_bytes=48 * MIB),
        name="in_proj",
    )(x2d, mod3, mod3, norm_g, w_in)


NA_REL_ROWS = 2 * NA_WIN_ROWS - 1
NA_REL_COLS = 2 * NA_WIN_COLS - 1
NA_PAIR_TILES = 2 * NA_WIN_ROWS


def _na_table_kernel(rpb_ref, o_ref):
    h = pl.program_id(0)
    shape = (GRID_W, 2 * GRID_W)
    qc = lax.broadcasted_iota(jnp.int32, shape, 0)
    lane = lax.broadcasted_iota(jnp.int32, shape, 1)
    upper = lane >= GRID_W
    kc = jnp.where(upper, lane - GRID_W, lane)
    c0 = jnp.clip(qc - NA_WIN_COLS // 2, 0, GRID_W - NA_WIN_COLS)
    in_win = (kc >= c0) & (kc < c0 + NA_WIN_COLS)
    lane_v = lax.broadcasted_iota(jnp.int32, (8, 2 * GRID_W), 1)
    base = h * (NA_REL_ROWS * NA_REL_COLS)

    def rotated_rows(rel_row, center):
        if not 0 <= rel_row < NA_REL_ROWS:
            return jnp.full(shape, NEG, F32)
        vec = jnp.zeros(lane_v.shape, F32)
        for d in range(NA_REL_COLS):
            at = (center + d - (NA_WIN_COLS - 1)) % (2 * GRID_W)
            vec = jnp.where(lane_v == at, rpb_ref[base + rel_row * NA_REL_COLS + d] * LOG2E, vec)
        rows = jnp.concatenate([vec] * (GRID_W // 8), axis=0)
        return pltpu.roll(rows, 0, 1, stride=1, stride_axis=0)

    for e in range(NA_PAIR_TILES):
        tile = jnp.where(upper, rotated_rows(e, GRID_W), rotated_rows(e - 1, 0))
        o_ref[e] = jnp.where(in_win, tile, NEG)


def _na_table(rpb):
    heads = rpb.shape[0]
    assert rpb.shape[1:] == (NA_REL_ROWS, NA_REL_COLS)
    return pl.pallas_call(
        _na_table_kernel,
        out_shape=jax.ShapeDtypeStruct((heads * NA_PAIR_TILES, GRID_W, 2 * GRID_W), F32),
        grid=(heads,),
        in_specs=[pl.BlockSpec(memory_space=pltpu.SMEM)],
        out_specs=pl.BlockSpec((NA_PAIR_TILES, GRID_W, 2 * GRID_W), lambda h: (h, 0, 0)),
        compiler_params=pltpu.CompilerParams(dimension_semantics=("arbitrary",)),
        name="na_table",
    )(rpb.reshape(-1))


def _na_window_start_row(blk, rows):
    return min(max(blk * NA_QROWS - (NA_KROWS - NA_QROWS) // 2, 0), rows - NA_KROWS)


def _na_block(q, kw, vw, tp_ref, blk, rows):
    lane = lax.broadcasted_iota(jnp.int32, (GRID_W, 2 * GRID_W), 1)
    kr0 = _na_window_start_row(blk, rows)
    s = _dot_nt(q, kw)
    p_rows, inv_l = [], []
    for i in range(NA_QROWS):
        qr = blk * NA_QROWS + i
        rs = min(max(qr - NA_WIN_ROWS // 2, 0), rows - NA_WIN_ROWS)
        tiles = {}
        for m in range(NA_KROWS // 2):
            ka = kr0 + 2 * m
            va = rs <= ka < rs + NA_WIN_ROWS
            vb = rs <= ka + 1 < rs + NA_WIN_ROWS
            if not (va or vb):
                continue
            e = ka - qr + NA_WIN_ROWS
            t = s[i * GRID_W:(i + 1) * GRID_W, m * 2 * GRID_W:(m + 1) * 2 * GRID_W] + tp_ref[e]
            if not vb:
                t = jnp.where(lane < GRID_W, t, NEG)
            elif not va:
                t = jnp.where(lane >= GRID_W, t, NEG)
            tiles[m] = t
        ts = list(tiles.values())
        mx = ts[0]
        for t in ts[1:]:
            mx = jnp.maximum(mx, t)
        mx = jnp.max(mx, axis=-1, keepdims=True)
        ps = {m: jnp.exp2(t - mx) for m, t in tiles.items()}
        tot = None
        for pt in ps.values():
            tot = pt if tot is None else tot + pt
        inv_l.append(1.0 / jnp.sum(tot, axis=-1, keepdims=True))
        zero = jnp.zeros((GRID_W, 2 * GRID_W), BF16)
        p_rows.append(jnp.concatenate(
            [ps[m].astype(BF16) if m in ps else zero for m in range(NA_KROWS // 2)], axis=1))
    p = jnp.concatenate(p_rows, axis=0)
    return _dot(p, vw) * jnp.concatenate(inv_l, axis=0)


def _na_kernel(q_ref, k_ref, v_ref, tp_ref, o_ref, *, rows):
    for blk in range(rows // NA_QROWS):
        start = _na_window_start_row(blk, rows) * GRID_W
        o = _na_block(q_ref[blk * NA_TQ:(blk + 1) * NA_TQ, :], k_ref[start:start + NA_TK, :],
                      v_ref[start:start + NA_TK, :], tp_ref, blk, rows)
        o_ref[blk * NA_TQ:(blk + 1) * NA_TQ, :] = o.astype(BF16)


def _na_attn(proj_hm, tp, *, batch, seq, heads, q_blk0, k_blk0, v_blk0):
    rows = seq // GRID_W
    kern = lambda *a: _na_kernel(*a, rows=rows)
    head_blk = lambda blk0: pl.BlockSpec((None, seq, HEAD_DIM), lambda b, h: (blk0 + h, b, 0))
    return pl.pallas_call(
        kern,
        out_shape=jax.ShapeDtypeStruct((batch * seq, heads * HEAD_DIM), BF16),
        grid=(batch, heads),
        in_specs=[
            head_blk(q_blk0), head_blk(k_blk0), head_blk(v_blk0),
            pl.BlockSpec((NA_PAIR_TILES, GRID_W, 2 * GRID_W), lambda b, h: (h, 0, 0)),
        ],
        out_specs=pl.BlockSpec((seq, HEAD_DIM), lambda b, h: (b, h)),
        compiler_params=pltpu.CompilerParams(
            dimension_semantics=("arbitrary", "arbitrary"), vmem_limit_bytes=48 * MIB),
        name="na_attn",
    )(proj_hm, proj_hm, proj_hm, tp)


def _t5_bucket(rel):
    nb = T5_BUCKETS // 2
    ret = jnp.where(rel > 0, nb, 0)
    n = jnp.abs(rel)
    max_exact = nb // 2
    nf = jnp.maximum(n, 1).astype(jnp.float32)
    large = max_exact + (jnp.log(nf / max_exact) / math.log(T5_MAX_DIST / max_exact)
                         * (nb - max_exact)).astype(jnp.int32)
    large = jnp.minimum(large, nb - 1)
    return ret + jnp.where(n < max_exact, n, large)


DA_SLABS = 5


DA_REL_SPAN = 4 * DA_TQ


def _da_table_kernel(bucket_ref, t5_ref, o_ref, *, heads):
    h = pl.program_id(0)
    nb = T5_BUCKETS // 2
    bk = bucket_ref[...]
    vec = jnp.zeros(bk.shape, F32)
    for bkt in range(T5_BUCKETS):
        vec = jnp.where(bk == bkt, t5_ref[bkt * heads + h] * LOG2E, vec)
    rows = jnp.concatenate([vec] * (DA_TQ // 8), axis=0)
    rolled = pltpu.roll(rows, 1, 1, stride=1, stride_axis=0)
    for dj in (-1, 0, 1):
        lanes = 2 * DA_TQ - dj * DA_TQ
        o_ref[dj + 2] = rolled[:, lanes:lanes + DA_TQ]
    shape = (DA_TQ, DA_TQ)
    o_ref[0] = jnp.full(shape, t5_ref[(nb - 1) * heads + h] * LOG2E, F32)
    o_ref[DA_SLABS - 1] = jnp.full(shape, t5_ref[(T5_BUCKETS - 1) * heads + h] * LOG2E, F32)


def _da_table(t5_table):
    heads = t5_table.shape[1]
    assert DA_TQ >= T5_MAX_DIST
    rel = 2 * DA_TQ - 1 - jnp.arange(DA_REL_SPAN)
    bucket = jnp.broadcast_to(_t5_bucket(rel).astype(jnp.int32)[None, :], (8, DA_REL_SPAN))
    kern = lambda *a: _da_table_kernel(*a, heads=heads)
    return pl.pallas_call(
        kern,
        out_shape=jax.ShapeDtypeStruct((heads * DA_SLABS, DA_TQ, DA_TQ), F32),
        grid=(heads,),
        in_specs=[
            pl.BlockSpec((8, DA_REL_SPAN), lambda h: (0, 0)),
            pl.BlockSpec(memory_space=pltpu.SMEM),
        ],
        out_specs=pl.BlockSpec((DA_SLABS, DA_TQ, DA_TQ), lambda h: (h, 0, 0)),
        compiler_params=pltpu.CompilerParams(dimension_semantics=("arbitrary",)),
        name="da_table",
    )(bucket, t5_table.reshape(-1))


def _da_kernel(q_ref, k_ref, v_ref, tb_ref, lam_ref, g_ref, wg_ref, wu_ref, wd_ref,
               c_ref, wada_ref, bada_ref,
               o_ref, wg_bf_ref, wu_bf_ref, wd_bf_ref, mod_ref, vt_ref, s_ref, p_ref, *, seq):
    n_kt = seq // DA_TQ

    wg_bf_ref[...] = wg_ref[...].astype(BF16)
    wu_bf_ref[...] = wu_ref[...].astype(BF16)
    wd_bf_ref[...] = wd_ref[...].astype(BF16)
    mod_ref[...] = _ada_columns(c_ref, wada_ref, bada_ref)

    vt_ref[:HEAD_DIM, :] = v_ref[...].astype(F32).T.astype(BF16)
    row = lax.broadcasted_iota(jnp.int32, (DA_VT_ROWS - HEAD_DIM, seq), 0)
    vt_ref[HEAD_DIM:, :] = jnp.where(row == 0, 1.0, 0.0).astype(BF16)

    lp = lam_ref[...]
    t1 = jnp.sum(lp[0:1] * lp[1:2], axis=-1, keepdims=True)
    t2 = jnp.sum(lp[2:3] * lp[3:4], axis=-1, keepdims=True)
    lam = jnp.exp(t1) - jnp.exp(t2) + LAM_INIT

    c_left = tb_ref[0, 0:1, 0:1]
    c_right = tb_ref[DA_SLABS - 1, 0:1, 0:1]
    lane = lax.broadcasted_iota(jnp.int32, (DA_TQ, HEAD_DIM), 1)

    def key_rows(j):
        return slice(j * DA_TQ, (j + 1) * DA_TQ)

    def col_max(parts):
        acc = parts[0]
        for part in parts[1:]:
            acc = jnp.maximum(acc, part)
        return jnp.max(acc, axis=0, keepdims=True)

    def qk_scores(t):
        q = q_ref[t * DA_TQ:(t + 1) * DA_TQ, :]
        zero = jnp.zeros_like(q)
        q12 = jnp.concatenate(
            [jnp.where(lane < DA_QK_DIM, q, zero), jnp.where(lane >= DA_QK_DIM, q, zero)], axis=0)
        s_ref[t % 2] = _dot_nt(k_ref[...], q12)

    qk_scores(0)
    for t in range(n_kt):
        buf = t % 2
        if t + 1 < n_kt:
            qk_scores(t + 1)

        def scores(j):
            s = s_ref[buf, key_rows(j), :]
            if abs(j - t) <= 1:
                bias = tb_ref[j - t + 2]
                s = s + jnp.concatenate([bias, bias], axis=1)
            return s

        near = [j for j in range(n_kt) if abs(j - t) <= 1]
        left = [j for j in range(n_kt) if j < t - 1]
        right = [j for j in range(n_kt) if j > t + 1]
        m = col_max([scores(j) for j in near])
        if left:
            m = jnp.maximum(m, col_max([scores(j) for j in left]) + c_left)
        if right:
            m = jnp.maximum(m, col_max([scores(j) for j in right]) + c_right)

        for j in range(n_kt):
            shift = m if abs(j - t) <= 1 else (m - c_left if j < t else m - c_right)
            p_ref[buf, key_rows(j), :] = jnp.exp2(scores(j) - shift).astype(BF16)
        oe1 = _dot(vt_ref[...], p_ref[buf, :, :DA_TQ])
        oe2 = _dot(vt_ref[...], p_ref[buf, :, DA_TQ:])
        o1, l1 = oe1[:HEAD_DIM], oe1[HEAD_DIM:HEAD_DIM + 1]
        o2, l2 = oe2[:HEAD_DIM], oe2[HEAD_DIM:HEAD_DIM + 1]
        o = o1 * (1.0 / l1) - o2 * (lam / l2)
        ms = jnp.mean(o * o, axis=0, keepdims=True)
        y = (o * lax.rsqrt(ms + EPS) * g_ref[...]) * (1.0 - LAM_INIT)
        o_ref[t * DA_TQ:(t + 1) * DA_TQ, :] = y.T.astype(BF16)


def _da_attn(proj_hm, tb, da_lambda, subln_g, w_gate, w_up, w_down, c_pad, w_ada, b_ada, ada_col0,
             *, batch, seq, heads, q_blk0, k_blk0, v_blk0):
    kern = lambda *a: _da_kernel(*a, seq=seq)
    head_blk = lambda blk0: pl.BlockSpec((None, seq, HEAD_DIM), lambda b, h: (blk0 + h, b, 0))
    n_steps = batch * heads

    def row_slab(w):
        assert w.shape[0] % (16 * n_steps) == 0
        return pl.BlockSpec((w.shape[0] // n_steps, w.shape[1]), lambda b, h: (b * heads + h, 0))

    weights = (w_gate, w_up, w_down)
    D = c_pad.shape[1]
    n_ada = w_ada.shape[1] - ada_col0
    ada_tn = n_ada // n_steps
    assert ada_tn % 128 == 0 and ada_col0 % ada_tn == 0
    ada_blk0 = ada_col0 // ada_tn
    return pl.pallas_call(
        kern,
        out_shape=(jax.ShapeDtypeStruct((batch * seq, heads * HEAD_DIM), BF16),
                   *[jax.ShapeDtypeStruct(w.shape, BF16) for w in weights],
                   jax.ShapeDtypeStruct((c_pad.shape[0], n_ada), F32)),
        grid=(batch, heads),
        in_specs=[
            head_blk(q_blk0), head_blk(k_blk0), head_blk(v_blk0),
            pl.BlockSpec((DA_SLABS, DA_TQ, DA_TQ), lambda b, h: (h, 0, 0)),
            pl.BlockSpec(da_lambda.shape, lambda b, h: (0, 0)),
            pl.BlockSpec((HEAD_DIM, 1), lambda b, h: (0, 0)),
            *[row_slab(w) for w in weights],
            pl.BlockSpec(c_pad.shape, lambda b, h: (0, 0)),
            pl.BlockSpec((D, ada_tn), lambda b, h: (0, ada_blk0 + b * heads + h)),
            pl.BlockSpec((1, ada_tn), lambda b, h: (0, ada_blk0 + b * heads + h)),
        ],
        out_specs=(pl.BlockSpec((seq, HEAD_DIM), lambda b, h: (b, h)),
                   *[row_slab(w) for w in weights],
                   pl.BlockSpec((c_pad.shape[0], ada_tn), lambda b, h: (0, b * heads + h))),
        scratch_shapes=[
            pltpu.VMEM((DA_VT_ROWS, seq), BF16),
            pltpu.VMEM((2, seq, 2 * DA_TQ), F32),
            pltpu.VMEM((2, seq, 2 * DA_TQ), BF16),
        ],
        compiler_params=pltpu.CompilerParams(
            dimension_semantics=("arbitrary", "arbitrary"), vmem_limit_bytes=48 * MIB),
        name="da_attn",
    )(proj_hm, proj_hm, proj_hm, tb, da_lambda, subln_g.reshape(HEAD_DIM, 1), *weights,
      c_pad, w_ada, b_ada)


def _outproj_kernel(na_ref, da_ref, nag_ref, w_ref, x_ref, g1_ref, o_ref, wbf_ref, nan_ref):
    kh = na_ref.shape[1]

    @pl.when(pl.program_id(0) == 0)
    def _():
        def body(r, carry):
            rows = pl.ds(pl.multiple_of(r * WCAST_ROW_CHUNK, WCAST_ROW_CHUNK), WCAST_ROW_CHUNK)
            wbf_ref[rows, :] = w_ref[rows, :].astype(BF16)
            return carry
        lax.fori_loop(0, w_ref.shape[0] // WCAST_ROW_CHUNK, body, 0)

    gain = nag_ref[...]

    def body(r, carry):
        rows = pl.ds(pl.multiple_of(r * NORM_ROW_CHUNK, NORM_ROW_CHUNK), NORM_ROW_CHUNK)
        o = na_ref[rows, :].astype(F32)
        ms = jnp.mean(o * o, axis=-1, keepdims=True)
        nan_ref[rows, :] = (o * lax.rsqrt(ms + EPS) * gain).astype(BF16)
        return carry
    lax.fori_loop(0, na_ref.shape[0] // NORM_ROW_CHUNK, body, 0)

    mix = _dot(nan_ref[...], wbf_ref[:kh, :]) + _dot(da_ref[...], wbf_ref[kh:, :])
    o_ref[...] = x_ref[...] + g1_ref[0] * mix


def _out_proj(na_o, da_o, na_out_g, w_out, x2d, mod3, *, seq):
    M, D = x2d.shape
    Kh = na_o.shape[1]
    tm = 512
    per_b = seq // tm
    return pl.pallas_call(
        _outproj_kernel,
        out_shape=jax.ShapeDtypeStruct((M, D), F32),
        grid=(M // tm,),
        in_specs=[
            pl.BlockSpec((tm, Kh), lambda i: (i, 0)),
            pl.BlockSpec((tm, Kh), lambda i: (i, 0)),
            pl.BlockSpec((1, Kh), lambda i: (0, 0)),
            pl.BlockSpec(w_out.shape, lambda i: (0, 0), pipeline_mode=pl.Buffered(1)),
            pl.BlockSpec((tm, D), lambda i: (i, 0)),
            pl.BlockSpec((1, 1, D), lambda i: ((i // per_b) * 4 + 0, 0, 0)),
        ],
        out_specs=pl.BlockSpec((tm, D), lambda i: (i, 0)),
        scratch_shapes=[pltpu.VMEM(w_out.shape, BF16), pltpu.VMEM((tm, Kh), BF16)],
        compiler_params=pltpu.CompilerParams(
            dimension_semantics=("arbitrary",), vmem_limit_bytes=56 * MIB),
        name="out_proj",
    )(na_o, da_o, na_out_g, w_out, x2d, mod3)


def _ffn_kernel(x_ref, sh_ref, sc_ref, g2_ref, ng_ref, fg_ref, wg_ref, wu_ref, wd_ref, o_ref, h_ref,
                *, n_f):
    f = pl.program_id(1)
    n_chunks = x_ref.shape[0] // FFN_ROW_CHUNK

    def chunk_rows(r):
        return pl.ds(pl.multiple_of(r * FFN_ROW_CHUNK, FFN_ROW_CHUNK), FFN_ROW_CHUNK)

    def norm_rows(r):
        return pl.ds(pl.multiple_of(r * NORM_ROW_CHUNK, NORM_ROW_CHUNK), NORM_ROW_CHUNK)

    n_norm_chunks = x_ref.shape[0] // NORM_ROW_CHUNK

    @pl.when(f == 0)
    def _():
        gain = ng_ref[...] * (1.0 + sc_ref[0])
        shift = sh_ref[0]

        def body(r, carry):
            rows = norm_rows(r)
            x = x_ref[rows, :]
            ms = jnp.mean(x * x, axis=-1, keepdims=True)
            h_ref[rows, :] = (x * lax.rsqrt(ms + EPS) * gain + shift).astype(BF16)
            o_ref[rows, :] = jnp.zeros((NORM_ROW_CHUNK, o_ref.shape[1]), F32)
            return carry
        lax.fori_loop(0, n_norm_chunks, body, 0)

    def body(r, carry):
        rows = chunk_rows(r)
        h = h_ref[rows, :]
        g = _dot(h, wg_ref[...])
        u = _dot(h, wu_ref[...])
        a = (g * _sigmoid(g) * u).astype(BF16)
        o_ref[rows, :] += _dot(a, wd_ref[...])
        return carry
    lax.fori_loop(0, n_chunks, body, 0, unroll=True)

    @pl.when(f == n_f - 1)
    def _():
        gate = g2_ref[0]
        final_gain = fg_ref[...]

        def body(r, carry):
            rows = norm_rows(r)
            x2 = x_ref[rows, :] + gate * o_ref[rows, :]
            ms = jnp.mean(x2 * x2, axis=-1, keepdims=True)
            o_ref[rows, :] = x2 * lax.rsqrt(ms + EPS) * final_gain
            return carry
        lax.fori_loop(0, n_norm_chunks, body, 0)


def _ffn(x1, mod3, norm_g, final_g, w_gate, w_up, w_down, *, seq):
    M, D = x1.shape
    F = w_gate.shape[1]
    tm, tf = 1024, 512
    per_b = seq // tm
    n_f = F // tf
    kern = lambda *a: _ffn_kernel(*a, n_f=n_f)
    return pl.pallas_call(
        kern,
        out_shape=jax.ShapeDtypeStruct((M, D), F32),
        grid=(M // tm, n_f),
        in_specs=[
            pl.BlockSpec((tm, D), lambda i, f: (i, 0)),
            pl.BlockSpec((1, 1, D), lambda i, f: ((i // per_b) * 4 + 1, 0, 0)),
            pl.BlockSpec((1, 1, D), lambda i, f: ((i // per_b) * 4 + 2, 0, 0)),
            pl.BlockSpec((1, 1, D), lambda i, f: ((i // per_b) * 4 + 3, 0, 0)),
            pl.BlockSpec((1, D), lambda i, f: (0, 0)),
            pl.BlockSpec((1, D), lambda i, f: (0, 0)),
            pl.BlockSpec((D, tf), lambda i, f: (0, f)),
            pl.BlockSpec((D, tf), lambda i, f: (0, f)),
            pl.BlockSpec((tf, D), lambda i, f: (f, 0)),
        ],
        out_specs=pl.BlockSpec((tm, D), lambda i, f: (i, 0)),
        scratch_shapes=[pltpu.VMEM((tm, D), BF16)],
        compiler_params=pltpu.CompilerParams(
            dimension_semantics=("arbitrary", "arbitrary"), vmem_limit_bytes=60 * MIB),
        name="ffn",
    )(x1, mod3, mod3, mod3, norm_g, final_g, w_gate, w_up, w_down)


def kernel(x, c, w_ada, b_ada, norm1_g, w_in, na_rpb, na_out_g, da_lambda, da_subln_g, t5_table,
           w_out, norm2_g, w_gate, w_up, w_down, final_g):
    B, S, D = x.shape
    na_heads = na_rpb.shape[1]
    da_heads = t5_table.shape[1]
    na_width = na_heads * HEAD_DIM
    assert w_ada.shape[0] == 1, "single layer"

    c_pad = jnp.zeros((8, D), F32).at[:B].set(c)
    b_ada2d = b_ada[0].reshape(1, -1)
    mod_a = _ada_mod(c_pad, w_ada[0], b_ada2d, 2 * D)[:B].reshape(B * 2, 1, D)
    x2d = x.reshape(B * S, D)

    proj_hm = _in_proj(x2d, mod_a, norm1_g, w_in[0], seq=S,
                       na_width=na_width, da_q_off=3 * na_width)
    na_o = _na_attn(proj_hm, _na_table(na_rpb[0]), batch=B, seq=S, heads=na_heads,
                    q_blk0=0, k_blk0=na_heads, v_blk0=2 * na_heads)
    da_o, wg_bf, wu_bf, wd_bf, mod_b = _da_attn(
        proj_hm, _da_table(t5_table), da_lambda[0], da_subln_g, w_gate[0], w_up[0], w_down[0],
        c_pad, w_ada[0], b_ada2d, 2 * D,
        batch=B, seq=S, heads=da_heads, q_blk0=3 * na_heads, k_blk0=3 * na_heads + da_heads,
        v_blk0=3 * na_heads + 2 * da_heads)
    mod_b = mod_b[:B].reshape(B * 4, 1, D)
    x1 = _out_proj(na_o, da_o, na_out_g, w_out[0], x2d, mod_b, seq=S)
    out = _ffn(x1, mod_b, norm2_g, final_g.reshape(1, D), wg_bf, wu_bf, wd_bf, seq=S)
    return out.reshape(B, S, D)
```

```python
import math

import jax
import jax.numpy as jnp
from jax import lax
from jax.experimental import pallas as pl
from jax.experimental.pallas import tpu as pltpu

F32 = jnp.float32
BF16 = jnp.bfloat16

EPS = 1e-6
NEG = -1e30

GRID_W = 64
HEAD_DIM = 128
DA_QK_DIM = 64
NA_WIN_ROWS = 8
NA_WIN_COLS = 16
T5_BUCKETS = 32
T5_MAX_DIST = 128
LAM_INIT = 0.8 - 0.6 * math.exp(-0.3 * 0)
LOG2E = math.log2(math.e)

MIB = 1024 * 1024

NA_QROWS = 8
NA_KROWS = 16
NA_TQ = NA_QROWS * GRID_W
NA_TK = NA_KROWS * GRID_W

DA_TQ = 256
DA_VT_ROWS = HEAD_DIM + 16

FFN_ROW_CHUNK = 256
NORM_ROW_CHUNK = 128
WCAST_ROW_CHUNK = 128


def _dot(a, b):
    return jnp.dot(a, b, preferred_element_type=F32)


def _dot_nt(a, b):
    return lax.dot_general(a, b, (((1,), (1,)), ((), ())), preferred_element_type=F32)


def _sigmoid(x):
    return 1.0 / (1.0 + jnp.exp(-x))


def _ada_columns(c_ref, w_ref, b_ref):
    c = c_ref[...]
    cs = c * _sigmoid(c)
    return _dot(cs.astype(BF16), w_ref[...].astype(BF16)) + b_ref[...]


def _ada_kernel(c_ref, w_ref, b_ref, o_ref):
    o_ref[...] = _ada_columns(c_ref, w_ref, b_ref)


def _ada_mod(c_pad, w_ada, b_ada, n_cols):
    rows, D = c_pad.shape
    tn = 1024
    return pl.pallas_call(
        _ada_kernel,
        out_shape=jax.ShapeDtypeStruct((rows, n_cols), F32),
        grid=(n_cols // tn,),
        in_specs=[
            pl.BlockSpec((rows, D), lambda j: (0, 0)),
            pl.BlockSpec((D, tn), lambda j: (0, j)),
            pl.BlockSpec((1, tn), lambda j: (0, j)),
        ],
        out_specs=pl.BlockSpec((rows, tn), lambda j: (0, j)),
        compiler_params=pltpu.CompilerParams(
            dimension_semantics=("arbitrary",), vmem_limit_bytes=40 * MIB),
        name="ada_mod",
    )(c_pad, w_ada, b_ada)


def _inproj_kernel(x_ref, sh_ref, sc_ref, g_ref, w_ref, o_ref, h_ref, *, na_q_blk, da_q_blk,
                   na_scale, da_scale):
    j = pl.program_id(1)

    @pl.when(j == 0)
    def _():
        gain = g_ref[...] * (1.0 + sc_ref[0])
        shift = sh_ref[0]

        def body(r, carry):
            rows = pl.ds(pl.multiple_of(r * NORM_ROW_CHUNK, NORM_ROW_CHUNK), NORM_ROW_CHUNK)
            x = x_ref[rows, :]
            ms = jnp.mean(x * x, axis=-1, keepdims=True)
            h_ref[rows, :] = (x * lax.rsqrt(ms + EPS) * gain + shift).astype(BF16)
            return carry
        lax.fori_loop(0, x_ref.shape[0] // NORM_ROW_CHUNK, body, 0)

    acc = _dot(h_ref[...], w_ref[...].astype(BF16))
    scale = jnp.where(j == na_q_blk, na_scale, jnp.where(j == da_q_blk, da_scale, 1.0))
    acc = acc * scale
    for k in range(o_ref.shape[0]):
        o_ref[k] = acc[:, k * HEAD_DIM:(k + 1) * HEAD_DIM].astype(BF16)


def _in_proj(x2d, mod3, norm_g, w_in, *, seq, na_width, da_q_off):
    M, D = x2d.shape
    N = w_in.shape[1]
    tm, tn = 1024, 1024
    per_b = seq // tm
    kern = lambda *a: _inproj_kernel(
        *a, na_q_blk=0, da_q_blk=da_q_off // tn,
        na_scale=HEAD_DIM ** -0.5 * LOG2E, da_scale=DA_QK_DIM ** -0.5 * LOG2E)
    assert na_width == tn
    return pl.pallas_call(
        kern,
        out_shape=jax.ShapeDtypeStruct((N // HEAD_DIM, M, HEAD_DIM), BF16),
        grid=(M // tm, N // tn),
        in_specs=[
            pl.BlockSpec((tm, D), lambda i, j: (i, 0)),
            pl.BlockSpec((1, 1, D), lambda i, j: ((i // per_b) * 2 + 0, 0, 0)),
            pl.BlockSpec((1, 1, D), lambda i, j: ((i // per_b) * 2 + 1, 0, 0)),
            pl.BlockSpec((1, D), lambda i, j: (0, 0)),
            pl.BlockSpec((D, tn), lambda i, j: (0, j)),
        ],
        out_specs=pl.BlockSpec((tn // HEAD_DIM, tm, HEAD_DIM), lambda i, j: (j, i, 0)),
        scratch_shapes=[pltpu.VMEM((tm, D), BF16)],
        compiler_params=pltpu.CompilerParams(
            dimension_semantics=("arbitrary", "arbitrary"), vmem_limit_bytes=48 * MIB),
        name="in_proj",
    )(x2d, mod3, mod3, norm_g, w_in)


NA_REL_ROWS = 2 * NA_WIN_ROWS - 1
NA_REL_COLS = 2 * NA_WIN_COLS - 1
NA_PAIR_TILES = 2 * NA_WIN_ROWS


def _na_table_kernel(rpb_ref, o_ref):
    h = pl.program_id(0)
    shape = (GRID_W, 2 * GRID_W)
    qc = lax.broadcasted_iota(jnp.int32, shape, 0)
    lane = lax.broadcasted_iota(jnp.int32, shape, 1)
    upper = lane >= GRID_W
    kc = jnp.where(upper, lane - GRID_W, lane)
    c0 = jnp.clip(qc - NA_WIN_COLS // 2, 0, GRID_W - NA_WIN_COLS)
    in_win = (kc >= c0) & (kc < c0 + NA_WIN_COLS)
    lane_v = lax.broadcasted_iota(jnp.int32, (8, 2 * GRID_W), 1)
    base = h * (NA_REL_ROWS * NA_REL_COLS)

    def rotated_rows(rel_row, center):
        if not 0 <= rel_row < NA_REL_ROWS:
            return jnp.full(shape, NEG, F32)
        vec = jnp.zeros(lane_v.shape, F32)
        for d in range(NA_REL_COLS):
            at = (center + d - (NA_WIN_COLS - 1)) % (2 * GRID_W)
            vec = jnp.where(lane_v == at, rpb_ref[base + rel_row * NA_REL_COLS + d] * LOG2E, vec)
        rows = jnp.concatenate([vec] * (GRID_W // 8), axis=0)
        return pltpu.roll(rows, 0, 1, stride=1, stride_axis=0)

    for e in range(NA_PAIR_TILES):
        tile = jnp.where(upper, rotated_rows(e, GRID_W), rotated_rows(e - 1, 0))
        o_ref[e] = jnp.where(in_win, tile, NEG)


def _na_table(rpb):
    heads = rpb.shape[0]
    assert rpb.shape[1:] == (NA_REL_ROWS, NA_REL_COLS)
    return pl.pallas_call(
        _na_table_kernel,
        out_shape=jax.ShapeDtypeStruct((heads * NA_PAIR_TILES, GRID_W, 2 * GRID_W), F32),
        grid=(heads,),
        in_specs=[pl.BlockSpec(memory_space=pltpu.SMEM)],
        out_specs=pl.BlockSpec((NA_PAIR_TILES, GRID_W, 2 * GRID_W), lambda h: (h, 0, 0)),
        compiler_params=pltpu.CompilerParams(dimension_semantics=("arbitrary",)),
        name="na_table",
    )(rpb.reshape(-1))


def _na_window_start_row(blk, rows):
    return min(max(blk * NA_QROWS - (NA_KROWS - NA_QROWS) // 2, 0), rows - NA_KROWS)


def _na_block(q, kw, vw, tp_ref, blk, rows):
    lane = lax.broadcasted_iota(jnp.int32, (GRID_W, 2 * GRID_W), 1)
    kr0 = _na_window_start_row(blk, rows)
    s = _dot_nt(q, kw)
    p_rows, inv_l = [], []
    for i in range(NA_QROWS):
        qr = blk * NA_QROWS + i
        rs = min(max(qr - NA_WIN_ROWS // 2, 0), rows - NA_WIN_ROWS)
        tiles = {}
        for m in range(NA_KROWS // 2):
            ka = kr0 + 2 * m
            va = rs <= ka < rs + NA_WIN_ROWS
            vb = rs <= ka + 1 < rs + NA_WIN_ROWS
            if not (va or vb):
                continue
            e = ka - qr + NA_WIN_ROWS
            t = s[i * GRID_W:(i + 1) * GRID_W, m * 2 * GRID_W:(m + 1) * 2 * GRID_W] + tp_ref[e]
            if not vb:
                t = jnp.where(lane < GRID_W, t, NEG)
            elif not va:
                t = jnp.where(lane >= GRID_W, t, NEG)
            tiles[m] = t
        ts = list(tiles.values())
        mx = ts[0]
        for t in ts[1:]:
            mx = jnp.maximum(mx, t)
        mx = jnp.max(mx, axis=-1, keepdims=True)
        ps = {m: jnp.exp2(t - mx) for m, t in tiles.items()}
        tot = None
        for pt in ps.values():
            tot = pt if tot is None else tot + pt
        inv_l.append(1.0 / jnp.sum(tot, axis=-1, keepdims=True))
        zero = jnp.zeros((GRID_W, 2 * GRID_W), BF16)
        p_rows.append(jnp.concatenate(
            [ps[m].astype(BF16) if m in ps else zero for m in range(NA_KROWS // 2)], axis=1))
    p = jnp.concatenate(p_rows, axis=0)
    return _dot(p, vw) * jnp.concatenate(inv_l, axis=0)


def _na_kernel(q_ref, k_ref, v_ref, tp_ref, o_ref, *, rows):
    for blk in range(rows // NA_QROWS):
        start = _na_window_start_row(blk, rows) * GRID_W
        o = _na_block(q_ref[blk * NA_TQ:(blk + 1) * NA_TQ, :], k_ref[start:start + NA_TK, :],
                      v_ref[start:start + NA_TK, :], tp_ref, blk, rows)
        o_ref[blk * NA_TQ:(blk + 1) * NA_TQ, :] = o.astype(BF16)


def _na_attn(proj_hm, tp, *, batch, seq, heads, q_blk0, k_blk0, v_blk0):
    rows = seq // GRID_W
    kern = lambda *a: _na_kernel(*a, rows=rows)
    head_blk = lambda blk0: pl.BlockSpec((None, seq, HEAD_DIM), lambda b, h: (blk0 + h, b, 0))
    return pl.pallas_call(
        kern,
        out_shape=jax.ShapeDtypeStruct((batch * seq, heads * HEAD_DIM), BF16),
        grid=(batch, heads),
        in_specs=[
            head_blk(q_blk0), head_blk(k_blk0), head_blk(v_blk0),
            pl.BlockSpec((NA_PAIR_TILES, GRID_W, 2 * GRID_W), lambda b, h: (h, 0, 0)),
        ],
        out_specs=pl.BlockSpec((seq, HEAD_DIM), lambda b, h: (b, h)),
        compiler_params=pltpu.CompilerParams(
            dimension_semantics=("arbitrary", "arbitrary"), vmem_limit_bytes=48 * MIB),
        name="na_attn",
    )(proj_hm, proj_hm, proj_hm, tp)


def _t5_bucket(rel):
    nb = T5_BUCKETS // 2
    ret = jnp.where(rel > 0, nb, 0)
    n = jnp.abs(rel)
    max_exact = nb // 2
    nf = jnp.maximum(n, 1).astype(jnp.float32)
    large = max_exact + (jnp.log(nf / max_exact) / math.log(T5_MAX_DIST / max_exact)
                         * (nb - max_exact)).astype(jnp.int32)
    large = jnp.minimum(large, nb - 1)
    return ret + jnp.where(n < max_exact, n, large)


DA_SLABS = 5


DA_REL_SPAN = 4 * DA_TQ


def _da_table_kernel(bucket_ref, t5_ref, o_ref, *, heads):
    h = pl.program_id(0)
    nb = T5_BUCKETS // 2
    bk = bucket_ref[...]
    vec = jnp.zeros(bk.shape, F32)
    for bkt in range(T5_BUCKETS):
        vec = jnp.where(bk == bkt, t5_ref[bkt * heads + h] * LOG2E, vec)
    rows = jnp.concatenate([vec] * (DA_TQ // 8), axis=0)
    rolled = pltpu.roll(rows, 1, 1, stride=1, stride_axis=0)
    for dj in (-1, 0, 1):
        lanes = 2 * DA_TQ - dj * DA_TQ
        o_ref[dj + 2] = rolled[:, lanes:lanes + DA_TQ]
    shape = (DA_TQ, DA_TQ)
    o_ref[0] = jnp.full(shape, t5_ref[(nb - 1) * heads + h] * LOG2E, F32)
    o_ref[DA_SLABS - 1] = jnp.full(shape, t5_ref[(T5_BUCKETS - 1) * heads + h] * LOG2E, F32)


def _da_table(t5_table):
    heads = t5_table.shape[1]
    assert DA_TQ >= T5_MAX_DIST
    rel = 2 * DA_TQ - 1 - jnp.arange(DA_REL_SPAN)
    bucket = jnp.broadcast_to(_t5_bucket(rel).astype(jnp.int32)[None, :], (8, DA_REL_SPAN))
    kern = lambda *a: _da_table_kernel(*a, heads=heads)
    return pl.pallas_call(
        kern,
        out_shape=jax.ShapeDtypeStruct((heads * DA_SLABS, DA_TQ, DA_TQ), F32),
        grid=(heads,),
        in_specs=[
            pl.BlockSpec((8, DA_REL_SPAN), lambda h: (0, 0)),
            pl.BlockSpec(memory_space=pltpu.SMEM),
        ],
        out_specs=pl.BlockSpec((DA_SLABS, DA_TQ, DA_TQ), lambda h: (h, 0, 0)),
        compiler_params=pltpu.CompilerParams(dimension_semantics=("arbitrary",)),
        name="da_table",
    )(bucket, t5_table.reshape(-1))


def _da_kernel(q_ref, k_ref, v_ref, tb_ref, lam_ref, g_ref, wg_ref, wu_ref, wd_ref,
               c_ref, wada_ref, bada_ref,
               o_ref, wg_bf_ref, wu_bf_ref, wd_bf_ref, mod_ref, vt_ref, s_ref, p_ref, *, seq):
    n_kt = seq // DA_TQ

    wg_bf_ref[...] = wg_ref[...].astype(BF16)
    wu_bf_ref[...] = wu_ref[...].astype(BF16)
    wd_bf_ref[...] = wd_ref[...].astype(BF16)
    mod_ref[...] = _ada_columns(c_ref, wada_ref, bada_ref)

    vt_ref[:HEAD_DIM, :] = v_ref[...].astype(F32).T.astype(BF16)
    row = lax.broadcasted_iota(jnp.int32, (DA_VT_ROWS - HEAD_DIM, seq), 0)
    vt_ref[HEAD_DIM:, :] = jnp.where(row == 0, 1.0, 0.0).astype(BF16)

    lp = lam_ref[...]
    t1 = jnp.sum(lp[0:1] * lp[1:2], axis=-1, keepdims=True)
    t2 = jnp.sum(lp[2:3] * lp[3:4], axis=-1, keepdims=True)
    lam = jnp.exp(t1) - jnp.exp(t2) + LAM_INIT

    c_left = tb_ref[0, 0:1, 0:1]
    c_right = tb_ref[DA_SLABS - 1, 0:1, 0:1]
    lane = lax.broadcasted_iota(jnp.int32, (DA_TQ, HEAD_DIM), 1)

    def key_rows(j):
        return slice(j * DA_TQ, (j + 1) * DA_TQ)

    def col_max(parts):
        acc = parts[0]
        for part in parts[1:]:
            acc = jnp.maximum(acc, part)
        return jnp.max(acc, axis=0, keepdims=True)

    def qk_scores(t):
        q = q_ref[t * DA_TQ:(t + 1) * DA_TQ, :]
        zero = jnp.zeros_like(q)
        q12 = jnp.concatenate(
            [jnp.where(lane < DA_QK_DIM, q, zero), jnp.where(lane >= DA_QK_DIM, q, zero)], axis=0)
        s_ref[t % 2] = _dot_nt(k_ref[...], q12)

    qk_scores(0)
    for t in range(n_kt):
        buf = t % 2
        if t + 1 < n_kt:
            qk_scores(t + 1)

        def scores(j):
            s = s_ref[buf, key_rows(j), :]
            if abs(j - t) <= 1:
                bias = tb_ref[j - t + 2]
                s = s + jnp.concatenate([bias, bias], axis=1)
            return s

        near = [j for j in range(n_kt) if abs(j - t) <= 1]
        left = [j for j in range(n_kt) if j < t - 1]
        right = [j for j in range(n_kt) if j > t + 1]
        m = col_max([scores(j) for j in near])
        if left:
            m = jnp.maximum(m, col_max([scores(j) for j in left]) + c_left)
        if right:
            m = jnp.maximum(m, col_max([scores(j) for j in right]) + c_right)

        for j in range(n_kt):
            shift = m if abs(j - t) <= 1 else (m - c_left if j < t else m - c_right)
            p_ref[buf, key_rows(j), :] = jnp.exp2(scores(j) - shift).astype(BF16)
        oe1 = _dot(vt_ref[...], p_ref[buf, :, :DA_TQ])
        oe2 = _dot(vt_ref[...], p_ref[buf, :, DA_TQ:])
        o1, l1 = oe1[:HEAD_DIM], oe1[HEAD_DIM:HEAD_DIM + 1]
        o2, l2 = oe2[:HEAD_DIM], oe2[HEAD_DIM:HEAD_DIM + 1]
        o = o1 * (1.0 / l1) - o2 * (lam / l2)
        ms = jnp.mean(o * o, axis=0, keepdims=True)
        y = (o * lax.rsqrt(ms + EPS) * g_ref[...]) * (1.0 - LAM_INIT)
        o_ref[t * DA_TQ:(t + 1) * DA_TQ, :] = y.T.astype(BF16)


def _da_attn(proj_hm, tb, da_lambda, subln_g, w_gate, w_up, w_down, c_pad, w_ada, b_ada, ada_col0,
             *, batch, seq, heads, q_blk0, k_blk0, v_blk0):
    kern = lambda *a: _da_kernel(*a, seq=seq)
    head_blk = lambda blk0: pl.BlockSpec((None, seq, HEAD_DIM), lambda b, h: (blk0 + h, b, 0))
    n_steps = batch * heads

    def row_slab(w):
        assert w.shape[0] % (16 * n_steps) == 0
        return pl.BlockSpec((w.shape[0] // n_steps, w.shape[1]), lambda b, h: (b * heads + h, 0))

    weights = (w_gate, w_up, w_down)
    D = c_pad.shape[1]
    n_ada = w_ada.shape[1] - ada_col0
    ada_tn = n_ada // n_steps
    assert ada_tn % 128 == 0 and ada_col0 % ada_tn == 0
    ada_blk0 = ada_col0 // ada_tn
    return pl.pallas_call(
        kern,
        out_shape=(jax.ShapeDtypeStruct((batch * seq, heads * HEAD_DIM), BF16),
                   *[jax.ShapeDtypeStruct(w.shape, BF16) for w in weights],
                   jax.ShapeDtypeStruct((c_pad.shape[0], n_ada), F32)),
        grid=(batch, heads),
        in_specs=[
            head_blk(q_blk0), head_blk(k_blk0), head_blk(v_blk0),
            pl.BlockSpec((DA_SLABS, DA_TQ, DA_TQ), lambda b, h: (h, 0, 0)),
            pl.BlockSpec(da_lambda.shape, lambda b, h: (0, 0)),
            pl.BlockSpec((HEAD_DIM, 1), lambda b, h: (0, 0)),
            *[row_slab(w) for w in weights],
            pl.BlockSpec(c_pad.shape, lambda b, h: (0, 0)),
            pl.BlockSpec((D, ada_tn), lambda b, h: (0, ada_blk0 + b * heads + h)),
            pl.BlockSpec((1, ada_tn), lambda b, h: (0, ada_blk0 + b * heads + h)),
        ],
        out_specs=(pl.BlockSpec((seq, HEAD_DIM), lambda b, h: (b, h)),
                   *[row_slab(w) for w in weights],
                   pl.BlockSpec((c_pad.shape[0], ada_tn), lambda b, h: (0, b * heads + h))),
        scratch_shapes=[
            pltpu.VMEM((DA_VT_ROWS, seq), BF16),
            pltpu.VMEM((2, seq, 2 * DA_TQ), F32),
            pltpu.VMEM((2, seq, 2 * DA_TQ), BF16),
        ],
        compiler_params=pltpu.CompilerParams(
            dimension_semantics=("arbitrary", "arbitrary"), vmem_limit_bytes=48 * MIB),
        name="da_attn",
    )(proj_hm, proj_hm, proj_hm, tb, da_lambda, subln_g.reshape(HEAD_DIM, 1), *weights,
      c_pad, w_ada, b_ada)


def _outproj_kernel(na_ref, da_ref, nag_ref, w_ref, x_ref, g1_ref, o_ref, wbf_ref, nan_ref):
    kh = na_ref.shape[1]

    @pl.when(pl.program_id(0) == 0)
    def _():
        def body(r, carry):
            rows = pl.ds(pl.multiple_of(r * WCAST_ROW_CHUNK, WCAST_ROW_CHUNK), WCAST_ROW_CHUNK)
            wbf_ref[rows, :] = w_ref[rows, :].astype(BF16)
            return carry
        lax.fori_loop(0, w_ref.shape[0] // WCAST_ROW_CHUNK, body, 0)

    gain = nag_ref[...]

    def body(r, carry):
        rows = pl.ds(pl.multiple_of(r * NORM_ROW_CHUNK, NORM_ROW_CHUNK), NORM_ROW_CHUNK)
        o = na_ref[rows, :].astype(F32)
        ms = jnp.mean(o * o, axis=-1, keepdims=True)
        nan_ref[rows, :] = (o * lax.rsqrt(ms + EPS) * gain).astype(BF16)
        return carry
    lax.fori_loop(0, na_ref.shape[0] // NORM_ROW_CHUNK, body, 0)

    mix = _dot(nan_ref[...], wbf_ref[:kh, :]) + _dot(da_ref[...], wbf_ref[kh:, :])
    o_ref[...] = x_ref[...] + g1_ref[0] * mix


def _out_proj(na_o, da_o, na_out_g, w_out, x2d, mod3, *, seq):
    M, D = x2d.shape
    Kh = na_o.shape[1]
    tm = 512
    per_b = seq // tm
    return pl.pallas_call(
        _outproj_kernel,
        out_shape=jax.ShapeDtypeStruct((M, D), F32),
        grid=(M // tm,),
        in_specs=[
            pl.BlockSpec((tm, Kh), lambda i: (i, 0)),
            pl.BlockSpec((tm, Kh), lambda i: (i, 0)),
            pl.BlockSpec((1, Kh), lambda i: (0, 0)),
            pl.BlockSpec(w_out.shape, lambda i: (0, 0), pipeline_mode=pl.Buffered(1)),
            pl.BlockSpec((tm, D), lambda i: (i, 0)),
            pl.BlockSpec((1, 1, D), lambda i: ((i // per_b) * 4 + 0, 0, 0)),
        ],
        out_specs=pl.BlockSpec((tm, D), lambda i: (i, 0)),
        scratch_shapes=[pltpu.VMEM(w_out.shape, BF16), pltpu.VMEM((tm, Kh), BF16)],
        compiler_params=pltpu.CompilerParams(
            dimension_semantics=("arbitrary",), vmem_limit_bytes=56 * MIB),
        name="out_proj",
    )(na_o, da_o, na_out_g, w_out, x2d, mod3)


def _ffn_kernel(x_ref, sh_ref, sc_ref, g2_ref, ng_ref, fg_ref, wg_ref, wu_ref, wd_ref, o_ref, h_ref,
                *, n_f):
    f = pl.program_id(1)
    n_chunks = x_ref.shape[0] // FFN_ROW_CHUNK
    assert n_f >= 2

    def step(first, last):
        for r in range(n_chunks):
            rows = slice(r * FFN_ROW_CHUNK, (r + 1) * FFN_ROW_CHUNK)
            if first:
                gain = ng_ref[...] * (1.0 + sc_ref[0])
                for sub in range(FFN_ROW_CHUNK // NORM_ROW_CHUNK):
                    nrows = slice(rows.start + sub * NORM_ROW_CHUNK,
                                  rows.start + (sub + 1) * NORM_ROW_CHUNK)
                    x = x_ref[nrows, :]
                    ms = jnp.mean(x * x, axis=-1, keepdims=True)
                    h_ref[nrows, :] = (x * lax.rsqrt(ms + EPS) * gain + sh_ref[0]).astype(BF16)
            h = h_ref[rows, :]
            g = _dot(h, wg_ref[...])
            u = _dot(h, wu_ref[...])
            a = (g * _sigmoid(g) * u).astype(BF16)
            part = _dot(a, wd_ref[...])
            acc = part if first else o_ref[rows, :] + part
            if last:
                x2 = x_ref[rows, :] + g2_ref[0] * acc
                ms = jnp.mean(x2 * x2, axis=-1, keepdims=True)
                acc = x2 * lax.rsqrt(ms + EPS) * fg_ref[...]
            o_ref[rows, :] = acc

    pl.when(f == 0)(lambda: step(True, False))
    pl.when(jnp.logical_and(f > 0, f < n_f - 1))(lambda: step(False, False))
    pl.when(f == n_f - 1)(lambda: step(False, True))


def _ffn(x1, mod3, norm_g, final_g, w_gate, w_up, w_down, *, seq):
    M, D = x1.shape
    F = w_gate.shape[1]
    tm, tf = 1024, 512
    per_b = seq // tm
    n_f = F // tf
    kern = lambda *a: _ffn_kernel(*a, n_f=n_f)
    return pl.pallas_call(
        kern,
        out_shape=jax.ShapeDtypeStruct((M, D), F32),
        grid=(M // tm, n_f),
        in_specs=[
            pl.BlockSpec((tm, D), lambda i, f: (i, 0)),
            pl.BlockSpec((1, 1, D), lambda i, f: ((i // per_b) * 4 + 1, 0, 0)),
            pl.BlockSpec((1, 1, D), lambda i, f: ((i // per_b) * 4 + 2, 0, 0)),
            pl.BlockSpec((1, 1, D), lambda i, f: ((i // per_b) * 4 + 3, 0, 0)),
            pl.BlockSpec((1, D), lambda i, f: (0, 0)),
            pl.BlockSpec((1, D), lambda i, f: (0, 0)),
            pl.BlockSpec((D, tf), lambda i, f: (0, f)),
            pl.BlockSpec((D, tf), lambda i, f: (0, f)),
            pl.BlockSpec((tf, D), lambda i, f: (f, 0)),
        ],
        out_specs=pl.BlockSpec((tm, D), lambda i, f: (i, 0)),
        scratch_shapes=[pltpu.VMEM((tm, D), BF16)],
        compiler_params=pltpu.CompilerParams(
            dimension_semantics=("arbitrary", "arbitrary"), vmem_limit_bytes=60 * MIB),
        name="ffn",
    )(x1, mod3, mod3, mod3, norm_g, final_g, w_gate, w_up, w_down)


def kernel(x, c, w_ada, b_ada, norm1_g, w_in, na_rpb, na_out_g, da_lambda, da_subln_g, t5_table,
           w_out, norm2_g, w_gate, w_up, w_down, final_g):
    B, S, D = x.shape
    na_heads = na_rpb.shape[1]
    da_heads = t5_table.shape[1]
    na_width = na_heads * HEAD_DIM
    assert w_ada.shape[0] == 1, "single layer"

    c_pad = jnp.zeros((8, D), F32).at[:B].set(c)
    b_ada2d = b_ada[0].reshape(1, -1)
    mod_a = _ada_mod(c_pad, w_ada[0], b_ada2d, 2 * D)[:B].reshape(B * 2, 1, D)
    x2d = x.reshape(B * S, D)

    proj_hm = _in_proj(x2d, mod_a, norm1_g, w_in[0], seq=S,
                       na_width=na_width, da_q_off=3 * na_width)
    na_o = _na_attn(proj_hm, _na_table(na_rpb[0]), batch=B, seq=S, heads=na_heads,
                    q_blk0=0, k_blk0=na_heads, v_blk0=2 * na_heads)
    da_o, wg_bf, wu_bf, wd_bf, mod_b = _da_attn(
        proj_hm, _da_table(t5_table), da_lambda[0], da_subln_g, w_gate[0], w_up[0], w_down[0],
        c_pad, w_ada[0], b_ada2d, 2 * D,
        batch=B, seq=S, heads=da_heads, q_blk0=3 * na_heads, k_blk0=3 * na_heads + da_heads,
        v_blk0=3 * na_heads + 2 * da_heads)
    mod_b = mod_b[:B].reshape(B * 4, 1, D)
    x1 = _out_proj(na_o, da_o, na_out_g, w_out[0], x2d, mod_b, seq=S)
    out = _ffn(x1, mod_b, norm2_g, final_g.reshape(1, D), wg_bf, wu_bf, wd_bf, seq=S)
    return out.reshape(B, S, D)
```

```python
import math

import jax
import jax.numpy as jnp
from jax import lax
from jax.experimental import pallas as pl
from jax.experimental.pallas import tpu as pltpu

F32 = jnp.float32
BF16 = jnp.bfloat16

EPS = 1e-6
NEG = -1e30

GRID_W = 64
HEAD_DIM = 128
DA_QK_DIM = 64
NA_WIN_ROWS = 8
NA_WIN_COLS = 16
T5_BUCKETS = 32
T5_MAX_DIST = 128
LAM_INIT = 0.8 - 0.6 * math.exp(-0.3 * 0)
LOG2E = math.log2(math.e)

MIB = 1024 * 1024

NA_QROWS = 8
NA_KROWS = 16
NA_TQ = NA_QROWS * GRID_W
NA_TK = NA_KROWS * GRID_W

DA_TQ = 256
DA_VT_ROWS = HEAD_DIM + 16

FFN_ROW_CHUNK = 256
NORM_ROW_CHUNK = 128
INPROJ_ROW_CHUNK = 256
OUTPROJ_ROW_CHUNK = 256
WCAST_ROW_CHUNK = 128


def _dot(a, b):
    return jnp.dot(a, b, preferred_element_type=F32)


def _dot_nt(a, b):
    return lax.dot_general(a, b, (((1,), (1,)), ((), ())), preferred_element_type=F32)


def _sigmoid(x):
    return 1.0 / (1.0 + jnp.exp(-x))


def _ada_columns(c_ref, w_ref, b_ref):
    c = c_ref[...]
    cs = c * _sigmoid(c)
    return _dot(cs.astype(BF16), w_ref[...].astype(BF16)) + b_ref[...]


def _ada_kernel(c_ref, w_ref, b_ref, o_ref):
    o_ref[...] = _ada_columns(c_ref, w_ref, b_ref)


def _ada_mod(c_pad, w_ada, b_ada, n_cols):
    rows, D = c_pad.shape
    tn = 1024
    return pl.pallas_call(
        _ada_kernel,
        out_shape=jax.ShapeDtypeStruct((rows, n_cols), F32),
        grid=(n_cols // tn,),
        in_specs=[
            pl.BlockSpec((rows, D), lambda j: (0, 0)),
            pl.BlockSpec((D, tn), lambda j: (0, j)),
            pl.BlockSpec((1, tn), lambda j: (0, j)),
        ],
        out_specs=pl.BlockSpec((rows, tn), lambda j: (0, j)),
        compiler_params=pltpu.CompilerParams(
            dimension_semantics=("arbitrary",), vmem_limit_bytes=40 * MIB),
        name="ada_mod",
    )(c_pad, w_ada, b_ada)


def _inproj_kernel(x_ref, sh_ref, sc_ref, g_ref, w_ref, o_ref, h_ref, *, na_q_blk, da_q_blk,
                   na_scale, da_scale):
    j = pl.program_id(1)
    scale = jnp.where(j == na_q_blk, na_scale, jnp.where(j == da_q_blk, da_scale, 1.0))

    def project(rows):
        acc = _dot(h_ref[rows, :], w_ref[...].astype(BF16)) * scale
        for k in range(o_ref.shape[0]):
            o_ref[k, rows, :] = acc[:, k * HEAD_DIM:(k + 1) * HEAD_DIM].astype(BF16)

    @pl.when(j == 0)
    def _():
        gain = g_ref[...] * (1.0 + sc_ref[0])
        tm = x_ref.shape[0]
        for r in range(tm // INPROJ_ROW_CHUNK):
            for sub in range(INPROJ_ROW_CHUNK // NORM_ROW_CHUNK):
                start = r * INPROJ_ROW_CHUNK + sub * NORM_ROW_CHUNK
                nrows = slice(start, start + NORM_ROW_CHUNK)
                x = x_ref[nrows, :]
                ms = jnp.mean(x * x, axis=-1, keepdims=True)
                h_ref[nrows, :] = (x * lax.rsqrt(ms + EPS) * gain + sh_ref[0]).astype(BF16)
            project(slice(r * INPROJ_ROW_CHUNK, (r + 1) * INPROJ_ROW_CHUNK))

    @pl.when(j > 0)
    def _():
        project(slice(None))


def _in_proj(x2d, mod3, norm_g, w_in, *, seq, na_width, da_q_off):
    M, D = x2d.shape
    N = w_in.shape[1]
    tm, tn = 1024, 1024
    per_b = seq // tm
    kern = lambda *a: _inproj_kernel(
        *a, na_q_blk=0, da_q_blk=da_q_off // tn,
        na_scale=HEAD_DIM ** -0.5 * LOG2E, da_scale=DA_QK_DIM ** -0.5 * LOG2E)
    assert na_width == tn
    return pl.pallas_call(
        kern,
        out_shape=jax.ShapeDtypeStruct((N // HEAD_DIM, M, HEAD_DIM), BF16),
        grid=(M // tm, N // tn),
        in_specs=[
            pl.BlockSpec((tm, D), lambda i, j: (i, 0)),
            pl.BlockSpec((1, 1, D), lambda i, j: ((i // per_b) * 2 + 0, 0, 0)),
            pl.BlockSpec((1, 1, D), lambda i, j: ((i // per_b) * 2 + 1, 0, 0)),
            pl.BlockSpec((1, D), lambda i, j: (0, 0)),
            pl.BlockSpec((D, tn), lambda i, j: (0, j)),
        ],
        out_specs=pl.BlockSpec((tn // HEAD_DIM, tm, HEAD_DIM), lambda i, j: (j, i, 0)),
        scratch_shapes=[pltpu.VMEM((tm, D), BF16)],
        compiler_params=pltpu.CompilerParams(
            dimension_semantics=("arbitrary", "arbitrary"), vmem_limit_bytes=48 * MIB),
        name="in_proj",
    )(x2d, mod3, mod3, norm_g, w_in)


NA_REL_ROWS = 2 * NA_WIN_ROWS - 1
NA_REL_COLS = 2 * NA_WIN_COLS - 1
NA_PAIR_TILES = 2 * NA_WIN_ROWS


def _na_table_kernel(rpb_ref, o_ref):
    h = pl.program_id(0)
    shape = (GRID_W, 2 * GRID_W)
    qc = lax.broadcasted_iota(jnp.int32, shape, 0)
    lane = lax.broadcasted_iota(jnp.int32, shape, 1)
    upper = lane >= GRID_W
    kc = jnp.where(upper, lane - GRID_W, lane)
    c0 = jnp.clip(qc - NA_WIN_COLS // 2, 0, GRID_W - NA_WIN_COLS)
    in_win = (kc >= c0) & (kc < c0 + NA_WIN_COLS)
    lane_v = lax.broadcasted_iota(jnp.int32, (8, 2 * GRID_W), 1)
    base = h * (NA_REL_ROWS * NA_REL_COLS)

    def rotated_rows(rel_row, center):
        if not 0 <= rel_row < NA_REL_ROWS:
            return jnp.full(shape, NEG, F32)
        vec = jnp.zeros(lane_v.shape, F32)
        for d in range(NA_REL_COLS):
            at = (center + d - (NA_WIN_COLS - 1)) % (2 * GRID_W)
            vec = jnp.where(lane_v == at, rpb_ref[base + rel_row * NA_REL_COLS + d] * LOG2E, vec)
        rows = jnp.concatenate([vec] * (GRID_W // 8), axis=0)
        return pltpu.roll(rows, 0, 1, stride=1, stride_axis=0)

    for e in range(NA_PAIR_TILES):
        tile = jnp.where(upper, rotated_rows(e, GRID_W), rotated_rows(e - 1, 0))
        o_ref[e] = jnp.where(in_win, tile, NEG)


def _na_table(rpb):
    heads = rpb.shape[0]
    assert rpb.shape[1:] == (NA_REL_ROWS, NA_REL_COLS)
    return pl.pallas_call(
        _na_table_kernel,
        out_shape=jax.ShapeDtypeStruct((heads * NA_PAIR_TILES, GRID_W, 2 * GRID_W), F32),
        grid=(heads,),
        in_specs=[pl.BlockSpec(memory_space=pltpu.SMEM)],
        out_specs=pl.BlockSpec((NA_PAIR_TILES, GRID_W, 2 * GRID_W), lambda h: (h, 0, 0)),
        compiler_params=pltpu.CompilerParams(dimension_semantics=("arbitrary",)),
        name="na_table",
    )(rpb.reshape(-1))


def _na_window_start_row(blk, rows):
    return min(max(blk * NA_QROWS - (NA_KROWS - NA_QROWS) // 2, 0), rows - NA_KROWS)


def _na_block(q, kw, vw, tp_ref, blk, rows):
    lane = lax.broadcasted_iota(jnp.int32, (GRID_W, 2 * GRID_W), 1)
    kr0 = _na_window_start_row(blk, rows)
    s = _dot_nt(q, kw)
    p_rows, inv_l = [], []
    for i in range(NA_QROWS):
        qr = blk * NA_QROWS + i
        rs = min(max(qr - NA_WIN_ROWS // 2, 0), rows - NA_WIN_ROWS)
        tiles = {}
        for m in range(NA_KROWS // 2):
            ka = kr0 + 2 * m
            va = rs <= ka < rs + NA_WIN_ROWS
            vb = rs <= ka + 1 < rs + NA_WIN_ROWS
            if not (va or vb):
                continue
            e = ka - qr + NA_WIN_ROWS
            t = s[i * GRID_W:(i + 1) * GRID_W, m * 2 * GRID_W:(m + 1) * 2 * GRID_W] + tp_ref[e]
            if not vb:
                t = jnp.where(lane < GRID_W, t, NEG)
            elif not va:
                t = jnp.where(lane >= GRID_W, t, NEG)
            tiles[m] = t
        ts = list(tiles.values())
        mx = ts[0]
        for t in ts[1:]:
            mx = jnp.maximum(mx, t)
        mx = jnp.max(mx, axis=-1, keepdims=True)
        ps = {m: jnp.exp2(t - mx) for m, t in tiles.items()}
        tot = None
        for pt in ps.values():
            tot = pt if tot is None else tot + pt
        inv_l.append(1.0 / jnp.sum(tot, axis=-1, keepdims=True))
        zero = jnp.zeros((GRID_W, 2 * GRID_W), BF16)
        p_rows.append(jnp.concatenate(
            [ps[m].astype(BF16) if m in ps else zero for m in range(NA_KROWS // 2)], axis=1))
    p = jnp.concatenate(p_rows, axis=0)
    return _dot(p, vw) * jnp.concatenate(inv_l, axis=0)


def _na_kernel(q_ref, k_ref, v_ref, tp_ref, o_ref, *, rows):
    for blk in range(rows // NA_QROWS):
        start = _na_window_start_row(blk, rows) * GRID_W
        o = _na_block(q_ref[blk * NA_TQ:(blk + 1) * NA_TQ, :], k_ref[start:start + NA_TK, :],
                      v_ref[start:start + NA_TK, :], tp_ref, blk, rows)
        o_ref[blk * NA_TQ:(blk + 1) * NA_TQ, :] = o.astype(BF16)


def _na_attn(proj_hm, tp, *, batch, seq, heads, q_blk0, k_blk0, v_blk0):
    rows = seq // GRID_W
    kern = lambda *a: _na_kernel(*a, rows=rows)
    head_blk = lambda blk0: pl.BlockSpec((None, seq, HEAD_DIM), lambda b, h: (blk0 + h, b, 0))
    return pl.pallas_call(
        kern,
        out_shape=jax.ShapeDtypeStruct((batch * seq, heads * HEAD_DIM), BF16),
        grid=(batch, heads),
        in_specs=[
            head_blk(q_blk0), head_blk(k_blk0), head_blk(v_blk0),
            pl.BlockSpec((NA_PAIR_TILES, GRID_W, 2 * GRID_W), lambda b, h: (h, 0, 0)),
        ],
        out_specs=pl.BlockSpec((seq, HEAD_DIM), lambda b, h: (b, h)),
        compiler_params=pltpu.CompilerParams(
            dimension_semantics=("arbitrary", "arbitrary"), vmem_limit_bytes=48 * MIB),
        name="na_attn",
    )(proj_hm, proj_hm, proj_hm, tp)


def _t5_bucket(rel):
    nb = T5_BUCKETS // 2
    ret = jnp.where(rel > 0, nb, 0)
    n = jnp.abs(rel)
    max_exact = nb // 2
    nf = jnp.maximum(n, 1).astype(jnp.float32)
    large = max_exact + (jnp.log(nf / max_exact) / math.log(T5_MAX_DIST / max_exact)
                         * (nb - max_exact)).astype(jnp.int32)
    large = jnp.minimum(large, nb - 1)
    return ret + jnp.where(n < max_exact, n, large)


DA_SLABS = 5


DA_REL_SPAN = 4 * DA_TQ


def _da_table_kernel(bucket_ref, t5_ref, o_ref, *, heads):
    h = pl.program_id(0)
    nb = T5_BUCKETS // 2
    bk = bucket_ref[...]
    vec = jnp.zeros(bk.shape, F32)
    for bkt in range(T5_BUCKETS):
        vec = jnp.where(bk == bkt, t5_ref[bkt * heads + h] * LOG2E, vec)
    rows = jnp.concatenate([vec] * (DA_TQ // 8), axis=0)
    rolled = pltpu.roll(rows, 1, 1, stride=1, stride_axis=0)
    for dj in (-1, 0, 1):
        lanes = 2 * DA_TQ - dj * DA_TQ
        o_ref[dj + 2] = rolled[:, lanes:lanes + DA_TQ]
    shape = (DA_TQ, DA_TQ)
    o_ref[0] = jnp.full(shape, t5_ref[(nb - 1) * heads + h] * LOG2E, F32)
    o_ref[DA_SLABS - 1] = jnp.full(shape, t5_ref[(T5_BUCKETS - 1) * heads + h] * LOG2E, F32)


def _da_table(t5_table):
    heads = t5_table.shape[1]
    assert DA_TQ >= T5_MAX_DIST
    rel = 2 * DA_TQ - 1 - jnp.arange(DA_REL_SPAN)
    bucket = jnp.broadcast_to(_t5_bucket(rel).astype(jnp.int32)[None, :], (8, DA_REL_SPAN))
    kern = lambda *a: _da_table_kernel(*a, heads=heads)
    return pl.pallas_call(
        kern,
        out_shape=jax.ShapeDtypeStruct((heads * DA_SLABS, DA_TQ, DA_TQ), F32),
        grid=(heads,),
        in_specs=[
            pl.BlockSpec((8, DA_REL_SPAN), lambda h: (0, 0)),
            pl.BlockSpec(memory_space=pltpu.SMEM),
        ],
        out_specs=pl.BlockSpec((DA_SLABS, DA_TQ, DA_TQ), lambda h: (h, 0, 0)),
        compiler_params=pltpu.CompilerParams(dimension_semantics=("arbitrary",)),
        name="da_table",
    )(bucket, t5_table.reshape(-1))


def _da_kernel(q_ref, k_ref, v_ref, tb_ref, lam_ref, g_ref, wg_ref, wu_ref, wd_ref,
               c_ref, wada_ref, bada_ref,
               o_ref, wg_bf_ref, wu_bf_ref, wd_bf_ref, mod_ref, vt_ref, s_ref, p_ref, *, seq):
    n_kt = seq // DA_TQ

    wg_bf_ref[...] = wg_ref[...].astype(BF16)
    wu_bf_ref[...] = wu_ref[...].astype(BF16)
    wd_bf_ref[...] = wd_ref[...].astype(BF16)
    mod_ref[...] = _ada_columns(c_ref, wada_ref, bada_ref)

    vt_ref[:HEAD_DIM, :] = v_ref[...].astype(F32).T.astype(BF16)
    row = lax.broadcasted_iota(jnp.int32, (DA_VT_ROWS - HEAD_DIM, seq), 0)
    vt_ref[HEAD_DIM:, :] = jnp.where(row == 0, 1.0, 0.0).astype(BF16)

    lp = lam_ref[...]
    t1 = jnp.sum(lp[0:1] * lp[1:2], axis=-1, keepdims=True)
    t2 = jnp.sum(lp[2:3] * lp[3:4], axis=-1, keepdims=True)
    lam = jnp.exp(t1) - jnp.exp(t2) + LAM_INIT

    c_left = tb_ref[0, 0:1, 0:1]
    c_right = tb_ref[DA_SLABS - 1, 0:1, 0:1]
    lane = lax.broadcasted_iota(jnp.int32, (DA_TQ, HEAD_DIM), 1)

    def key_rows(j):
        return slice(j * DA_TQ, (j + 1) * DA_TQ)

    def col_max(parts):
        acc = parts[0]
        for part in parts[1:]:
            acc = jnp.maximum(acc, part)
        return jnp.max(acc, axis=0, keepdims=True)

    def qk_scores(t):
        q = q_ref[t * DA_TQ:(t + 1) * DA_TQ, :]
        zero = jnp.zeros_like(q)
        q12 = jnp.concatenate(
            [jnp.where(lane < DA_QK_DIM, q, zero), jnp.where(lane >= DA_QK_DIM, q, zero)], axis=0)
        s_ref[t % 2] = _dot_nt(k_ref[...], q12)

    qk_scores(0)
    for t in range(n_kt):
        buf = t % 2
        if t + 1 < n_kt:
            qk_scores(t + 1)

        def scores(j):
            s = s_ref[buf, key_rows(j), :]
            if abs(j - t) <= 1:
                bias = tb_ref[j - t + 2]
                s = s + jnp.concatenate([bias, bias], axis=1)
            return s

        near = [j for j in range(n_kt) if abs(j - t) <= 1]
        left = [j for j in range(n_kt) if j < t - 1]
        right = [j for j in range(n_kt) if j > t + 1]
        m = col_max([scores(j) for j in near])
        if left:
            m = jnp.maximum(m, col_max([scores(j) for j in left]) + c_left)
        if right:
            m = jnp.maximum(m, col_max([scores(j) for j in right]) + c_right)

        for j in range(n_kt):
            shift = m if abs(j - t) <= 1 else (m - c_left if j < t else m - c_right)
            p_ref[buf, key_rows(j), :] = jnp.exp2(scores(j) - shift).astype(BF16)
        oe1 = _dot(vt_ref[...], p_ref[buf, :, :DA_TQ])
        oe2 = _dot(vt_ref[...], p_ref[buf, :, DA_TQ:])
        o1, l1 = oe1[:HEAD_DIM], oe1[HEAD_DIM:HEAD_DIM + 1]
        o2, l2 = oe2[:HEAD_DIM], oe2[HEAD_DIM:HEAD_DIM + 1]
        o = o1 * (1.0 / l1) - o2 * (lam / l2)
        ms = jnp.mean(o * o, axis=0, keepdims=True)
        y = (o * lax.rsqrt(ms + EPS) * g_ref[...]) * (1.0 - LAM_INIT)
        o_ref[t * DA_TQ:(t + 1) * DA_TQ, :] = y.T.astype(BF16)


def _da_attn(proj_hm, tb, da_lambda, subln_g, w_gate, w_up, w_down, c_pad, w_ada, b_ada, ada_col0,
             *, batch, seq, heads, q_blk0, k_blk0, v_blk0):
    kern = lambda *a: _da_kernel(*a, seq=seq)
    head_blk = lambda blk0: pl.BlockSpec((None, seq, HEAD_DIM), lambda b, h: (blk0 + h, b, 0))
    n_steps = batch * heads

    def row_slab(w):
        assert w.shape[0] % (16 * n_steps) == 0
        return pl.BlockSpec((w.shape[0] // n_steps, w.shape[1]), lambda b, h: (b * heads + h, 0))

    weights = (w_gate, w_up, w_down)
    D = c_pad.shape[1]
    n_ada = w_ada.shape[1] - ada_col0
    ada_tn = n_ada // n_steps
    assert ada_tn % 128 == 0 and ada_col0 % ada_tn == 0
    ada_blk0 = ada_col0 // ada_tn
    return pl.pallas_call(
        kern,
        out_shape=(jax.ShapeDtypeStruct((batch * seq, heads * HEAD_DIM), BF16),
                   *[jax.ShapeDtypeStruct(w.shape, BF16) for w in weights],
                   jax.ShapeDtypeStruct((c_pad.shape[0], n_ada), F32)),
        grid=(batch, heads),
        in_specs=[
            head_blk(q_blk0), head_blk(k_blk0), head_blk(v_blk0),
            pl.BlockSpec((DA_SLABS, DA_TQ, DA_TQ), lambda b, h: (h, 0, 0)),
            pl.BlockSpec(da_lambda.shape, lambda b, h: (0, 0)),
            pl.BlockSpec((HEAD_DIM, 1), lambda b, h: (0, 0)),
            *[row_slab(w) for w in weights],
            pl.BlockSpec(c_pad.shape, lambda b, h: (0, 0)),
            pl.BlockSpec((D, ada_tn), lambda b, h: (0, ada_blk0 + b * heads + h)),
            pl.BlockSpec((1, ada_tn), lambda b, h: (0, ada_blk0 + b * heads + h)),
        ],
        out_specs=(pl.BlockSpec((seq, HEAD_DIM), lambda b, h: (b, h)),
                   *[row_slab(w) for w in weights],
                   pl.BlockSpec((c_pad.shape[0], ada_tn), lambda b, h: (0, b * heads + h))),
        scratch_shapes=[
            pltpu.VMEM((DA_VT_ROWS, seq), BF16),
            pltpu.VMEM((2, seq, 2 * DA_TQ), F32),
            pltpu.VMEM((2, seq, 2 * DA_TQ), BF16),
        ],
        compiler_params=pltpu.CompilerParams(
            dimension_semantics=("arbitrary", "arbitrary"), vmem_limit_bytes=48 * MIB),
        name="da_attn",
    )(proj_hm, proj_hm, proj_hm, tb, da_lambda, subln_g.reshape(HEAD_DIM, 1), *weights,
      c_pad, w_ada, b_ada)


def _outproj_kernel(na_ref, da_ref, nag_ref, w_ref, x_ref, g1_ref, o_ref, wbf_ref):
    kh = na_ref.shape[1]

    @pl.when(pl.program_id(0) == 0)
    def _():
        def body(r, carry):
            rows = pl.ds(pl.multiple_of(r * WCAST_ROW_CHUNK, WCAST_ROW_CHUNK), WCAST_ROW_CHUNK)
            wbf_ref[rows, :] = w_ref[rows, :].astype(BF16)
            return carry
        lax.fori_loop(0, w_ref.shape[0] // WCAST_ROW_CHUNK, body, 0)

    gain = nag_ref[...]
    for r in range(na_ref.shape[0] // OUTPROJ_ROW_CHUNK):
        rows = slice(r * OUTPROJ_ROW_CHUNK, (r + 1) * OUTPROJ_ROW_CHUNK)
        o = na_ref[rows, :].astype(F32)
        ms = jnp.mean(o * o, axis=-1, keepdims=True)
        na_n = (o * lax.rsqrt(ms + EPS) * gain).astype(BF16)
        mix = _dot(na_n, wbf_ref[:kh, :]) + _dot(da_ref[rows, :], wbf_ref[kh:, :])
        o_ref[rows, :] = x_ref[rows, :] + g1_ref[0] * mix


def _out_proj(na_o, da_o, na_out_g, w_out, x2d, mod3, *, seq):
    M, D = x2d.shape
    Kh = na_o.shape[1]
    tm = 512
    per_b = seq // tm
    return pl.pallas_call(
        _outproj_kernel,
        out_shape=jax.ShapeDtypeStruct((M, D), F32),
        grid=(M // tm,),
        in_specs=[
            pl.BlockSpec((tm, Kh), lambda i: (i, 0)),
            pl.BlockSpec((tm, Kh), lambda i: (i, 0)),
            pl.BlockSpec((1, Kh), lambda i: (0, 0)),
            pl.BlockSpec(w_out.shape, lambda i: (0, 0), pipeline_mode=pl.Buffered(1)),
            pl.BlockSpec((tm, D), lambda i: (i, 0)),
            pl.BlockSpec((1, 1, D), lambda i: ((i // per_b) * 4 + 0, 0, 0)),
        ],
        out_specs=pl.BlockSpec((tm, D), lambda i: (i, 0)),
        scratch_shapes=[pltpu.VMEM(w_out.shape, BF16)],
        compiler_params=pltpu.CompilerParams(
            dimension_semantics=("arbitrary",), vmem_limit_bytes=56 * MIB),
        name="out_proj",
    )(na_o, da_o, na_out_g, w_out, x2d, mod3)


def _ffn_kernel(x_ref, sh_ref, sc_ref, g2_ref, ng_ref, fg_ref, wg_ref, wu_ref, wd_ref, o_ref, h_ref,
                *, n_f):
    f = pl.program_id(1)
    n_chunks = x_ref.shape[0] // FFN_ROW_CHUNK
    assert n_f >= 2

    def step(first, last):
        for r in range(n_chunks):
            rows = slice(r * FFN_ROW_CHUNK, (r + 1) * FFN_ROW_CHUNK)
            if first:
                gain = ng_ref[...] * (1.0 + sc_ref[0])
                for sub in range(FFN_ROW_CHUNK // NORM_ROW_CHUNK):
                    nrows = slice(rows.start + sub * NORM_ROW_CHUNK,
                                  rows.start + (sub + 1) * NORM_ROW_CHUNK)
                    x = x_ref[nrows, :]
                    ms = jnp.mean(x * x, axis=-1, keepdims=True)
                    h_ref[nrows, :] = (x * lax.rsqrt(ms + EPS) * gain + sh_ref[0]).astype(BF16)
            h = h_ref[rows, :]
            g = _dot(h, wg_ref[...])
            u = _dot(h, wu_ref[...])
            a = (g * _sigmoid(g) * u).astype(BF16)
            part = _dot(a, wd_ref[...])
            acc = part if first else o_ref[rows, :] + part
            if last:
                x2 = x_ref[rows, :] + g2_ref[0] * acc
                ms = jnp.mean(x2 * x2, axis=-1, keepdims=True)
                acc = x2 * lax.rsqrt(ms + EPS) * fg_ref[...]
            o_ref[rows, :] = acc

    pl.when(f == 0)(lambda: step(True, False))
    pl.when(jnp.logical_and(f > 0, f < n_f - 1))(lambda: step(False, False))
    pl.when(f == n_f - 1)(lambda: step(False, True))


def _ffn(x1, mod3, norm_g, final_g, w_gate, w_up, w_down, *, seq):
    M, D = x1.shape
    F = w_gate.shape[1]
    tm, tf = 1024, 512
    per_b = seq // tm
    n_f = F // tf
    kern = lambda *a: _ffn_kernel(*a, n_f=n_f)
    return pl.pallas_call(
        kern,
        out_shape=jax.ShapeDtypeStruct((M, D), F32),
        grid=(M // tm, n_f),
        in_specs=[
            pl.BlockSpec((tm, D), lambda i, f: (i, 0)),
            pl.BlockSpec((1, 1, D), lambda i, f: ((i // per_b) * 4 + 1, 0, 0)),
            pl.BlockSpec((1, 1, D), lambda i, f: ((i // per_b) * 4 + 2, 0, 0)),
            pl.BlockSpec((1, 1, D), lambda i, f: ((i // per_b) * 4 + 3, 0, 0)),
            pl.BlockSpec((1, D), lambda i, f: (0, 0)),
            pl.BlockSpec((1, D), lambda i, f: (0, 0)),
            pl.BlockSpec((D, tf), lambda i, f: (0, f)),
            pl.BlockSpec((D, tf), lambda i, f: (0, f)),
            pl.BlockSpec((tf, D), lambda i, f: (f, 0)),
        ],
        out_specs=pl.BlockSpec((tm, D), lambda i, f: (i, 0)),
        scratch_shapes=[pltpu.VMEM((tm, D), BF16)],
        compiler_params=pltpu.CompilerParams(
            dimension_semantics=("arbitrary", "arbitrary"), vmem_limit_bytes=60 * MIB),
        name="ffn",
    )(x1, mod3, mod3, mod3, norm_g, final_g, w_gate, w_up, w_down)


def kernel(x, c, w_ada, b_ada, norm1_g, w_in, na_rpb, na_out_g, da_lambda, da_subln_g, t5_table,
           w_out, norm2_g, w_gate, w_up, w_down, final_g):
    B, S, D = x.shape
    na_heads = na_rpb.shape[1]
    da_heads = t5_table.shape[1]
    na_width = na_heads * HEAD_DIM
    assert w_ada.shape[0] == 1, "single layer"

    c_pad = jnp.zeros((8, D), F32).at[:B].set(c)
    b_ada2d = b_ada[0].reshape(1, -1)
    mod_a = _ada_mod(c_pad, w_ada[0], b_ada2d, 2 * D)[:B].reshape(B * 2, 1, D)
    x2d = x.reshape(B * S, D)

    proj_hm = _in_proj(x2d, mod_a, norm1_g, w_in[0], seq=S,
                       na_width=na_width, da_q_off=3 * na_width)
    na_o = _na_attn(proj_hm, _na_table(na_rpb[0]), batch=B, seq=S, heads=na_heads,
                    q_blk0=0, k_blk0=na_heads, v_blk0=2 * na_heads)
    da_o, wg_bf, wu_bf, wd_bf, mod_b = _da_attn(
        proj_hm, _da_table(t5_table), da_lambda[0], da_subln_g, w_gate[0], w_up[0], w_down[0],
        c_pad, w_ada[0], b_ada2d, 2 * D,
        batch=B, seq=S, heads=da_heads, q_blk0=3 * na_heads, k_blk0=3 * na_heads + da_heads,
        v_blk0=3 * na_heads + 2 * da_heads)
    mod_b = mod_b[:B].reshape(B * 4, 1, D)
    x1 = _out_proj(na_o, da_o, na_out_g, w_out[0], x2d, mod_b, seq=S)
    out = _ffn(x1, mod_b, norm2_g, final_g.reshape(1, D), wg_bf, wu_bf, wd_bf, seq=S)
    return out.reshape(B, S, D)
```

```python
import math

import jax
import jax.numpy as jnp
from jax import lax
from jax.experimental import pallas as pl
from jax.experimental.pallas import tpu as pltpu

F32 = jnp.float32
BF16 = jnp.bfloat16

EPS = 1e-6
NEG = -1e30

GRID_W = 64
HEAD_DIM = 128
DA_QK_DIM = 64
NA_WIN_ROWS = 8
NA_WIN_COLS = 16
T5_BUCKETS = 32
T5_MAX_DIST = 128
LAM_INIT = 0.8 - 0.6 * math.exp(-0.3 * 0)
LOG2E = math.log2(math.e)

MIB = 1024 * 1024

NA_QROWS = 8
NA_KROWS = 16
NA_TQ = NA_QROWS * GRID_W
NA_TK = NA_KROWS * GRID_W

DA_TQ = 256
DA_VT_ROWS = HEAD_DIM + 16

FFN_ROW_CHUNK = 512
NORM_ROW_CHUNK = 128
INPROJ_ROW_CHUNK = 256
OUTPROJ_ROW_CHUNK = 256
WCAST_ROW_CHUNK = 128


def _dot(a, b):
    return jnp.dot(a, b, preferred_element_type=F32)


def _dot_nt(a, b):
    return lax.dot_general(a, b, (((1,), (1,)), ((), ())), preferred_element_type=F32)


def _sigmoid(x):
    return 1.0 / (1.0 + jnp.exp(-x))


def _ada_columns(c_ref, w_ref, b_ref):
    c = c_ref[...]
    cs = c * _sigmoid(c)
    return _dot(cs.astype(BF16), w_ref[...].astype(BF16)) + b_ref[...]


def _ada_kernel(c_ref, w_ref, b_ref, o_ref):
    o_ref[...] = _ada_columns(c_ref, w_ref, b_ref)


def _ada_mod(c_pad, w_ada, b_ada, n_cols):
    rows, D = c_pad.shape
    tn = 1024
    return pl.pallas_call(
        _ada_kernel,
        out_shape=jax.ShapeDtypeStruct((rows, n_cols), F32),
        grid=(n_cols // tn,),
        in_specs=[
            pl.BlockSpec((rows, D), lambda j: (0, 0)),
            pl.BlockSpec((D, tn), lambda j: (0, j)),
            pl.BlockSpec((1, tn), lambda j: (0, j)),
        ],
        out_specs=pl.BlockSpec((rows, tn), lambda j: (0, j)),
        compiler_params=pltpu.CompilerParams(
            dimension_semantics=("arbitrary",), vmem_limit_bytes=40 * MIB),
        name="ada_mod",
    )(c_pad, w_ada, b_ada)


def _inproj_kernel(x_ref, sh_ref, sc_ref, g_ref, w_ref, o_ref, h_ref, *, na_q_blk, da_q_blk,
                   na_scale, da_scale):
    j = pl.program_id(1)
    scale = jnp.where(j == na_q_blk, na_scale, jnp.where(j == da_q_blk, da_scale, 1.0))

    def project(rows):
        acc = _dot(h_ref[rows, :], w_ref[...].astype(BF16)) * scale
        for k in range(o_ref.shape[0]):
            o_ref[k, rows, :] = acc[:, k * HEAD_DIM:(k + 1) * HEAD_DIM].astype(BF16)

    @pl.when(j == 0)
    def _():
        gain = g_ref[...] * (1.0 + sc_ref[0])
        tm = x_ref.shape[0]
        for r in range(tm // INPROJ_ROW_CHUNK):
            for sub in range(INPROJ_ROW_CHUNK // NORM_ROW_CHUNK):
                start = r * INPROJ_ROW_CHUNK + sub * NORM_ROW_CHUNK
                nrows = slice(start, start + NORM_ROW_CHUNK)
                x = x_ref[nrows, :]
                ms = jnp.mean(x * x, axis=-1, keepdims=True)
                h_ref[nrows, :] = (x * lax.rsqrt(ms + EPS) * gain + sh_ref[0]).astype(BF16)
            project(slice(r * INPROJ_ROW_CHUNK, (r + 1) * INPROJ_ROW_CHUNK))

    @pl.when(j > 0)
    def _():
        project(slice(None))


def _in_proj(x2d, mod3, norm_g, w_in, *, seq, na_width, da_q_off):
    M, D = x2d.shape
    N = w_in.shape[1]
    tm, tn = 1024, 1024
    per_b = seq // tm
    kern = lambda *a: _inproj_kernel(
        *a, na_q_blk=0, da_q_blk=da_q_off // tn,
        na_scale=HEAD_DIM ** -0.5 * LOG2E, da_scale=DA_QK_DIM ** -0.5 * LOG2E)
    assert na_width == tn
    return pl.pallas_call(
        kern,
        out_shape=jax.ShapeDtypeStruct((N // HEAD_DIM, M, HEAD_DIM), BF16),
        grid=(M // tm, N // tn),
        in_specs=[
            pl.BlockSpec((tm, D), lambda i, j: (i, 0)),
            pl.BlockSpec((1, 1, D), lambda i, j: ((i // per_b) * 2 + 0, 0, 0)),
            pl.BlockSpec((1, 1, D), lambda i, j: ((i // per_b) * 2 + 1, 0, 0)),
            pl.BlockSpec((1, D), lambda i, j: (0, 0)),
            pl.BlockSpec((D, tn), lambda i, j: (0, j)),
        ],
        out_specs=pl.BlockSpec((tn // HEAD_DIM, tm, HEAD_DIM), lambda i, j: (j, i, 0)),
        scratch_shapes=[pltpu.VMEM((tm, D), BF16)],
        compiler_params=pltpu.CompilerParams(
            dimension_semantics=("arbitrary", "arbitrary"), vmem_limit_bytes=48 * MIB),
        name="in_proj",
    )(x2d, mod3, mod3, norm_g, w_in)


NA_REL_ROWS = 2 * NA_WIN_ROWS - 1
NA_REL_COLS = 2 * NA_WIN_COLS - 1
NA_PAIR_TILES = 2 * NA_WIN_ROWS


def _na_table_kernel(rpb_ref, o_ref):
    h = pl.program_id(0)
    shape = (GRID_W, 2 * GRID_W)
    qc = lax.broadcasted_iota(jnp.int32, shape, 0)
    lane = lax.broadcasted_iota(jnp.int32, shape, 1)
    upper = lane >= GRID_W
    kc = jnp.where(upper, lane - GRID_W, lane)
    c0 = jnp.clip(qc - NA_WIN_COLS // 2, 0, GRID_W - NA_WIN_COLS)
    in_win = (kc >= c0) & (kc < c0 + NA_WIN_COLS)
    lane_v = lax.broadcasted_iota(jnp.int32, (8, 2 * GRID_W), 1)
    base = h * (NA_REL_ROWS * NA_REL_COLS)

    def rotated_rows(rel_row, center):
        if not 0 <= rel_row < NA_REL_ROWS:
            return jnp.full(shape, NEG, F32)
        vec = jnp.zeros(lane_v.shape, F32)
        for d in range(NA_REL_COLS):
            at = (center + d - (NA_WIN_COLS - 1)) % (2 * GRID_W)
            vec = jnp.where(lane_v == at, rpb_ref[base + rel_row * NA_REL_COLS + d] * LOG2E, vec)
        rows = jnp.concatenate([vec] * (GRID_W // 8), axis=0)
        return pltpu.roll(rows, 0, 1, stride=1, stride_axis=0)

    for e in range(NA_PAIR_TILES):
        tile = jnp.where(upper, rotated_rows(e, GRID_W), rotated_rows(e - 1, 0))
        o_ref[e] = jnp.where(in_win, tile, NEG)


def _na_table(rpb):
    heads = rpb.shape[0]
    assert rpb.shape[1:] == (NA_REL_ROWS, NA_REL_COLS)
    return pl.pallas_call(
        _na_table_kernel,
        out_shape=jax.ShapeDtypeStruct((heads * NA_PAIR_TILES, GRID_W, 2 * GRID_W), F32),
        grid=(heads,),
        in_specs=[pl.BlockSpec(memory_space=pltpu.SMEM)],
        out_specs=pl.BlockSpec((NA_PAIR_TILES, GRID_W, 2 * GRID_W), lambda h: (h, 0, 0)),
        compiler_params=pltpu.CompilerParams(dimension_semantics=("arbitrary",)),
        name="na_table",
    )(rpb.reshape(-1))


def _na_window_start_row(blk, rows):
    return min(max(blk * NA_QROWS - (NA_KROWS - NA_QROWS) // 2, 0), rows - NA_KROWS)


def _na_block(q, kw, vw, tp_ref, blk, rows):
    lane = lax.broadcasted_iota(jnp.int32, (GRID_W, 2 * GRID_W), 1)
    kr0 = _na_window_start_row(blk, rows)
    s = _dot_nt(q, kw)
    p_rows, inv_l = [], []
    for i in range(NA_QROWS):
        qr = blk * NA_QROWS + i
        rs = min(max(qr - NA_WIN_ROWS // 2, 0), rows - NA_WIN_ROWS)
        tiles = {}
        for m in range(NA_KROWS // 2):
            ka = kr0 + 2 * m
            va = rs <= ka < rs + NA_WIN_ROWS
            vb = rs <= ka + 1 < rs + NA_WIN_ROWS
            if not (va or vb):
                continue
            e = ka - qr + NA_WIN_ROWS
            t = s[i * GRID_W:(i + 1) * GRID_W, m * 2 * GRID_W:(m + 1) * 2 * GRID_W] + tp_ref[e]
            if not vb:
                t = jnp.where(lane < GRID_W, t, NEG)
            elif not va:
                t = jnp.where(lane >= GRID_W, t, NEG)
            tiles[m] = t
        ts = list(tiles.values())
        mx = ts[0]
        for t in ts[1:]:
            mx = jnp.maximum(mx, t)
        mx = jnp.max(mx, axis=-1, keepdims=True)
        ps = {m: jnp.exp2(t - mx) for m, t in tiles.items()}
        tot = None
        for pt in ps.values():
            tot = pt if tot is None else tot + pt
        inv_l.append(1.0 / jnp.sum(tot, axis=-1, keepdims=True))
        zero = jnp.zeros((GRID_W, 2 * GRID_W), BF16)
        p_rows.append(jnp.concatenate(
            [ps[m].astype(BF16) if m in ps else zero for m in range(NA_KROWS // 2)], axis=1))
    p = jnp.concatenate(p_rows, axis=0)
    return _dot(p, vw) * jnp.concatenate(inv_l, axis=0)


def _na_kernel(q_ref, k_ref, v_ref, tp_ref, o_ref, *, rows):
    for blk in range(rows // NA_QROWS):
        start = _na_window_start_row(blk, rows) * GRID_W
        o = _na_block(q_ref[blk * NA_TQ:(blk + 1) * NA_TQ, :], k_ref[start:start + NA_TK, :],
                      v_ref[start:start + NA_TK, :], tp_ref, blk, rows)
        o_ref[blk * NA_TQ:(blk + 1) * NA_TQ, :] = o.astype(BF16)


def _na_attn(proj_hm, tp, *, batch, seq, heads, q_blk0, k_blk0, v_blk0):
    rows = seq // GRID_W
    kern = lambda *a: _na_kernel(*a, rows=rows)
    head_blk = lambda blk0: pl.BlockSpec((None, seq, HEAD_DIM), lambda b, h: (blk0 + h, b, 0))
    return pl.pallas_call(
        kern,
        out_shape=jax.ShapeDtypeStruct((batch * seq, heads * HEAD_DIM), BF16),
        grid=(batch, heads),
        in_specs=[
            head_blk(q_blk0), head_blk(k_blk0), head_blk(v_blk0),
            pl.BlockSpec((NA_PAIR_TILES, GRID_W, 2 * GRID_W), lambda b, h: (h, 0, 0)),
        ],
        out_specs=pl.BlockSpec((seq, HEAD_DIM), lambda b, h: (b, h)),
        compiler_params=pltpu.CompilerParams(
            dimension_semantics=("arbitrary", "arbitrary"), vmem_limit_bytes=48 * MIB),
        name="na_attn",
    )(proj_hm, proj_hm, proj_hm, tp)


def _t5_bucket(rel):
    nb = T5_BUCKETS // 2
    ret = jnp.where(rel > 0, nb, 0)
    n = jnp.abs(rel)
    max_exact = nb // 2
    nf = jnp.maximum(n, 1).astype(jnp.float32)
    large = max_exact + (jnp.log(nf / max_exact) / math.log(T5_MAX_DIST / max_exact)
                         * (nb - max_exact)).astype(jnp.int32)
    large = jnp.minimum(large, nb - 1)
    return ret + jnp.where(n < max_exact, n, large)


DA_SLABS = 5


DA_REL_SPAN = 4 * DA_TQ


def _da_table_kernel(bias_ref, o_ref):
    vec = bias_ref[...] * LOG2E
    rows = jnp.broadcast_to(vec, (DA_TQ, DA_REL_SPAN))
    rolled = pltpu.roll(rows, 1, 1, stride=1, stride_axis=0)
    for dj in (-1, 0, 1):
        lanes = 2 * DA_TQ - dj * DA_TQ
        o_ref[dj + 2] = rolled[:, lanes:lanes + DA_TQ]
    shape = (DA_TQ, DA_TQ)
    o_ref[0] = jnp.broadcast_to(vec[:, DA_REL_SPAN - 1:], shape)
    o_ref[DA_SLABS - 1] = jnp.broadcast_to(vec[:, :1], shape)


def _da_table(t5_table):
    heads = t5_table.shape[1]
    assert DA_TQ >= T5_MAX_DIST
    rel = 2 * DA_TQ - 1 - jnp.arange(DA_REL_SPAN)
    bias = t5_table.astype(F32)[_t5_bucket(rel)].T.reshape(heads, 1, DA_REL_SPAN)
    return pl.pallas_call(
        _da_table_kernel,
        out_shape=jax.ShapeDtypeStruct((heads * DA_SLABS, DA_TQ, DA_TQ), F32),
        grid=(heads,),
        in_specs=[pl.BlockSpec((None, 1, DA_REL_SPAN), lambda h: (h, 0, 0))],
        out_specs=pl.BlockSpec((DA_SLABS, DA_TQ, DA_TQ), lambda h: (h, 0, 0)),
        compiler_params=pltpu.CompilerParams(dimension_semantics=("arbitrary",)),
        name="da_table",
    )(bias)


def _da_kernel(q_ref, k_ref, v_ref, tb_ref, lam_ref, g_ref, wg_ref, wu_ref, wd_ref,
               c_ref, wada_ref, bada_ref,
               o_ref, wg_bf_ref, wu_bf_ref, wd_bf_ref, mod_ref, vt_ref, s_ref, p_ref, *, seq):
    n_kt = seq // DA_TQ

    wg_bf_ref[...] = wg_ref[...].astype(BF16)
    wu_bf_ref[...] = wu_ref[...].astype(BF16)
    wd_bf_ref[...] = wd_ref[...].astype(BF16)
    mod_ref[...] = _ada_columns(c_ref, wada_ref, bada_ref)

    vt_ref[:HEAD_DIM, :] = v_ref[...].astype(F32).T.astype(BF16)
    row = lax.broadcasted_iota(jnp.int32, (DA_VT_ROWS - HEAD_DIM, seq), 0)
    vt_ref[HEAD_DIM:, :] = jnp.where(row == 0, 1.0, 0.0).astype(BF16)

    lp = lam_ref[...]
    t1 = jnp.sum(lp[0:1] * lp[1:2], axis=-1, keepdims=True)
    t2 = jnp.sum(lp[2:3] * lp[3:4], axis=-1, keepdims=True)
    lam = jnp.exp(t1) - jnp.exp(t2) + LAM_INIT

    c_left = tb_ref[0, 0:1, 0:1]
    c_right = tb_ref[DA_SLABS - 1, 0:1, 0:1]
    lane = lax.broadcasted_iota(jnp.int32, (DA_TQ, HEAD_DIM), 1)

    def key_rows(j):
        return slice(j * DA_TQ, (j + 1) * DA_TQ)

    def col_max(parts):
        acc = parts[0]
        for part in parts[1:]:
            acc = jnp.maximum(acc, part)
        return jnp.max(acc, axis=0, keepdims=True)

    def qk_scores(t):
        q = q_ref[t * DA_TQ:(t + 1) * DA_TQ, :]
        zero = jnp.zeros_like(q)
        q12 = jnp.concatenate(
            [jnp.where(lane < DA_QK_DIM, q, zero), jnp.where(lane >= DA_QK_DIM, q, zero)], axis=0)
        s_ref[t % 2] = _dot_nt(k_ref[...], q12)

    qk_scores(0)
    for t in range(n_kt):
        buf = t % 2
        if t + 1 < n_kt:
            qk_scores(t + 1)

        def scores(j):
            s = s_ref[buf, key_rows(j), :]
            if abs(j - t) <= 1:
                bias = tb_ref[j - t + 2]
                s = s + jnp.concatenate([bias, bias], axis=1)
            return s

        near = [j for j in range(n_kt) if abs(j - t) <= 1]
        left = [j for j in range(n_kt) if j < t - 1]
        right = [j for j in range(n_kt) if j > t + 1]
        m = col_max([scores(j) for j in near])
        if left:
            m = jnp.maximum(m, col_max([scores(j) for j in left]) + c_left)
        if right:
            m = jnp.maximum(m, col_max([scores(j) for j in right]) + c_right)

        for j in range(n_kt):
            shift = m if abs(j - t) <= 1 else (m - c_left if j < t else m - c_right)
            p_ref[buf, key_rows(j), :] = jnp.exp2(scores(j) - shift).astype(BF16)
        oe1 = _dot(vt_ref[...], p_ref[buf, :, :DA_TQ])
        oe2 = _dot(vt_ref[...], p_ref[buf, :, DA_TQ:])
        o1, l1 = oe1[:HEAD_DIM], oe1[HEAD_DIM:HEAD_DIM + 1]
        o2, l2 = oe2[:HEAD_DIM], oe2[HEAD_DIM:HEAD_DIM + 1]
        o = o1 * (1.0 / l1) - o2 * (lam / l2)
        ms = jnp.mean(o * o, axis=0, keepdims=True)
        y = ((o * lax.rsqrt(ms + EPS)).T * g_ref[...]) * (1.0 - LAM_INIT)
        o_ref[t * DA_TQ:(t + 1) * DA_TQ, :] = y.astype(BF16)


def _da_attn(proj_hm, tb, da_lambda, subln_g, w_gate, w_up, w_down, c_pad, w_ada, b_ada, ada_col0,
             *, batch, seq, heads, q_blk0, k_blk0, v_blk0):
    kern = lambda *a: _da_kernel(*a, seq=seq)
    head_blk = lambda blk0: pl.BlockSpec((None, seq, HEAD_DIM), lambda b, h: (blk0 + h, b, 0))
    n_steps = batch * heads

    def row_slab(w):
        assert w.shape[0] % (16 * n_steps) == 0
        return pl.BlockSpec((w.shape[0] // n_steps, w.shape[1]), lambda b, h: (b * heads + h, 0))

    weights = (w_gate, w_up, w_down)
    D = c_pad.shape[1]
    n_ada = w_ada.shape[1] - ada_col0
    ada_tn = n_ada // n_steps
    assert ada_tn % 128 == 0 and ada_col0 % ada_tn == 0
    ada_blk0 = ada_col0 // ada_tn
    return pl.pallas_call(
        kern,
        out_shape=(jax.ShapeDtypeStruct((batch * seq, heads * HEAD_DIM), BF16),
                   *[jax.ShapeDtypeStruct(w.shape, BF16) for w in weights],
                   jax.ShapeDtypeStruct((c_pad.shape[0], n_ada), F32)),
        grid=(batch, heads),
        in_specs=[
            head_blk(q_blk0), head_blk(k_blk0), head_blk(v_blk0),
            pl.BlockSpec((DA_SLABS, DA_TQ, DA_TQ), lambda b, h: (h, 0, 0)),
            pl.BlockSpec(da_lambda.shape, lambda b, h: (0, 0)),
            pl.BlockSpec((1, HEAD_DIM), lambda b, h: (0, 0)),
            *[row_slab(w) for w in weights],
            pl.BlockSpec(c_pad.shape, lambda b, h: (0, 0)),
            pl.BlockSpec((D, ada_tn), lambda b, h: (0, ada_blk0 + b * heads + h)),
            pl.BlockSpec((1, ada_tn), lambda b, h: (0, ada_blk0 + b * heads + h)),
        ],
        out_specs=(pl.BlockSpec((seq, HEAD_DIM), lambda b, h: (b, h)),
                   *[row_slab(w) for w in weights],
                   pl.BlockSpec((c_pad.shape[0], ada_tn), lambda b, h: (0, b * heads + h))),
        scratch_shapes=[
            pltpu.VMEM((DA_VT_ROWS, seq), BF16),
            pltpu.VMEM((2, seq, 2 * DA_TQ), F32),
            pltpu.VMEM((2, seq, 2 * DA_TQ), BF16),
        ],
        compiler_params=pltpu.CompilerParams(
            dimension_semantics=("arbitrary", "arbitrary"), vmem_limit_bytes=48 * MIB),
        name="da_attn",
    )(proj_hm, proj_hm, proj_hm, tb, da_lambda, subln_g, *weights,
      c_pad, w_ada, b_ada)


def _outproj_kernel(na_ref, da_ref, nag_ref, w_ref, x_ref, g1_ref, o_ref, wbf_ref):
    kh = na_ref.shape[1]

    @pl.when(pl.program_id(0) == 0)
    def _():
        def body(r, carry):
            rows = pl.ds(pl.multiple_of(r * WCAST_ROW_CHUNK, WCAST_ROW_CHUNK), WCAST_ROW_CHUNK)
            wbf_ref[rows, :] = w_ref[rows, :].astype(BF16)
            return carry
        lax.fori_loop(0, w_ref.shape[0] // WCAST_ROW_CHUNK, body, 0)

    gain = nag_ref[...]
    for r in range(na_ref.shape[0] // OUTPROJ_ROW_CHUNK):
        rows = slice(r * OUTPROJ_ROW_CHUNK, (r + 1) * OUTPROJ_ROW_CHUNK)
        o = na_ref[rows, :].astype(F32)
        ms = jnp.mean(o * o, axis=-1, keepdims=True)
        na_n = (o * lax.rsqrt(ms + EPS) * gain).astype(BF16)
        mix = _dot(na_n, wbf_ref[:kh, :]) + _dot(da_ref[rows, :], wbf_ref[kh:, :])
        o_ref[rows, :] = x_ref[rows, :] + g1_ref[0] * mix


def _out_proj(na_o, da_o, na_out_g, w_out, x2d, mod3, *, seq):
    M, D = x2d.shape
    Kh = na_o.shape[1]
    tm = 512
    per_b = seq // tm
    return pl.pallas_call(
        _outproj_kernel,
        out_shape=jax.ShapeDtypeStruct((M, D), F32),
        grid=(M // tm,),
        in_specs=[
            pl.BlockSpec((tm, Kh), lambda i: (i, 0)),
            pl.BlockSpec((tm, Kh), lambda i: (i, 0)),
            pl.BlockSpec((1, Kh), lambda i: (0, 0)),
            pl.BlockSpec(w_out.shape, lambda i: (0, 0), pipeline_mode=pl.Buffered(1)),
            pl.BlockSpec((tm, D), lambda i: (i, 0)),
            pl.BlockSpec((1, 1, D), lambda i: ((i // per_b) * 4 + 0, 0, 0)),
        ],
        out_specs=pl.BlockSpec((tm, D), lambda i: (i, 0)),
        scratch_shapes=[pltpu.VMEM(w_out.shape, BF16)],
        compiler_params=pltpu.CompilerParams(
            dimension_semantics=("arbitrary",), vmem_limit_bytes=56 * MIB),
        name="out_proj",
    )(na_o, da_o, na_out_g, w_out, x2d, mod3)


def _ffn_kernel(x_ref, sh_ref, sc_ref, g2_ref, ng_ref, fg_ref, wg_ref, wu_ref, wd_ref, o_ref, h_ref,
                *, n_f):
    f = pl.program_id(1)
    tm = x_ref.shape[0]
    assert n_f >= 2

    def step(first, last):
        chunk = FFN_ROW_CHUNK if (first or last) else tm
        for r in range(tm // chunk):
            rows = slice(r * chunk, (r + 1) * chunk)
            if first:
                gain = ng_ref[...] * (1.0 + sc_ref[0])
                for sub in range(chunk // NORM_ROW_CHUNK):
                    nrows = slice(rows.start + sub * NORM_ROW_CHUNK,
                                  rows.start + (sub + 1) * NORM_ROW_CHUNK)
                    x = x_ref[nrows, :]
                    ms = jnp.mean(x * x, axis=-1, keepdims=True)
                    h_ref[nrows, :] = (x * lax.rsqrt(ms + EPS) * gain + sh_ref[0]).astype(BF16)
            h = h_ref[rows, :]
            g = _dot(h, wg_ref[...])
            u = _dot(h, wu_ref[...])
            a = (g * _sigmoid(g) * u).astype(BF16)
            part = _dot(a, wd_ref[...])
            acc = part if first else o_ref[rows, :] + part
            if last:
                x2 = x_ref[rows, :] + g2_ref[0] * acc
                ms = jnp.mean(x2 * x2, axis=-1, keepdims=True)
                acc = x2 * lax.rsqrt(ms + EPS) * fg_ref[...]
            o_ref[rows, :] = acc

    pl.when(f == 0)(lambda: step(True, False))
    pl.when(jnp.logical_and(f > 0, f < n_f - 1))(lambda: step(False, False))
    pl.when(f == n_f - 1)(lambda: step(False, True))


def _ffn(x1, mod3, norm_g, final_g, w_gate, w_up, w_down, *, seq):
    M, D = x1.shape
    F = w_gate.shape[1]
    tm, tf = 1024, 512
    per_b = seq // tm
    n_f = F // tf
    kern = lambda *a: _ffn_kernel(*a, n_f=n_f)
    return pl.pallas_call(
        kern,
        out_shape=jax.ShapeDtypeStruct((M, D), F32),
        grid=(M // tm, n_f),
        in_specs=[
            pl.BlockSpec((tm, D), lambda i, f: (i, 0)),
            pl.BlockSpec((1, 1, D), lambda i, f: ((i // per_b) * 4 + 1, 0, 0)),
            pl.BlockSpec((1, 1, D), lambda i, f: ((i // per_b) * 4 + 2, 0, 0)),
            pl.BlockSpec((1, 1, D), lambda i, f: ((i // per_b) * 4 + 3, 0, 0)),
            pl.BlockSpec((1, D), lambda i, f: (0, 0)),
            pl.BlockSpec((1, D), lambda i, f: (0, 0)),
            pl.BlockSpec((D, tf), lambda i, f: (0, f)),
            pl.BlockSpec((D, tf), lambda i, f: (0, f)),
            pl.BlockSpec((tf, D), lambda i, f: (f, 0)),
        ],
        out_specs=pl.BlockSpec((tm, D), lambda i, f: (i, 0)),
        scratch_shapes=[pltpu.VMEM((tm, D), BF16)],
        compiler_params=pltpu.CompilerParams(
            dimension_semantics=("arbitrary", "arbitrary"), vmem_limit_bytes=60 * MIB),
        name="ffn",
    )(x1, mod3, mod3, mod3, norm_g, final_g, w_gate, w_up, w_down)


def kernel(x, c, w_ada, b_ada, norm1_g, w_in, na_rpb, na_out_g, da_lambda, da_subln_g, t5_table,
           w_out, norm2_g, w_gate, w_up, w_down, final_g):
    B, S, D = x.shape
    na_heads = na_rpb.shape[1]
    da_heads = t5_table.shape[1]
    na_width = na_heads * HEAD_DIM
    assert w_ada.shape[0] == 1, "single layer"

    c_pad = jnp.zeros((8, D), F32).at[:B].set(c)
    b_ada2d = b_ada[0].reshape(1, -1)
    mod_a = _ada_mod(c_pad, w_ada[0], b_ada2d, 2 * D)[:B].reshape(B * 2, 1, D)
    x2d = x.reshape(B * S, D)

    proj_hm = _in_proj(x2d, mod_a, norm1_g, w_in[0], seq=S,
                       na_width=na_width, da_q_off=3 * na_width)
    na_o = _na_attn(proj_hm, _na_table(na_rpb[0]), batch=B, seq=S, heads=na_heads,
                    q_blk0=0, k_blk0=na_heads, v_blk0=2 * na_heads)
    da_o, wg_bf, wu_bf, wd_bf, mod_b = _da_attn(
        proj_hm, _da_table(t5_table), da_lambda[0], da_subln_g, w_gate[0], w_up[0], w_down[0],
        c_pad, w_ada[0], b_ada2d, 2 * D,
        batch=B, seq=S, heads=da_heads, q_blk0=3 * na_heads, k_blk0=3 * na_heads + da_heads,
        v_blk0=3 * na_heads + 2 * da_heads)
    mod_b = mod_b[:B].reshape(B * 4, 1, D)
    x1 = _out_proj(na_o, da_o, na_out_g, w_out[0], x2d, mod_b, seq=S)
    out = _ffn(x1, mod_b, norm2_g, final_g.reshape(1, D), wg_bf, wu_bf, wd_bf, seq=S)
    return out.reshape(B, S, D)
```

```python
import math

import jax
import jax.numpy as jnp
from jax import lax
from jax.experimental import pallas as pl
from jax.experimental.pallas import tpu as pltpu

F32 = jnp.float32
BF16 = jnp.bfloat16

EPS = 1e-6
NEG = -1e30

GRID_W = 64
HEAD_DIM = 128
DA_QK_DIM = 64
NA_WIN_ROWS = 8
NA_WIN_COLS = 16
T5_BUCKETS = 32
T5_MAX_DIST = 128
LAM_INIT = 0.8 - 0.6 * math.exp(-0.3 * 0)
LOG2E = math.log2(math.e)

MIB = 1024 * 1024

SUBLANES = 8
LANES = 128
BF16_ROW_TILE = 16

NA_QROWS = 8
NA_KROWS = 16
NA_TQ = NA_QROWS * GRID_W
NA_TK = NA_KROWS * GRID_W

DA_TQ = 256
DA_VT_ROWS = HEAD_DIM + BF16_ROW_TILE
DA_S_BUFS = 3

FFN_ROW_CHUNK = 512
NORM_ROW_CHUNK = 128
INPROJ_ROW_CHUNK = 256
OUTPROJ_ROW_CHUNK = 256
WCAST_ROW_CHUNK = 128


def _dot(a, b):
    return jnp.dot(a, b, preferred_element_type=F32)


def _dot_nt(a, b):
    return lax.dot_general(a, b, (((1,), (1,)), ((), ())), preferred_element_type=F32)


def _sigmoid(x):
    return 1.0 / (1.0 + jnp.exp(-x))


def _ada_columns(c_ref, w_ref, b_ref):
    c = c_ref[...]
    cs = c * _sigmoid(c)
    return _dot(cs.astype(BF16), w_ref[...].astype(BF16)) + b_ref[...]


def _ada_kernel(c_ref, w_ref, b_ref, o_ref):
    o_ref[...] = _ada_columns(c_ref, w_ref, b_ref)


def _ada_mod(c_pad, w_ada, b_ada, n_cols):
    rows, D = c_pad.shape
    tn = 1024
    return pl.pallas_call(
        _ada_kernel,
        out_shape=jax.ShapeDtypeStruct((rows, n_cols), F32),
        grid=(n_cols // tn,),
        in_specs=[
            pl.BlockSpec((rows, D), lambda j: (0, 0)),
            pl.BlockSpec((D, tn), lambda j: (0, j)),
            pl.BlockSpec((1, tn), lambda j: (0, j)),
        ],
        out_specs=pl.BlockSpec((rows, tn), lambda j: (0, j)),
        compiler_params=pltpu.CompilerParams(
            dimension_semantics=("arbitrary",), vmem_limit_bytes=40 * MIB),
        name="ada_mod",
    )(c_pad, w_ada, b_ada)


def _inproj_kernel(x_ref, sh_ref, sc_ref, g_ref, w_ref, o_ref, h_ref, *, na_q_blk, da_q_blk,
                   na_scale, da_scale):
    j = pl.program_id(1)
    scale = jnp.where(j == na_q_blk, na_scale, jnp.where(j == da_q_blk, da_scale, 1.0))

    def project(rows):
        acc = _dot(h_ref[rows, :], w_ref[...].astype(BF16)) * scale
        for k in range(o_ref.shape[0]):
            o_ref[k, rows, :] = acc[:, k * HEAD_DIM:(k + 1) * HEAD_DIM].astype(BF16)

    @pl.when(j == 0)
    def _():
        gain = g_ref[...] * (1.0 + sc_ref[0])
        tm = x_ref.shape[0]
        for r in range(tm // INPROJ_ROW_CHUNK):
            for sub in range(INPROJ_ROW_CHUNK // NORM_ROW_CHUNK):
                start = r * INPROJ_ROW_CHUNK + sub * NORM_ROW_CHUNK
                nrows = slice(start, start + NORM_ROW_CHUNK)
                x = x_ref[nrows, :]
                ms = jnp.mean(x * x, axis=-1, keepdims=True)
                h_ref[nrows, :] = (x * lax.rsqrt(ms + EPS) * gain + sh_ref[0]).astype(BF16)
            project(slice(r * INPROJ_ROW_CHUNK, (r + 1) * INPROJ_ROW_CHUNK))

    @pl.when(j > 0)
    def _():
        project(slice(None))


def _in_proj(x2d, mod3, norm_g, w_in, *, seq, na_width, da_q_off):
    M, D = x2d.shape
    N = w_in.shape[1]
    tm, tn = 1024, 1024
    per_b = seq // tm
    kern = lambda *a: _inproj_kernel(
        *a, na_q_blk=0, da_q_blk=da_q_off // tn,
        na_scale=HEAD_DIM ** -0.5 * LOG2E, da_scale=DA_QK_DIM ** -0.5 * LOG2E)
    assert na_width == tn
    return pl.pallas_call(
        kern,
        out_shape=jax.ShapeDtypeStruct((N // HEAD_DIM, M, HEAD_DIM), BF16),
        grid=(M // tm, N // tn),
        in_specs=[
            pl.BlockSpec((tm, D), lambda i, j: (i, 0)),
            pl.BlockSpec((1, 1, D), lambda i, j: ((i // per_b) * 2 + 0, 0, 0)),
            pl.BlockSpec((1, 1, D), lambda i, j: ((i // per_b) * 2 + 1, 0, 0)),
            pl.BlockSpec((1, D), lambda i, j: (0, 0)),
            pl.BlockSpec((D, tn), lambda i, j: (0, j)),
        ],
        out_specs=pl.BlockSpec((tn // HEAD_DIM, tm, HEAD_DIM), lambda i, j: (j, i, 0)),
        scratch_shapes=[pltpu.VMEM((tm, D), BF16)],
        compiler_params=pltpu.CompilerParams(
            dimension_semantics=("arbitrary", "arbitrary"), vmem_limit_bytes=48 * MIB),
        name="in_proj",
    )(x2d, mod3, mod3, norm_g, w_in)


NA_REL_ROWS = 2 * NA_WIN_ROWS - 1
NA_REL_COLS = 2 * NA_WIN_COLS - 1
NA_PAIR_TILES = 2 * NA_WIN_ROWS


def _na_table_kernel(rpb_ref, o_ref):
    h = pl.program_id(0)
    shape = (GRID_W, 2 * GRID_W)
    qc = lax.broadcasted_iota(jnp.int32, shape, 0)
    lane = lax.broadcasted_iota(jnp.int32, shape, 1)
    upper = lane >= GRID_W
    kc = jnp.where(upper, lane - GRID_W, lane)
    c0 = jnp.clip(qc - NA_WIN_COLS // 2, 0, GRID_W - NA_WIN_COLS)
    in_win = (kc >= c0) & (kc < c0 + NA_WIN_COLS)
    lane_v = lax.broadcasted_iota(jnp.int32, (SUBLANES, 2 * GRID_W), 1)
    base = h * (NA_REL_ROWS * NA_REL_COLS)

    def rotated_rows(rel_row, center):
        if not 0 <= rel_row < NA_REL_ROWS:
            return jnp.full(shape, NEG, F32)
        vec = jnp.zeros(lane_v.shape, F32)
        for d in range(NA_REL_COLS):
            at = (center + d - (NA_WIN_COLS - 1)) % (2 * GRID_W)
            vec = jnp.where(lane_v == at, rpb_ref[base + rel_row * NA_REL_COLS + d] * LOG2E, vec)
        rows = jnp.concatenate([vec] * (GRID_W // SUBLANES), axis=0)
        return pltpu.roll(rows, 0, 1, stride=1, stride_axis=0)

    for e in range(NA_PAIR_TILES):
        tile = jnp.where(upper, rotated_rows(e, GRID_W), rotated_rows(e - 1, 0))
        o_ref[e] = jnp.where(in_win, tile, NEG)


def _na_table(rpb):
    heads = rpb.shape[0]
    assert rpb.shape[1:] == (NA_REL_ROWS, NA_REL_COLS)
    return pl.pallas_call(
        _na_table_kernel,
        out_shape=jax.ShapeDtypeStruct((heads * NA_PAIR_TILES, GRID_W, 2 * GRID_W), F32),
        grid=(heads,),
        in_specs=[pl.BlockSpec(memory_space=pltpu.SMEM)],
        out_specs=pl.BlockSpec((NA_PAIR_TILES, GRID_W, 2 * GRID_W), lambda h: (h, 0, 0)),
        compiler_params=pltpu.CompilerParams(dimension_semantics=("arbitrary",)),
        name="na_table",
    )(rpb.reshape(-1))


def _na_window_start_row(blk, rows):
    return min(max(blk * NA_QROWS - (NA_KROWS - NA_QROWS) // 2, 0), rows - NA_KROWS)


def _na_block(q, kw, vw, tp_ref, blk, rows):
    lane = lax.broadcasted_iota(jnp.int32, (GRID_W, 2 * GRID_W), 1)
    kr0 = _na_window_start_row(blk, rows)
    s = _dot_nt(q, kw)
    p_rows, inv_l = [], []
    for i in range(NA_QROWS):
        qr = blk * NA_QROWS + i
        rs = min(max(qr - NA_WIN_ROWS // 2, 0), rows - NA_WIN_ROWS)
        tiles = {}
        for m in range(NA_KROWS // 2):
            ka = kr0 + 2 * m
            va = rs <= ka < rs + NA_WIN_ROWS
            vb = rs <= ka + 1 < rs + NA_WIN_ROWS
            if not (va or vb):
                continue
            e = ka - qr + NA_WIN_ROWS
            t = s[i * GRID_W:(i + 1) * GRID_W, m * 2 * GRID_W:(m + 1) * 2 * GRID_W] + tp_ref[e]
            if not vb:
                t = jnp.where(lane < GRID_W, t, NEG)
            elif not va:
                t = jnp.where(lane >= GRID_W, t, NEG)
            tiles[m] = t
        ts = list(tiles.values())
        mx = ts[0]
        for t in ts[1:]:
            mx = jnp.maximum(mx, t)
        mx = jnp.max(mx, axis=-1, keepdims=True)
        ps = {m: jnp.exp2(t - mx) for m, t in tiles.items()}
        tot = None
        for pt in ps.values():
            tot = pt if tot is None else tot + pt
        inv_l.append(1.0 / jnp.sum(tot, axis=-1, keepdims=True))
        zero = jnp.zeros((GRID_W, 2 * GRID_W), BF16)
        p_rows.append(jnp.concatenate(
            [ps[m].astype(BF16) if m in ps else zero for m in range(NA_KROWS // 2)], axis=1))
    p = jnp.concatenate(p_rows, axis=0)
    return _dot(p, vw) * jnp.concatenate(inv_l, axis=0)


def _na_kernel(q_ref, k_ref, v_ref, tp_ref, o_ref, *, rows):
    for blk in range(rows // NA_QROWS):
        start = _na_window_start_row(blk, rows) * GRID_W
        o = _na_block(q_ref[blk * NA_TQ:(blk + 1) * NA_TQ, :], k_ref[start:start + NA_TK, :],
                      v_ref[start:start + NA_TK, :], tp_ref, blk, rows)
        o_ref[blk * NA_TQ:(blk + 1) * NA_TQ, :] = o.astype(BF16)


def _na_attn(proj_hm, tp, *, batch, seq, heads, q_blk0, k_blk0, v_blk0):
    rows = seq // GRID_W
    kern = lambda *a: _na_kernel(*a, rows=rows)
    head_blk = lambda blk0: pl.BlockSpec((None, seq, HEAD_DIM), lambda b, h: (blk0 + h, b, 0))
    return pl.pallas_call(
        kern,
        out_shape=jax.ShapeDtypeStruct((batch * seq, heads * HEAD_DIM), BF16),
        grid=(batch, heads),
        in_specs=[
            head_blk(q_blk0), head_blk(k_blk0), head_blk(v_blk0),
            pl.BlockSpec((NA_PAIR_TILES, GRID_W, 2 * GRID_W), lambda b, h: (h, 0, 0)),
        ],
        out_specs=pl.BlockSpec((seq, HEAD_DIM), lambda b, h: (b, h)),
        compiler_params=pltpu.CompilerParams(
            dimension_semantics=("arbitrary", "arbitrary"), vmem_limit_bytes=48 * MIB),
        name="na_attn",
    )(proj_hm, proj_hm, proj_hm, tp)


def _t5_bucket(rel):
    nb = T5_BUCKETS // 2
    ret = jnp.where(rel > 0, nb, 0)
    n = jnp.abs(rel)
    max_exact = nb // 2
    nf = jnp.maximum(n, 1).astype(jnp.float32)
    large = max_exact + (jnp.log(nf / max_exact) / math.log(T5_MAX_DIST / max_exact)
                         * (nb - max_exact)).astype(jnp.int32)
    large = jnp.minimum(large, nb - 1)
    return ret + jnp.where(n < max_exact, n, large)


DA_SLABS = 5


DA_REL_SPAN = 4 * DA_TQ


def _da_table_kernel(bias_ref, o_ref):
    vec = bias_ref[...] * LOG2E
    rows = jnp.broadcast_to(vec, (DA_TQ, DA_REL_SPAN))
    rolled = pltpu.roll(rows, 1, 1, stride=1, stride_axis=0)
    for dj in (-1, 0, 1):
        lanes = 2 * DA_TQ - dj * DA_TQ
        o_ref[dj + 2] = rolled[:, lanes:lanes + DA_TQ]
    shape = (DA_TQ, DA_TQ)
    o_ref[0] = jnp.broadcast_to(vec[:, DA_REL_SPAN - 1:], shape)
    o_ref[DA_SLABS - 1] = jnp.broadcast_to(vec[:, :1], shape)


def _da_table(t5_table):
    heads = t5_table.shape[1]
    assert DA_TQ >= T5_MAX_DIST
    rel = 2 * DA_TQ - 1 - jnp.arange(DA_REL_SPAN)
    bias = t5_table.astype(F32)[_t5_bucket(rel)].T.reshape(heads, 1, DA_REL_SPAN)
    return pl.pallas_call(
        _da_table_kernel,
        out_shape=jax.ShapeDtypeStruct((heads * DA_SLABS, DA_TQ, DA_TQ), F32),
        grid=(heads,),
        in_specs=[pl.BlockSpec((None, 1, DA_REL_SPAN), lambda h: (h, 0, 0))],
        out_specs=pl.BlockSpec((DA_SLABS, DA_TQ, DA_TQ), lambda h: (h, 0, 0)),
        compiler_params=pltpu.CompilerParams(dimension_semantics=("arbitrary",)),
        name="da_table",
    )(bias)


def _da_kernel(q_ref, k_ref, v_ref, tb_ref, lam_ref, g_ref, wg_ref, wu_ref, wd_ref,
               c_ref, wada_ref, bada_ref,
               o_ref, wg_bf_ref, wu_bf_ref, wd_bf_ref, mod_ref, vt_ref, s_ref, p_ref, *, seq):
    n_kt = seq // DA_TQ

    wg_bf_ref[...] = wg_ref[...].astype(BF16)
    wu_bf_ref[...] = wu_ref[...].astype(BF16)
    wd_bf_ref[...] = wd_ref[...].astype(BF16)
    mod_ref[...] = _ada_columns(c_ref, wada_ref, bada_ref)

    vt_ref[:HEAD_DIM, :] = v_ref[...].astype(F32).T.astype(BF16)
    row = lax.broadcasted_iota(jnp.int32, (DA_VT_ROWS - HEAD_DIM, seq), 0)
    vt_ref[HEAD_DIM:, :] = jnp.where(row == 0, 1.0, 0.0).astype(BF16)

    lp = lam_ref[...]
    t1 = jnp.sum(lp[0:1] * lp[1:2], axis=-1, keepdims=True)
    t2 = jnp.sum(lp[2:3] * lp[3:4], axis=-1, keepdims=True)
    lam = jnp.exp(t1) - jnp.exp(t2) + LAM_INIT

    c_left = tb_ref[0, 0:1, 0:1]
    c_right = tb_ref[DA_SLABS - 1, 0:1, 0:1]
    lane = lax.broadcasted_iota(jnp.int32, (DA_TQ, HEAD_DIM), 1)

    def key_rows(j):
        return slice(j * DA_TQ, (j + 1) * DA_TQ)

    def col_max(parts):
        acc = parts[0]
        for part in parts[1:]:
            acc = jnp.maximum(acc, part)
        return jnp.max(acc, axis=0, keepdims=True)

    def qk_scores(t):
        q = q_ref[t * DA_TQ:(t + 1) * DA_TQ, :]
        zero = jnp.zeros_like(q)
        q12 = jnp.concatenate(
            [jnp.where(lane < DA_QK_DIM, q, zero), jnp.where(lane >= DA_QK_DIM, q, zero)], axis=0)
        s_ref[t % DA_S_BUFS] = _dot_nt(k_ref[...], q12)

    for t0 in range(DA_S_BUFS - 1):
        qk_scores(t0)
    for t in range(n_kt):
        buf = t % 2
        if t + DA_S_BUFS - 1 < n_kt:
            qk_scores(t + DA_S_BUFS - 1)

        def scores(j):
            s = s_ref[t % DA_S_BUFS, key_rows(j), :]
            if abs(j - t) <= 1:
                bias = tb_ref[j - t + 2]
                s = s + jnp.concatenate([bias, bias], axis=1)
            return s

        near = [j for j in range(n_kt) if abs(j - t) <= 1]
        left = [j for j in range(n_kt) if j < t - 1]
        right = [j for j in range(n_kt) if j > t + 1]
        m = col_max([scores(j) for j in near])
        if left:
            m = jnp.maximum(m, col_max([scores(j) for j in left]) + c_left)
        if right:
            m = jnp.maximum(m, col_max([scores(j) for j in right]) + c_right)

        for j in range(n_kt):
            shift = m if abs(j - t) <= 1 else (m - c_left if j < t else m - c_right)
            p_ref[buf, key_rows(j), :] = jnp.exp2(scores(j) - shift).astype(BF16)
        oe1 = _dot(vt_ref[...], p_ref[buf, :, :DA_TQ])
        oe2 = _dot(vt_ref[...], p_ref[buf, :, DA_TQ:])
        o1, l1 = oe1[:HEAD_DIM], oe1[HEAD_DIM:HEAD_DIM + 1]
        o2, l2 = oe2[:HEAD_DIM], oe2[HEAD_DIM:HEAD_DIM + 1]
        o = o1 * (1.0 / l1) - o2 * (lam / l2)
        ms = jnp.mean(o * o, axis=0, keepdims=True)
        y = ((o * lax.rsqrt(ms + EPS)).T * g_ref[...]) * (1.0 - LAM_INIT)
        o_ref[t * DA_TQ:(t + 1) * DA_TQ, :] = y.astype(BF16)


def _da_attn(proj_hm, tb, da_lambda, subln_g, w_gate, w_up, w_down, c_pad, w_ada, b_ada, ada_col0,
             *, batch, seq, heads, q_blk0, k_blk0, v_blk0):
    kern = lambda *a: _da_kernel(*a, seq=seq)
    head_blk = lambda blk0: pl.BlockSpec((None, seq, HEAD_DIM), lambda b, h: (blk0 + h, b, 0))
    n_steps = batch * heads

    def row_slab(w):
        assert w.shape[0] % (BF16_ROW_TILE * n_steps) == 0
        return pl.BlockSpec((w.shape[0] // n_steps, w.shape[1]), lambda b, h: (b * heads + h, 0))

    weights = (w_gate, w_up, w_down)
    D = c_pad.shape[1]
    n_ada = w_ada.shape[1] - ada_col0
    ada_tn = n_ada // n_steps
    assert ada_tn % LANES == 0 and ada_col0 % ada_tn == 0
    ada_blk0 = ada_col0 // ada_tn
    return pl.pallas_call(
        kern,
        out_shape=(jax.ShapeDtypeStruct((batch * seq, heads * HEAD_DIM), BF16),
                   *[jax.ShapeDtypeStruct(w.shape, BF16) for w in weights],
                   jax.ShapeDtypeStruct((c_pad.shape[0], n_ada), F32)),
        grid=(batch, heads),
        in_specs=[
            head_blk(q_blk0), head_blk(k_blk0), head_blk(v_blk0),
            pl.BlockSpec((DA_SLABS, DA_TQ, DA_TQ), lambda b, h: (h, 0, 0)),
            pl.BlockSpec(da_lambda.shape, lambda b, h: (0, 0)),
            pl.BlockSpec((1, HEAD_DIM), lambda b, h: (0, 0)),
            *[row_slab(w) for w in weights],
            pl.BlockSpec(c_pad.shape, lambda b, h: (0, 0)),
            pl.BlockSpec((D, ada_tn), lambda b, h: (0, ada_blk0 + b * heads + h)),
            pl.BlockSpec((1, ada_tn), lambda b, h: (0, ada_blk0 + b * heads + h)),
        ],
        out_specs=(pl.BlockSpec((seq, HEAD_DIM), lambda b, h: (b, h)),
                   *[row_slab(w) for w in weights],
                   pl.BlockSpec((c_pad.shape[0], ada_tn), lambda b, h: (0, b * heads + h))),
        scratch_shapes=[
            pltpu.VMEM((DA_VT_ROWS, seq), BF16),
            pltpu.VMEM((DA_S_BUFS, seq, 2 * DA_TQ), F32),
            pltpu.VMEM((2, seq, 2 * DA_TQ), BF16),
        ],
        compiler_params=pltpu.CompilerParams(
            dimension_semantics=("arbitrary", "arbitrary"), vmem_limit_bytes=48 * MIB),
        name="da_attn",
    )(proj_hm, proj_hm, proj_hm, tb, da_lambda, subln_g, *weights,
      c_pad, w_ada, b_ada)


def _outproj_kernel(na_ref, da_ref, nag_ref, w_ref, x_ref, g1_ref, o_ref, wbf_ref):
    kh = na_ref.shape[1]

    @pl.when(pl.program_id(0) == 0)
    def _():
        def body(r, carry):
            rows = pl.ds(pl.multiple_of(r * WCAST_ROW_CHUNK, WCAST_ROW_CHUNK), WCAST_ROW_CHUNK)
            wbf_ref[rows, :] = w_ref[rows, :].astype(BF16)
            return carry
        lax.fori_loop(0, w_ref.shape[0] // WCAST_ROW_CHUNK, body, 0)

    gain = nag_ref[...]
    for r in range(na_ref.shape[0] // OUTPROJ_ROW_CHUNK):
        rows = slice(r * OUTPROJ_ROW_CHUNK, (r + 1) * OUTPROJ_ROW_CHUNK)
        o = na_ref[rows, :].astype(F32)
        ms = jnp.mean(o * o, axis=-1, keepdims=True)
        na_n = (o * lax.rsqrt(ms + EPS) * gain).astype(BF16)
        mix = _dot(na_n, wbf_ref[:kh, :]) + _dot(da_ref[rows, :], wbf_ref[kh:, :])
        o_ref[rows, :] = x_ref[rows, :] + g1_ref[0] * mix


def _out_proj(na_o, da_o, na_out_g, w_out, x2d, mod3, *, seq):
    M, D = x2d.shape
    Kh = na_o.shape[1]
    tm = 512
    per_b = seq // tm
    return pl.pallas_call(
        _outproj_kernel,
        out_shape=jax.ShapeDtypeStruct((M, D), F32),
        grid=(M // tm,),
        in_specs=[
            pl.BlockSpec((tm, Kh), lambda i: (i, 0)),
            pl.BlockSpec((tm, Kh), lambda i: (i, 0)),
            pl.BlockSpec((1, Kh), lambda i: (0, 0)),
            pl.BlockSpec(w_out.shape, lambda i: (0, 0), pipeline_mode=pl.Buffered(1)),
            pl.BlockSpec((tm, D), lambda i: (i, 0)),
            pl.BlockSpec((1, 1, D), lambda i: ((i // per_b) * 4 + 0, 0, 0)),
        ],
        out_specs=pl.BlockSpec((tm, D), lambda i: (i, 0)),
        scratch_shapes=[pltpu.VMEM(w_out.shape, BF16)],
        compiler_params=pltpu.CompilerParams(
            dimension_semantics=("arbitrary",), vmem_limit_bytes=56 * MIB),
        name="out_proj",
    )(na_o, da_o, na_out_g, w_out, x2d, mod3)


def _ffn_kernel(x_ref, sh_ref, sc_ref, g2_ref, ng_ref, fg_ref, wg_ref, wu_ref, wd_ref, o_ref, h_ref,
                *, n_f):
    f = pl.program_id(1)
    tm = x_ref.shape[0]
    assert n_f >= 2

    def step(first, last):
        chunk = FFN_ROW_CHUNK if (first or last) else tm
        for r in range(tm // chunk):
            rows = slice(r * chunk, (r + 1) * chunk)
            if first:
                gain = ng_ref[...] * (1.0 + sc_ref[0])
                for sub in range(chunk // NORM_ROW_CHUNK):
                    nrows = slice(rows.start + sub * NORM_ROW_CHUNK,
                                  rows.start + (sub + 1) * NORM_ROW_CHUNK)
                    x = x_ref[nrows, :]
                    ms = jnp.mean(x * x, axis=-1, keepdims=True)
                    h_ref[nrows, :] = (x * lax.rsqrt(ms + EPS) * gain + sh_ref[0]).astype(BF16)
            h = h_ref[rows, :]
            g = _dot(h, wg_ref[...])
            u = _dot(h, wu_ref[...])
            a = (g * _sigmoid(g) * u).astype(BF16)
            part = _dot(a, wd_ref[...])
            acc = part if first else o_ref[rows, :] + part
            if last:
                x2 = x_ref[rows, :] + g2_ref[0] * acc
                ms = jnp.mean(x2 * x2, axis=-1, keepdims=True)
                acc = x2 * lax.rsqrt(ms + EPS) * fg_ref[...]
            o_ref[rows, :] = acc

    pl.when(f == 0)(lambda: step(True, False))
    pl.when(jnp.logical_and(f > 0, f < n_f - 1))(lambda: step(False, False))
    pl.when(f == n_f - 1)(lambda: step(False, True))


def _ffn(x1, mod3, norm_g, final_g, w_gate, w_up, w_down, *, seq):
    M, D = x1.shape
    F = w_gate.shape[1]
    tm, tf = 1024, 512
    per_b = seq // tm
    n_f = F // tf
    kern = lambda *a: _ffn_kernel(*a, n_f=n_f)
    return pl.pallas_call(
        kern,
        out_shape=jax.ShapeDtypeStruct((M, D), F32),
        grid=(M // tm, n_f),
        in_specs=[
            pl.BlockSpec((tm, D), lambda i, f: (i, 0)),
            pl.BlockSpec((1, 1, D), lambda i, f: ((i // per_b) * 4 + 1, 0, 0)),
            pl.BlockSpec((1, 1, D), lambda i, f: ((i // per_b) * 4 + 2, 0, 0)),
            pl.BlockSpec((1, 1, D), lambda i, f: ((i // per_b) * 4 + 3, 0, 0)),
            pl.BlockSpec((1, D), lambda i, f: (0, 0)),
            pl.BlockSpec((1, D), lambda i, f: (0, 0)),
            pl.BlockSpec((D, tf), lambda i, f: (0, f)),
            pl.BlockSpec((D, tf), lambda i, f: (0, f)),
            pl.BlockSpec((tf, D), lambda i, f: (f, 0)),
        ],
        out_specs=pl.BlockSpec((tm, D), lambda i, f: (i, 0)),
        scratch_shapes=[pltpu.VMEM((tm, D), BF16)],
        compiler_params=pltpu.CompilerParams(
            dimension_semantics=("arbitrary", "arbitrary"), vmem_limit_bytes=60 * MIB),
        name="ffn",
    )(x1, mod3, mod3, mod3, norm_g, final_g, w_gate, w_up, w_down)


def kernel(x, c, w_ada, b_ada, norm1_g, w_in, na_rpb, na_out_g, da_lambda, da_subln_g, t5_table,
           w_out, norm2_g, w_gate, w_up, w_down, final_g):
    B, S, D = x.shape
    na_heads = na_rpb.shape[1]
    da_heads = t5_table.shape[1]
    na_width = na_heads * HEAD_DIM
    assert w_ada.shape[0] == 1, "single layer"

    assert B <= SUBLANES
    c_pad = jnp.zeros((SUBLANES, D), F32).at[:B].set(c)
    b_ada2d = b_ada[0].reshape(1, -1)
    mod_a = _ada_mod(c_pad, w_ada[0], b_ada2d, 2 * D)[:B].reshape(B * 2, 1, D)
    x2d = x.reshape(B * S, D)

    proj_hm = _in_proj(x2d, mod_a, norm1_g, w_in[0], seq=S,
                       na_width=na_width, da_q_off=3 * na_width)
    na_o = _na_attn(proj_hm, _na_table(na_rpb[0]), batch=B, seq=S, heads=na_heads,
                    q_blk0=0, k_blk0=na_heads, v_blk0=2 * na_heads)
    da_o, wg_bf, wu_bf, wd_bf, mod_b = _da_attn(
        proj_hm, _da_table(t5_table), da_lambda[0], da_subln_g, w_gate[0], w_up[0], w_down[0],
        c_pad, w_ada[0], b_ada2d, 2 * D,
        batch=B, seq=S, heads=da_heads, q_blk0=3 * na_heads, k_blk0=3 * na_heads + da_heads,
        v_blk0=3 * na_heads + 2 * da_heads)
    mod_b = mod_b[:B].reshape(B * 4, 1, D)
    x1 = _out_proj(na_o, da_o, na_out_g, w_out[0], x2d, mod_b, seq=S)
    out = _ffn(x1, mod_b, norm2_g, final_g.reshape(1, D), wg_bf, wu_bf, wd_bf, seq=S)
    return out.reshape(B, S, D)
```

```python
import math

import jax
import jax.numpy as jnp
from jax import lax
from jax.experimental import pallas as pl
from jax.experimental.pallas import tpu as pltpu

F32 = jnp.float32
BF16 = jnp.bfloat16

EPS = 1e-6
NEG = -1e30

GRID_W = 64
HEAD_DIM = 128
DA_QK_DIM = 64
NA_WIN_ROWS = 8
NA_WIN_COLS = 16
T5_BUCKETS = 32
T5_MAX_DIST = 128
LAM_INIT = 0.8 - 0.6 * math.exp(-0.3 * 0)
LOG2E = math.log2(math.e)

MIB = 1024 * 1024

SUBLANES = 8
LANES = 128
BF16_ROW_TILE = 16

NA_QROWS = 8
NA_KROWS = 16
NA_TQ = NA_QROWS * GRID_W
NA_TK = NA_KROWS * GRID_W
NA_LOOKAHEAD = 1

DA_TQ = 256
DA_VT_ROWS = HEAD_DIM + BF16_ROW_TILE
DA_S_BUFS = 3

FFN_ROW_CHUNK = 512
NORM_ROW_CHUNK = 128
INPROJ_ROW_CHUNK = 256
OUTPROJ_ROW_CHUNK = 256
WCAST_ROW_CHUNK = 128


def _dot(a, b):
    return jnp.dot(a, b, preferred_element_type=F32)


def _dot_nt(a, b):
    return lax.dot_general(a, b, (((1,), (1,)), ((), ())), preferred_element_type=F32)


def _sigmoid(x):
    return 1.0 / (1.0 + jnp.exp(-x))


def _ada_columns(c_ref, w_ref, b_ref):
    c = c_ref[...]
    cs = c * _sigmoid(c)
    return _dot(cs.astype(BF16), w_ref[...].astype(BF16)) + b_ref[...]


def _ada_kernel(c_ref, w_ref, b_ref, o_ref):
    o_ref[...] = _ada_columns(c_ref, w_ref, b_ref)


def _ada_mod(c_pad, w_ada, b_ada, n_cols):
    rows, D = c_pad.shape
    tn = 1024
    return pl.pallas_call(
        _ada_kernel,
        out_shape=jax.ShapeDtypeStruct((rows, n_cols), F32),
        grid=(n_cols // tn,),
        in_specs=[
            pl.BlockSpec((rows, D), lambda j: (0, 0)),
            pl.BlockSpec((D, tn), lambda j: (0, j)),
            pl.BlockSpec((1, tn), lambda j: (0, j)),
        ],
        out_specs=pl.BlockSpec((rows, tn), lambda j: (0, j)),
        compiler_params=pltpu.CompilerParams(
            dimension_semantics=("arbitrary",), vmem_limit_bytes=40 * MIB),
        name="ada_mod",
    )(c_pad, w_ada, b_ada)


def _inproj_kernel(x_ref, sh_ref, sc_ref, g_ref, w_ref, o_ref, h_ref, *, na_q_blk, da_q_blk,
                   na_scale, da_scale):
    j = pl.program_id(1)
    scale = jnp.where(j == na_q_blk, na_scale, jnp.where(j == da_q_blk, da_scale, 1.0))

    def project(rows):
        acc = _dot(h_ref[rows, :], w_ref[...].astype(BF16)) * scale
        for k in range(o_ref.shape[0]):
            o_ref[k, rows, :] = acc[:, k * HEAD_DIM:(k + 1) * HEAD_DIM].astype(BF16)

    @pl.when(j == 0)
    def _():
        gain = g_ref[...] * (1.0 + sc_ref[0])
        tm = x_ref.shape[0]
        for r in range(tm // INPROJ_ROW_CHUNK):
            for sub in range(INPROJ_ROW_CHUNK // NORM_ROW_CHUNK):
                start = r * INPROJ_ROW_CHUNK + sub * NORM_ROW_CHUNK
                nrows = slice(start, start + NORM_ROW_CHUNK)
                x = x_ref[nrows, :]
                ms = jnp.mean(x * x, axis=-1, keepdims=True)
                h_ref[nrows, :] = (x * lax.rsqrt(ms + EPS) * gain + sh_ref[0]).astype(BF16)
            project(slice(r * INPROJ_ROW_CHUNK, (r + 1) * INPROJ_ROW_CHUNK))

    @pl.when(j > 0)
    def _():
        project(slice(None))


def _in_proj(x2d, mod3, norm_g, w_in, *, seq, na_width, da_q_off):
    M, D = x2d.shape
    N = w_in.shape[1]
    tm, tn = 1024, 1024
    per_b = seq // tm
    kern = lambda *a: _inproj_kernel(
        *a, na_q_blk=0, da_q_blk=da_q_off // tn,
        na_scale=HEAD_DIM ** -0.5 * LOG2E, da_scale=DA_QK_DIM ** -0.5 * LOG2E)
    assert na_width == tn
    return pl.pallas_call(
        kern,
        out_shape=jax.ShapeDtypeStruct((N // HEAD_DIM, M, HEAD_DIM), BF16),
        grid=(M // tm, N // tn),
        in_specs=[
            pl.BlockSpec((tm, D), lambda i, j: (i, 0)),
            pl.BlockSpec((1, 1, D), lambda i, j: ((i // per_b) * 2 + 0, 0, 0)),
            pl.BlockSpec((1, 1, D), lambda i, j: ((i // per_b) * 2 + 1, 0, 0)),
            pl.BlockSpec((1, D), lambda i, j: (0, 0)),
            pl.BlockSpec((D, tn), lambda i, j: (0, j)),
        ],
        out_specs=pl.BlockSpec((tn // HEAD_DIM, tm, HEAD_DIM), lambda i, j: (j, i, 0)),
        scratch_shapes=[pltpu.VMEM((tm, D), BF16)],
        compiler_params=pltpu.CompilerParams(
            dimension_semantics=("arbitrary", "arbitrary"), vmem_limit_bytes=48 * MIB),
        name="in_proj",
    )(x2d, mod3, mod3, norm_g, w_in)


NA_REL_ROWS = 2 * NA_WIN_ROWS - 1
NA_REL_COLS = 2 * NA_WIN_COLS - 1
NA_PAIR_TILES = 2 * NA_WIN_ROWS


def _na_table_kernel(rpb_ref, o_ref):
    h = pl.program_id(0)
    shape = (GRID_W, 2 * GRID_W)
    qc = lax.broadcasted_iota(jnp.int32, shape, 0)
    lane = lax.broadcasted_iota(jnp.int32, shape, 1)
    upper = lane >= GRID_W
    kc = jnp.where(upper, lane - GRID_W, lane)
    c0 = jnp.clip(qc - NA_WIN_COLS // 2, 0, GRID_W - NA_WIN_COLS)
    in_win = (kc >= c0) & (kc < c0 + NA_WIN_COLS)
    lane_v = lax.broadcasted_iota(jnp.int32, (SUBLANES, 2 * GRID_W), 1)
    base = h * (NA_REL_ROWS * NA_REL_COLS)

    def rotated_rows(rel_row, center):
        if not 0 <= rel_row < NA_REL_ROWS:
            return jnp.full(shape, NEG, F32)
        vec = jnp.zeros(lane_v.shape, F32)
        for d in range(NA_REL_COLS):
            at = (center + d - (NA_WIN_COLS - 1)) % (2 * GRID_W)
            vec = jnp.where(lane_v == at, rpb_ref[base + rel_row * NA_REL_COLS + d] * LOG2E, vec)
        rows = jnp.concatenate([vec] * (GRID_W // SUBLANES), axis=0)
        return pltpu.roll(rows, 0, 1, stride=1, stride_axis=0)

    for e in range(NA_PAIR_TILES):
        tile = jnp.where(upper, rotated_rows(e, GRID_W), rotated_rows(e - 1, 0))
        o_ref[e] = jnp.where(in_win, tile, NEG)


def _na_table(rpb):
    heads = rpb.shape[0]
    assert rpb.shape[1:] == (NA_REL_ROWS, NA_REL_COLS)
    return pl.pallas_call(
        _na_table_kernel,
        out_shape=jax.ShapeDtypeStruct((heads * NA_PAIR_TILES, GRID_W, 2 * GRID_W), F32),
        grid=(heads,),
        in_specs=[pl.BlockSpec(memory_space=pltpu.SMEM)],
        out_specs=pl.BlockSpec((NA_PAIR_TILES, GRID_W, 2 * GRID_W), lambda h: (h, 0, 0)),
        compiler_params=pltpu.CompilerParams(dimension_semantics=("arbitrary",)),
        name="na_table",
    )(rpb.reshape(-1))


def _na_window_start_row(blk, rows):
    return min(max(blk * NA_QROWS - (NA_KROWS - NA_QROWS) // 2, 0), rows - NA_KROWS)


def _na_block(s, vw, tp_ref, blk, rows):
    lane = lax.broadcasted_iota(jnp.int32, (GRID_W, 2 * GRID_W), 1)
    kr0 = _na_window_start_row(blk, rows)
    p_rows, inv_l = [], []
    for i in range(NA_QROWS):
        qr = blk * NA_QROWS + i
        rs = min(max(qr - NA_WIN_ROWS // 2, 0), rows - NA_WIN_ROWS)
        tiles = {}
        for m in range(NA_KROWS // 2):
            ka = kr0 + 2 * m
            va = rs <= ka < rs + NA_WIN_ROWS
            vb = rs <= ka + 1 < rs + NA_WIN_ROWS
            if not (va or vb):
                continue
            e = ka - qr + NA_WIN_ROWS
            t = s[i * GRID_W:(i + 1) * GRID_W, m * 2 * GRID_W:(m + 1) * 2 * GRID_W] + tp_ref[e]
            if not vb:
                t = jnp.where(lane < GRID_W, t, NEG)
            elif not va:
                t = jnp.where(lane >= GRID_W, t, NEG)
            tiles[m] = t
        ts = list(tiles.values())
        mx = ts[0]
        for t in ts[1:]:
            mx = jnp.maximum(mx, t)
        mx = jnp.max(mx, axis=-1, keepdims=True)
        ps = {m: jnp.exp2(t - mx) for m, t in tiles.items()}
        tot = None
        for pt in ps.values():
            tot = pt if tot is None else tot + pt
        inv_l.append(1.0 / jnp.sum(tot, axis=-1, keepdims=True))
        zero = jnp.zeros((GRID_W, 2 * GRID_W), BF16)
        p_rows.append(jnp.concatenate(
            [ps[m].astype(BF16) if m in ps else zero for m in range(NA_KROWS // 2)], axis=1))
    p = jnp.concatenate(p_rows, axis=0)
    return _dot(p, vw) * jnp.concatenate(inv_l, axis=0)


def _na_kernel(q_ref, k_ref, v_ref, tp_ref, o_ref, *, rows):
    n_blk = rows // NA_QROWS

    def window(blk):
        start = _na_window_start_row(blk, rows) * GRID_W
        return slice(start, start + NA_TK)

    def scores(blk):
        return _dot_nt(q_ref[blk * NA_TQ:(blk + 1) * NA_TQ, :], k_ref[window(blk), :])

    pending = [scores(blk) for blk in range(min(NA_LOOKAHEAD, n_blk))]
    for blk in range(n_blk):
        if blk + NA_LOOKAHEAD < n_blk:
            pending.append(scores(blk + NA_LOOKAHEAD))
        o = _na_block(pending.pop(0), v_ref[window(blk), :], tp_ref, blk, rows)
        o_ref[blk * NA_TQ:(blk + 1) * NA_TQ, :] = o.astype(BF16)


def _na_attn(proj_hm, tp, *, batch, seq, heads, q_blk0, k_blk0, v_blk0):
    rows = seq // GRID_W
    kern = lambda *a: _na_kernel(*a, rows=rows)
    head_blk = lambda blk0: pl.BlockSpec((None, seq, HEAD_DIM), lambda b, h: (blk0 + h, b, 0))
    return pl.pallas_call(
        kern,
        out_shape=jax.ShapeDtypeStruct((batch * seq, heads * HEAD_DIM), BF16),
        grid=(batch, heads),
        in_specs=[
            head_blk(q_blk0), head_blk(k_blk0), head_blk(v_blk0),
            pl.BlockSpec((NA_PAIR_TILES, GRID_W, 2 * GRID_W), lambda b, h: (h, 0, 0)),
        ],
        out_specs=pl.BlockSpec((seq, HEAD_DIM), lambda b, h: (b, h)),
        compiler_params=pltpu.CompilerParams(
            dimension_semantics=("arbitrary", "arbitrary"), vmem_limit_bytes=48 * MIB),
        name="na_attn",
    )(proj_hm, proj_hm, proj_hm, tp)


def _t5_bucket(rel):
    nb = T5_BUCKETS // 2
    ret = jnp.where(rel > 0, nb, 0)
    n = jnp.abs(rel)
    max_exact = nb // 2
    nf = jnp.maximum(n, 1).astype(jnp.float32)
    large = max_exact + (jnp.log(nf / max_exact) / math.log(T5_MAX_DIST / max_exact)
                         * (nb - max_exact)).astype(jnp.int32)
    large = jnp.minimum(large, nb - 1)
    return ret + jnp.where(n < max_exact, n, large)


DA_SLABS = 5


DA_REL_SPAN = 4 * DA_TQ


def _da_table_kernel(bias_ref, o_ref):
    vec = bias_ref[...] * LOG2E
    rows = jnp.broadcast_to(vec, (DA_TQ, DA_REL_SPAN))
    rolled = pltpu.roll(rows, 1, 1, stride=1, stride_axis=0)
    for dj in (-1, 0, 1):
        lanes = 2 * DA_TQ - dj * DA_TQ
        o_ref[dj + 2] = rolled[:, lanes:lanes + DA_TQ]
    shape = (DA_TQ, DA_TQ)
    o_ref[0] = jnp.broadcast_to(vec[:, DA_REL_SPAN - 1:], shape)
    o_ref[DA_SLABS - 1] = jnp.broadcast_to(vec[:, :1], shape)


def _da_table(t5_table):
    heads = t5_table.shape[1]
    assert DA_TQ >= T5_MAX_DIST
    rel = 2 * DA_TQ - 1 - jnp.arange(DA_REL_SPAN)
    bias = t5_table.astype(F32)[_t5_bucket(rel)].T.reshape(heads, 1, DA_REL_SPAN)
    return pl.pallas_call(
        _da_table_kernel,
        out_shape=jax.ShapeDtypeStruct((heads * DA_SLABS, DA_TQ, DA_TQ), F32),
        grid=(heads,),
        in_specs=[pl.BlockSpec((None, 1, DA_REL_SPAN), lambda h: (h, 0, 0))],
        out_specs=pl.BlockSpec((DA_SLABS, DA_TQ, DA_TQ), lambda h: (h, 0, 0)),
        compiler_params=pltpu.CompilerParams(dimension_semantics=("arbitrary",)),
        name="da_table",
    )(bias)


def _da_kernel(q_ref, k_ref, v_ref, tb_ref, lam_ref, g_ref, wg_ref, wu_ref, wd_ref,
               c_ref, wada_ref, bada_ref,
               o_ref, wg_bf_ref, wu_bf_ref, wd_bf_ref, mod_ref, vt_ref, s_ref, p_ref, *, seq):
    n_kt = seq // DA_TQ

    wg_bf_ref[...] = wg_ref[...].astype(BF16)
    wu_bf_ref[...] = wu_ref[...].astype(BF16)
    wd_bf_ref[...] = wd_ref[...].astype(BF16)
    mod_ref[...] = _ada_columns(c_ref, wada_ref, bada_ref)

    vt_ref[:HEAD_DIM, :] = v_ref[...].astype(F32).T.astype(BF16)
    row = lax.broadcasted_iota(jnp.int32, (DA_VT_ROWS - HEAD_DIM, seq), 0)
    vt_ref[HEAD_DIM:, :] = jnp.where(row == 0, 1.0, 0.0).astype(BF16)

    lp = lam_ref[...]
    t1 = jnp.sum(lp[0:1] * lp[1:2], axis=-1, keepdims=True)
    t2 = jnp.sum(lp[2:3] * lp[3:4], axis=-1, keepdims=True)
    lam = jnp.exp(t1) - jnp.exp(t2) + LAM_INIT

    c_left = tb_ref[0, 0:1, 0:1]
    c_right = tb_ref[DA_SLABS - 1, 0:1, 0:1]
    lane = lax.broadcasted_iota(jnp.int32, (DA_TQ, HEAD_DIM), 1)

    def key_rows(j):
        return slice(j * DA_TQ, (j + 1) * DA_TQ)

    def col_max(parts):
        acc = parts[0]
        for part in parts[1:]:
            acc = jnp.maximum(acc, part)
        return jnp.max(acc, axis=0, keepdims=True)

    def qk_scores(t):
        q = q_ref[t * DA_TQ:(t + 1) * DA_TQ, :]
        zero = jnp.zeros_like(q)
        q12 = jnp.concatenate(
            [jnp.where(lane < DA_QK_DIM, q, zero), jnp.where(lane >= DA_QK_DIM, q, zero)], axis=0)
        s_ref[t % DA_S_BUFS] = _dot_nt(k_ref[...], q12)

    for t0 in range(DA_S_BUFS - 1):
        qk_scores(t0)
    for t in range(n_kt):
        buf = t % 2
        if t + DA_S_BUFS - 1 < n_kt:
            qk_scores(t + DA_S_BUFS - 1)

        def scores(j):
            s = s_ref[t % DA_S_BUFS, key_rows(j), :]
            if abs(j - t) <= 1:
                bias = tb_ref[j - t + 2]
                s = s + jnp.concatenate([bias, bias], axis=1)
            return s

        near = [j for j in range(n_kt) if abs(j - t) <= 1]
        left = [j for j in range(n_kt) if j < t - 1]
        right = [j for j in range(n_kt) if j > t + 1]
        m = col_max([scores(j) for j in near])
        if left:
            m = jnp.maximum(m, col_max([scores(j) for j in left]) + c_left)
        if right:
            m = jnp.maximum(m, col_max([scores(j) for j in right]) + c_right)

        for j in range(n_kt):
            shift = m if abs(j - t) <= 1 else (m - c_left if j < t else m - c_right)
            p_ref[buf, key_rows(j), :] = jnp.exp2(scores(j) - shift).astype(BF16)
        oe1 = _dot(vt_ref[...], p_ref[buf, :, :DA_TQ])
        oe2 = _dot(vt_ref[...], p_ref[buf, :, DA_TQ:])
        o1, l1 = oe1[:HEAD_DIM], oe1[HEAD_DIM:HEAD_DIM + 1]
        o2, l2 = oe2[:HEAD_DIM], oe2[HEAD_DIM:HEAD_DIM + 1]
        o = o1 * (1.0 / l1) - o2 * (lam / l2)
        ms = jnp.mean(o * o, axis=0, keepdims=True)
        y = ((o * lax.rsqrt(ms + EPS)).T * g_ref[...]) * (1.0 - LAM_INIT)
        o_ref[t * DA_TQ:(t + 1) * DA_TQ, :] = y.astype(BF16)


def _da_attn(proj_hm, tb, da_lambda, subln_g, w_gate, w_up, w_down, c_pad, w_ada, b_ada, ada_col0,
             *, batch, seq, heads, q_blk0, k_blk0, v_blk0):
    kern = lambda *a: _da_kernel(*a, seq=seq)
    head_blk = lambda blk0: pl.BlockSpec((None, seq, HEAD_DIM), lambda b, h: (blk0 + h, b, 0))
    n_steps = batch * heads

    def row_slab(w):
        assert w.shape[0] % (BF16_ROW_TILE * n_steps) == 0
        return pl.BlockSpec((w.shape[0] // n_steps, w.shape[1]), lambda b, h: (b * heads + h, 0))

    weights = (w_gate, w_up, w_down)
    D = c_pad.shape[1]
    n_ada = w_ada.shape[1] - ada_col0
    ada_tn = n_ada // n_steps
    assert ada_tn % LANES == 0 and ada_col0 % ada_tn == 0
    ada_blk0 = ada_col0 // ada_tn
    return pl.pallas_call(
        kern,
        out_shape=(jax.ShapeDtypeStruct((batch * seq, heads * HEAD_DIM), BF16),
                   *[jax.ShapeDtypeStruct(w.shape, BF16) for w in weights],
                   jax.ShapeDtypeStruct((c_pad.shape[0], n_ada), F32)),
        grid=(batch, heads),
        in_specs=[
            head_blk(q_blk0), head_blk(k_blk0), head_blk(v_blk0),
            pl.BlockSpec((DA_SLABS, DA_TQ, DA_TQ), lambda b, h: (h, 0, 0)),
            pl.BlockSpec(da_lambda.shape, lambda b, h: (0, 0)),
            pl.BlockSpec((1, HEAD_DIM), lambda b, h: (0, 0)),
            *[row_slab(w) for w in weights],
            pl.BlockSpec(c_pad.shape, lambda b, h: (0, 0)),
            pl.BlockSpec((D, ada_tn), lambda b, h: (0, ada_blk0 + b * heads + h)),
            pl.BlockSpec((1, ada_tn), lambda b, h: (0, ada_blk0 + b * heads + h)),
        ],
        out_specs=(pl.BlockSpec((seq, HEAD_DIM), lambda b, h: (b, h)),
                   *[row_slab(w) for w in weights],
                   pl.BlockSpec((c_pad.shape[0], ada_tn), lambda b, h: (0, b * heads + h))),
        scratch_shapes=[
            pltpu.VMEM((DA_VT_ROWS, seq), BF16),
            pltpu.VMEM((DA_S_BUFS, seq, 2 * DA_TQ), F32),
            pltpu.VMEM((2, seq, 2 * DA_TQ), BF16),
        ],
        compiler_params=pltpu.CompilerParams(
            dimension_semantics=("arbitrary", "arbitrary"), vmem_limit_bytes=48 * MIB),
        name="da_attn",
    )(proj_hm, proj_hm, proj_hm, tb, da_lambda, subln_g, *weights,
      c_pad, w_ada, b_ada)


def _outproj_kernel(na_ref, da_ref, nag_ref, w_ref, x_ref, g1_ref, o_ref, wbf_ref):
    kh = na_ref.shape[1]

    @pl.when(pl.program_id(0) == 0)
    def _():
        def body(r, carry):
            rows = pl.ds(pl.multiple_of(r * WCAST_ROW_CHUNK, WCAST_ROW_CHUNK), WCAST_ROW_CHUNK)
            wbf_ref[rows, :] = w_ref[rows, :].astype(BF16)
            return carry
        lax.fori_loop(0, w_ref.shape[0] // WCAST_ROW_CHUNK, body, 0)

    gain = nag_ref[...]
    for r in range(na_ref.shape[0] // OUTPROJ_ROW_CHUNK):
        rows = slice(r * OUTPROJ_ROW_CHUNK, (r + 1) * OUTPROJ_ROW_CHUNK)
        o = na_ref[rows, :].astype(F32)
        ms = jnp.mean(o * o, axis=-1, keepdims=True)
        na_n = (o * lax.rsqrt(ms + EPS) * gain).astype(BF16)
        mix = _dot(na_n, wbf_ref[:kh, :]) + _dot(da_ref[rows, :], wbf_ref[kh:, :])
        o_ref[rows, :] = x_ref[rows, :] + g1_ref[0] * mix


def _out_proj(na_o, da_o, na_out_g, w_out, x2d, mod3, *, seq):
    M, D = x2d.shape
    Kh = na_o.shape[1]
    tm = 512
    per_b = seq // tm
    return pl.pallas_call(
        _outproj_kernel,
        out_shape=jax.ShapeDtypeStruct((M, D), F32),
        grid=(M // tm,),
        in_specs=[
            pl.BlockSpec((tm, Kh), lambda i: (i, 0)),
            pl.BlockSpec((tm, Kh), lambda i: (i, 0)),
            pl.BlockSpec((1, Kh), lambda i: (0, 0)),
            pl.BlockSpec(w_out.shape, lambda i: (0, 0), pipeline_mode=pl.Buffered(1)),
            pl.BlockSpec((tm, D), lambda i: (i, 0)),
            pl.BlockSpec((1, 1, D), lambda i: ((i // per_b) * 4 + 0, 0, 0)),
        ],
        out_specs=pl.BlockSpec((tm, D), lambda i: (i, 0)),
        scratch_shapes=[pltpu.VMEM(w_out.shape, BF16)],
        compiler_params=pltpu.CompilerParams(
            dimension_semantics=("arbitrary",), vmem_limit_bytes=56 * MIB),
        name="out_proj",
    )(na_o, da_o, na_out_g, w_out, x2d, mod3)


def _ffn_kernel(x_ref, sh_ref, sc_ref, g2_ref, ng_ref, fg_ref, wg_ref, wu_ref, wd_ref, o_ref, h_ref,
                *, n_f):
    f = pl.program_id(1)
    tm = x_ref.shape[0]
    assert n_f >= 2

    def step(first, last):
        chunk = FFN_ROW_CHUNK if (first or last) else tm
        for r in range(tm // chunk):
            rows = slice(r * chunk, (r + 1) * chunk)
            if first:
                gain = ng_ref[...] * (1.0 + sc_ref[0])
                for sub in range(chunk // NORM_ROW_CHUNK):
                    nrows = slice(rows.start + sub * NORM_ROW_CHUNK,
                                  rows.start + (sub + 1) * NORM_ROW_CHUNK)
                    x = x_ref[nrows, :]
                    ms = jnp.mean(x * x, axis=-1, keepdims=True)
                    h_ref[nrows, :] = (x * lax.rsqrt(ms + EPS) * gain + sh_ref[0]).astype(BF16)
            h = h_ref[rows, :]
            g = _dot(h, wg_ref[...])
            u = _dot(h, wu_ref[...])
            a = (g * _sigmoid(g) * u).astype(BF16)
            part = _dot(a, wd_ref[...])
            acc = part if first else o_ref[rows, :] + part
            if last:
                x2 = x_ref[rows, :] + g2_ref[0] * acc
                ms = jnp.mean(x2 * x2, axis=-1, keepdims=True)
                acc = x2 * lax.rsqrt(ms + EPS) * fg_ref[...]
            o_ref[rows, :] = acc

    pl.when(f == 0)(lambda: step(True, False))
    pl.when(jnp.logical_and(f > 0, f < n_f - 1))(lambda: step(False, False))
    pl.when(f == n_f - 1)(lambda: step(False, True))


def _ffn(x1, mod3, norm_g, final_g, w_gate, w_up, w_down, *, seq):
    M, D = x1.shape
    F = w_gate.shape[1]
    tm, tf = 1024, 512
    per_b = seq // tm
    n_f = F // tf
    kern = lambda *a: _ffn_kernel(*a, n_f=n_f)
    return pl.pallas_call(
        kern,
        out_shape=jax.ShapeDtypeStruct((M, D), F32),
        grid=(M // tm, n_f),
        in_specs=[
            pl.BlockSpec((tm, D), lambda i, f: (i, 0)),
            pl.BlockSpec((1, 1, D), lambda i, f: ((i // per_b) * 4 + 1, 0, 0)),
            pl.BlockSpec((1, 1, D), lambda i, f: ((i // per_b) * 4 + 2, 0, 0)),
            pl.BlockSpec((1, 1, D), lambda i, f: ((i // per_b) * 4 + 3, 0, 0)),
            pl.BlockSpec((1, D), lambda i, f: (0, 0)),
            pl.BlockSpec((1, D), lambda i, f: (0, 0)),
            pl.BlockSpec((D, tf), lambda i, f: (0, f)),
            pl.BlockSpec((D, tf), lambda i, f: (0, f)),
            pl.BlockSpec((tf, D), lambda i, f: (f, 0)),
        ],
        out_specs=pl.BlockSpec((tm, D), lambda i, f: (i, 0)),
        scratch_shapes=[pltpu.VMEM((tm, D), BF16)],
        compiler_params=pltpu.CompilerParams(
            dimension_semantics=("arbitrary", "arbitrary"), vmem_limit_bytes=60 * MIB),
        name="ffn",
    )(x1, mod3, mod3, mod3, norm_g, final_g, w_gate, w_up, w_down)


def kernel(x, c, w_ada, b_ada, norm1_g, w_in, na_rpb, na_out_g, da_lambda, da_subln_g, t5_table,
           w_out, norm2_g, w_gate, w_up, w_down, final_g):
    B, S, D = x.shape
    na_heads = na_rpb.shape[1]
    da_heads = t5_table.shape[1]
    na_width = na_heads * HEAD_DIM
    assert w_ada.shape[0] == 1, "single layer"

    assert B <= SUBLANES
    c_pad = jnp.zeros((SUBLANES, D), F32).at[:B].set(c)
    b_ada2d = b_ada[0].reshape(1, -1)
    mod_a = _ada_mod(c_pad, w_ada[0], b_ada2d, 2 * D)[:B].reshape(B * 2, 1, D)
    x2d = x.reshape(B * S, D)

    proj_hm = _in_proj(x2d, mod_a, norm1_g, w_in[0], seq=S,
                       na_width=na_width, da_q_off=3 * na_width)
    na_o = _na_attn(proj_hm, _na_table(na_rpb[0]), batch=B, seq=S, heads=na_heads,
                    q_blk0=0, k_blk0=na_heads, v_blk0=2 * na_heads)
    da_o, wg_bf, wu_bf, wd_bf, mod_b = _da_attn(
        proj_hm, _da_table(t5_table), da_lambda[0], da_subln_g, w_gate[0], w_up[0], w_down[0],
        c_pad, w_ada[0], b_ada2d, 2 * D,
        batch=B, seq=S, heads=da_heads, q_blk0=3 * na_heads, k_blk0=3 * na_heads + da_heads,
        v_blk0=3 * na_heads + 2 * da_heads)
    mod_b = mod_b[:B].reshape(B * 4, 1, D)
    x1 = _out_proj(na_o, da_o, na_out_g, w_out[0], x2d, mod_b, seq=S)
    out = _ffn(x1, mod_b, norm2_g, final_g.reshape(1, D), wg_bf, wu_bf, wd_bf, seq=S)
    return out.reshape(B, S, D)
```

```python
import math

import jax
import jax.numpy as jnp
from jax import lax
from jax.experimental import pallas as pl
from jax.experimental.pallas import tpu as pltpu

F32 = jnp.float32
BF16 = jnp.bfloat16

EPS = 1e-6
NEG = -1e30

GRID_W = 64
HEAD_DIM = 128
DA_QK_DIM = 64
NA_WIN_ROWS = 8
NA_WIN_COLS = 16
T5_BUCKETS = 32
T5_MAX_DIST = 128
LAM_INIT = 0.8 - 0.6 * math.exp(-0.3 * 0)
LOG2E = math.log2(math.e)

MIB = 1024 * 1024

SUBLANES = 8
LANES = 128
BF16_ROW_TILE = 16

NA_QROWS = 8
NA_KROWS = 16
NA_TQ = NA_QROWS * GRID_W
NA_TK = NA_KROWS * GRID_W
NA_LOOKAHEAD = 1

DA_TQ = 256
DA_VT_ROWS = HEAD_DIM + BF16_ROW_TILE
DA_S_BUFS = 3

FFN_ROW_CHUNK = 512
NORM_ROW_CHUNK = 128
INPROJ_ROW_CHUNK = 256
OUTPROJ_ROW_CHUNK = 256
WCAST_ROW_CHUNK = 128


def _dot(a, b):
    return jnp.dot(a, b, preferred_element_type=F32)


def _dot_nt(a, b):
    return lax.dot_general(a, b, (((1,), (1,)), ((), ())), preferred_element_type=F32)


def _sigmoid(x):
    return 1.0 / (1.0 + jnp.exp(-x))


def _ada_columns(c_ref, w_ref, b_ref):
    c = c_ref[...]
    cs = c * _sigmoid(c)
    return _dot(cs.astype(BF16), w_ref[...].astype(BF16)) + b_ref[...]


def _ada_kernel(c_ref, w_ref, b_ref, o_ref):
    o_ref[...] = _ada_columns(c_ref, w_ref, b_ref)


def _ada_mod(c_pad, w_ada, b_ada, n_cols):
    rows, D = c_pad.shape
    tn = 1024
    return pl.pallas_call(
        _ada_kernel,
        out_shape=jax.ShapeDtypeStruct((rows, n_cols), F32),
        grid=(n_cols // tn,),
        in_specs=[
            pl.BlockSpec((rows, D), lambda j: (0, 0)),
            pl.BlockSpec((D, tn), lambda j: (0, j)),
            pl.BlockSpec((1, tn), lambda j: (0, j)),
        ],
        out_specs=pl.BlockSpec((rows, tn), lambda j: (0, j)),
        compiler_params=pltpu.CompilerParams(
            dimension_semantics=("arbitrary",), vmem_limit_bytes=40 * MIB),
        name="ada_mod",
    )(c_pad, w_ada, b_ada)


def _inproj_kernel(x_ref, sh_ref, sc_ref, g_ref, w_ref, o_ref, h_ref, *, na_q_blk, da_q_blk,
                   na_scale, da_scale):
    j = pl.program_id(1)
    scale = jnp.where(j == na_q_blk, na_scale, jnp.where(j == da_q_blk, da_scale, 1.0))

    def project(rows):
        acc = _dot(h_ref[rows, :], w_ref[...].astype(BF16)) * scale
        for k in range(o_ref.shape[0]):
            o_ref[k, rows, :] = acc[:, k * HEAD_DIM:(k + 1) * HEAD_DIM].astype(BF16)

    @pl.when(j == 0)
    def _():
        gain = g_ref[...] * (1.0 + sc_ref[0])
        tm = x_ref.shape[0]
        for r in range(tm // INPROJ_ROW_CHUNK):
            for sub in range(INPROJ_ROW_CHUNK // NORM_ROW_CHUNK):
                start = r * INPROJ_ROW_CHUNK + sub * NORM_ROW_CHUNK
                nrows = slice(start, start + NORM_ROW_CHUNK)
                x = x_ref[nrows, :]
                ms = jnp.mean(x * x, axis=-1, keepdims=True)
                h_ref[nrows, :] = (x * lax.rsqrt(ms + EPS) * gain + sh_ref[0]).astype(BF16)
            project(slice(r * INPROJ_ROW_CHUNK, (r + 1) * INPROJ_ROW_CHUNK))

    @pl.when(j > 0)
    def _():
        project(slice(None))


def _in_proj(x2d, mod3, norm_g, w_in, *, seq, na_width, da_q_off):
    M, D = x2d.shape
    N = w_in.shape[1]
    tm, tn = 1024, 1024
    per_b = seq // tm
    kern = lambda *a: _inproj_kernel(
        *a, na_q_blk=0, da_q_blk=da_q_off // tn,
        na_scale=HEAD_DIM ** -0.5 * LOG2E, da_scale=DA_QK_DIM ** -0.5 * LOG2E)
    assert na_width == tn
    return pl.pallas_call(
        kern,
        out_shape=jax.ShapeDtypeStruct((N // HEAD_DIM, M, HEAD_DIM), BF16),
        grid=(M // tm, N // tn),
        in_specs=[
            pl.BlockSpec((tm, D), lambda i, j: (i, 0)),
            pl.BlockSpec((1, 1, D), lambda i, j: ((i // per_b) * 2 + 0, 0, 0)),
            pl.BlockSpec((1, 1, D), lambda i, j: ((i // per_b) * 2 + 1, 0, 0)),
            pl.BlockSpec((1, D), lambda i, j: (0, 0)),
            pl.BlockSpec((D, tn), lambda i, j: (0, j)),
        ],
        out_specs=pl.BlockSpec((tn // HEAD_DIM, tm, HEAD_DIM), lambda i, j: (j, i, 0)),
        scratch_shapes=[pltpu.VMEM((tm, D), BF16)],
        compiler_params=pltpu.CompilerParams(
            dimension_semantics=("arbitrary", "arbitrary"), vmem_limit_bytes=48 * MIB),
        name="in_proj",
    )(x2d, mod3, mod3, norm_g, w_in)


NA_REL_ROWS = 2 * NA_WIN_ROWS - 1
NA_REL_COLS = 2 * NA_WIN_COLS - 1
NA_PAIR_TILES = 2 * NA_WIN_ROWS


def _na_table_kernel(rpb_ref, o_ref):
    h = pl.program_id(0)
    shape = (GRID_W, 2 * GRID_W)
    qc = lax.broadcasted_iota(jnp.int32, shape, 0)
    lane = lax.broadcasted_iota(jnp.int32, shape, 1)
    upper = lane >= GRID_W
    kc = jnp.where(upper, lane - GRID_W, lane)
    c0 = jnp.clip(qc - NA_WIN_COLS // 2, 0, GRID_W - NA_WIN_COLS)
    in_win = (kc >= c0) & (kc < c0 + NA_WIN_COLS)
    lane_v = lax.broadcasted_iota(jnp.int32, (SUBLANES, 2 * GRID_W), 1)
    base = h * (NA_REL_ROWS * NA_REL_COLS)

    def rotated_rows(rel_row, center):
        if not 0 <= rel_row < NA_REL_ROWS:
            return jnp.full(shape, NEG, F32)
        vec = jnp.zeros(lane_v.shape, F32)
        for d in range(NA_REL_COLS):
            at = (center + d - (NA_WIN_COLS - 1)) % (2 * GRID_W)
            vec = jnp.where(lane_v == at, rpb_ref[base + rel_row * NA_REL_COLS + d] * LOG2E, vec)
        rows = jnp.concatenate([vec] * (GRID_W // SUBLANES), axis=0)
        return pltpu.roll(rows, 0, 1, stride=1, stride_axis=0)

    for e in range(NA_PAIR_TILES):
        tile = jnp.where(upper, rotated_rows(e, GRID_W), rotated_rows(e - 1, 0))
        o_ref[e] = jnp.where(in_win, tile, NEG)


def _na_table(rpb):
    heads = rpb.shape[0]
    assert rpb.shape[1:] == (NA_REL_ROWS, NA_REL_COLS)
    return pl.pallas_call(
        _na_table_kernel,
        out_shape=jax.ShapeDtypeStruct((heads * NA_PAIR_TILES, GRID_W, 2 * GRID_W), F32),
        grid=(heads,),
        in_specs=[pl.BlockSpec(memory_space=pltpu.SMEM)],
        out_specs=pl.BlockSpec((NA_PAIR_TILES, GRID_W, 2 * GRID_W), lambda h: (h, 0, 0)),
        compiler_params=pltpu.CompilerParams(dimension_semantics=("arbitrary",)),
        name="na_table",
    )(rpb.reshape(-1))


def _na_window_start_row(blk, rows):
    return min(max(blk * NA_QROWS - (NA_KROWS - NA_QROWS) // 2, 0), rows - NA_KROWS)


NA_PAIRS_PER_KTILE = 2


def _na_pair_validity(blk, rows):
    kr0 = _na_window_start_row(blk, rows)
    valid = []
    for i in range(NA_QROWS):
        qr = blk * NA_QROWS + i
        rs = min(max(qr - NA_WIN_ROWS // 2, 0), rows - NA_WIN_ROWS)
        valid.append([(rs <= kr0 + 2 * m < rs + NA_WIN_ROWS, rs <= kr0 + 2 * m + 1 < rs + NA_WIN_ROWS)
                      for m in range(NA_KROWS // 2)])
    return valid


def _na_ktile_rows(blk, rows):
    valid = _na_pair_validity(blk, rows)
    ranges = []
    for kt in range(NA_KROWS // 2 // NA_PAIRS_PER_KTILE):
        pairs = range(kt * NA_PAIRS_PER_KTILE, (kt + 1) * NA_PAIRS_PER_KTILE)
        users = [i for i in range(NA_QROWS) if any(any(valid[i][m]) for m in pairs)]
        if not users:
            ranges.append((0, 0))
            continue
        assert users == list(range(users[0], users[-1] + 1))
        ranges.append((users[0], users[-1] + 1))
    return ranges


def _na_block(s_tiles, vw, tp_ref, blk, rows):
    lane = lax.broadcasted_iota(jnp.int32, (GRID_W, 2 * GRID_W), 1)
    kr0 = _na_window_start_row(blk, rows)
    valid = _na_pair_validity(blk, rows)
    kt_rows = _na_ktile_rows(blk, rows)
    probs, inv_l = [], []
    for i in range(NA_QROWS):
        qr = blk * NA_QROWS + i
        tiles = {}
        for m in range(NA_KROWS // 2):
            va, vb = valid[i][m]
            if not (va or vb):
                continue
            kt, sub = divmod(m, NA_PAIRS_PER_KTILE)
            lo = kt_rows[kt][0]
            e = kr0 + 2 * m - qr + NA_WIN_ROWS
            t = s_tiles[kt][(i - lo) * GRID_W:(i - lo + 1) * GRID_W,
                            sub * 2 * GRID_W:(sub + 1) * 2 * GRID_W] + tp_ref[e]
            if not vb:
                t = jnp.where(lane < GRID_W, t, NEG)
            elif not va:
                t = jnp.where(lane >= GRID_W, t, NEG)
            tiles[m] = t
        ts = list(tiles.values())
        mx = ts[0]
        for t in ts[1:]:
            mx = jnp.maximum(mx, t)
        mx = jnp.max(mx, axis=-1, keepdims=True)
        ps = {m: jnp.exp2(t - mx) for m, t in tiles.items()}
        tot = None
        for pt in ps.values():
            tot = pt if tot is None else tot + pt
        inv_l.append(1.0 / jnp.sum(tot, axis=-1, keepdims=True))
        probs.append({m: pt.astype(BF16) for m, pt in ps.items()})

    zero = jnp.zeros((GRID_W, 2 * GRID_W), BF16)
    out = [None] * NA_QROWS
    for kt, (lo, hi) in enumerate(kt_rows):
        if hi == lo:
            continue
        pairs = range(kt * NA_PAIRS_PER_KTILE, (kt + 1) * NA_PAIRS_PER_KTILE)
        p_kt = jnp.concatenate(
            [jnp.concatenate([probs[i].get(m, zero) for m in pairs], axis=1) for i in range(lo, hi)],
            axis=0)
        keys = slice(kt * NA_PAIRS_PER_KTILE * 2 * GRID_W, (kt + 1) * NA_PAIRS_PER_KTILE * 2 * GRID_W)
        part = _dot(p_kt, vw[keys, :])
        for i in range(lo, hi):
            piece = part[(i - lo) * GRID_W:(i - lo + 1) * GRID_W, :]
            out[i] = piece if out[i] is None else out[i] + piece
    return jnp.concatenate([out[i] * inv_l[i] for i in range(NA_QROWS)], axis=0)


def _na_kernel(q_ref, k_ref, v_ref, tp_ref, o_ref, *, rows):
    n_blk = rows // NA_QROWS

    def window(blk):
        start = _na_window_start_row(blk, rows) * GRID_W
        return slice(start, start + NA_TK)

    def scores(blk):
        start = window(blk).start
        tile_keys = NA_PAIRS_PER_KTILE * 2 * GRID_W
        return [
            None if hi == lo else
            _dot_nt(q_ref[blk * NA_TQ + lo * GRID_W:blk * NA_TQ + hi * GRID_W, :],
                    k_ref[start + kt * tile_keys:start + (kt + 1) * tile_keys, :])
            for kt, (lo, hi) in enumerate(_na_ktile_rows(blk, rows))]

    pending = [scores(blk) for blk in range(min(NA_LOOKAHEAD, n_blk))]
    for blk in range(n_blk):
        if blk + NA_LOOKAHEAD < n_blk:
            pending.append(scores(blk + NA_LOOKAHEAD))
        o = _na_block(pending.pop(0), v_ref[window(blk), :], tp_ref, blk, rows)
        o_ref[blk * NA_TQ:(blk + 1) * NA_TQ, :] = o.astype(BF16)


def _na_attn(proj_hm, tp, *, batch, seq, heads, q_blk0, k_blk0, v_blk0):
    rows = seq // GRID_W
    kern = lambda *a: _na_kernel(*a, rows=rows)
    head_blk = lambda blk0: pl.BlockSpec((None, seq, HEAD_DIM), lambda b, h: (blk0 + h, b, 0))
    return pl.pallas_call(
        kern,
        out_shape=jax.ShapeDtypeStruct((batch * seq, heads * HEAD_DIM), BF16),
        grid=(batch, heads),
        in_specs=[
            head_blk(q_blk0), head_blk(k_blk0), head_blk(v_blk0),
            pl.BlockSpec((NA_PAIR_TILES, GRID_W, 2 * GRID_W), lambda b, h: (h, 0, 0)),
        ],
        out_specs=pl.BlockSpec((seq, HEAD_DIM), lambda b, h: (b, h)),
        compiler_params=pltpu.CompilerParams(
            dimension_semantics=("arbitrary", "arbitrary"), vmem_limit_bytes=48 * MIB),
        name="na_attn",
    )(proj_hm, proj_hm, proj_hm, tp)


def _t5_bucket(rel):
    nb = T5_BUCKETS // 2
    ret = jnp.where(rel > 0, nb, 0)
    n = jnp.abs(rel)
    max_exact = nb // 2
    nf = jnp.maximum(n, 1).astype(jnp.float32)
    large = max_exact + (jnp.log(nf / max_exact) / math.log(T5_MAX_DIST / max_exact)
                         * (nb - max_exact)).astype(jnp.int32)
    large = jnp.minimum(large, nb - 1)
    return ret + jnp.where(n < max_exact, n, large)


DA_SLABS = 5


DA_REL_SPAN = 4 * DA_TQ


def _da_table_kernel(bias_ref, o_ref):
    vec = bias_ref[...] * LOG2E
    rows = jnp.broadcast_to(vec, (DA_TQ, DA_REL_SPAN))
    rolled = pltpu.roll(rows, 1, 1, stride=1, stride_axis=0)
    for dj in (-1, 0, 1):
        lanes = 2 * DA_TQ - dj * DA_TQ
        o_ref[dj + 2] = rolled[:, lanes:lanes + DA_TQ]
    shape = (DA_TQ, DA_TQ)
    o_ref[0] = jnp.broadcast_to(vec[:, DA_REL_SPAN - 1:], shape)
    o_ref[DA_SLABS - 1] = jnp.broadcast_to(vec[:, :1], shape)


def _da_table(t5_table):
    heads = t5_table.shape[1]
    assert DA_TQ >= T5_MAX_DIST
    rel = 2 * DA_TQ - 1 - jnp.arange(DA_REL_SPAN)
    bias = t5_table.astype(F32)[_t5_bucket(rel)].T.reshape(heads, 1, DA_REL_SPAN)
    return pl.pallas_call(
        _da_table_kernel,
        out_shape=jax.ShapeDtypeStruct((heads * DA_SLABS, DA_TQ, DA_TQ), F32),
        grid=(heads,),
        in_specs=[pl.BlockSpec((None, 1, DA_REL_SPAN), lambda h: (h, 0, 0))],
        out_specs=pl.BlockSpec((DA_SLABS, DA_TQ, DA_TQ), lambda h: (h, 0, 0)),
        compiler_params=pltpu.CompilerParams(dimension_semantics=("arbitrary",)),
        name="da_table",
    )(bias)


def _da_kernel(q_ref, k_ref, v_ref, tb_ref, lam_ref, g_ref, wg_ref, wu_ref, wd_ref,
               c_ref, wada_ref, bada_ref,
               o_ref, wg_bf_ref, wu_bf_ref, wd_bf_ref, mod_ref, vt_ref, s_ref, p_ref, *, seq):
    n_kt = seq // DA_TQ

    wg_bf_ref[...] = wg_ref[...].astype(BF16)
    wu_bf_ref[...] = wu_ref[...].astype(BF16)
    wd_bf_ref[...] = wd_ref[...].astype(BF16)
    mod_ref[...] = _ada_columns(c_ref, wada_ref, bada_ref)

    vt_ref[:HEAD_DIM, :] = v_ref[...].astype(F32).T.astype(BF16)
    row = lax.broadcasted_iota(jnp.int32, (DA_VT_ROWS - HEAD_DIM, seq), 0)
    vt_ref[HEAD_DIM:, :] = jnp.where(row == 0, 1.0, 0.0).astype(BF16)

    lp = lam_ref[...]
    t1 = jnp.sum(lp[0:1] * lp[1:2], axis=-1, keepdims=True)
    t2 = jnp.sum(lp[2:3] * lp[3:4], axis=-1, keepdims=True)
    lam = jnp.exp(t1) - jnp.exp(t2) + LAM_INIT

    c_left = tb_ref[0, 0:1, 0:1]
    c_right = tb_ref[DA_SLABS - 1, 0:1, 0:1]
    lane = lax.broadcasted_iota(jnp.int32, (DA_TQ, HEAD_DIM), 1)

    def key_rows(j):
        return slice(j * DA_TQ, (j + 1) * DA_TQ)

    def col_max(parts):
        acc = parts[0]
        for part in parts[1:]:
            acc = jnp.maximum(acc, part)
        return jnp.max(acc, axis=0, keepdims=True)

    def qk_scores(t):
        q = q_ref[t * DA_TQ:(t + 1) * DA_TQ, :]
        zero = jnp.zeros_like(q)
        q12 = jnp.concatenate(
            [jnp.where(lane < DA_QK_DIM, q, zero), jnp.where(lane >= DA_QK_DIM, q, zero)], axis=0)
        s_ref[t % DA_S_BUFS] = _dot_nt(k_ref[...], q12)

    for t0 in range(DA_S_BUFS - 1):
        qk_scores(t0)
    for t in range(n_kt):
        buf = t % 2
        if t + DA_S_BUFS - 1 < n_kt:
            qk_scores(t + DA_S_BUFS - 1)

        def scores(j):
            s = s_ref[t % DA_S_BUFS, key_rows(j), :]
            if abs(j - t) <= 1:
                bias = tb_ref[j - t + 2]
                s = s + jnp.concatenate([bias, bias], axis=1)
            return s

        near = [j for j in range(n_kt) if abs(j - t) <= 1]
        left = [j for j in range(n_kt) if j < t - 1]
        right = [j for j in range(n_kt) if j > t + 1]
        m = col_max([scores(j) for j in near])
        if left:
            m = jnp.maximum(m, col_max([scores(j) for j in left]) + c_left)
        if right:
            m = jnp.maximum(m, col_max([scores(j) for j in right]) + c_right)

        for j in range(n_kt):
            shift = m if abs(j - t) <= 1 else (m - c_left if j < t else m - c_right)
            p_ref[buf, key_rows(j), :] = jnp.exp2(scores(j) - shift).astype(BF16)
        oe1 = _dot(vt_ref[...], p_ref[buf, :, :DA_TQ])
        oe2 = _dot(vt_ref[...], p_ref[buf, :, DA_TQ:])
        o1, l1 = oe1[:HEAD_DIM], oe1[HEAD_DIM:HEAD_DIM + 1]
        o2, l2 = oe2[:HEAD_DIM], oe2[HEAD_DIM:HEAD_DIM + 1]
        o = o1 * (1.0 / l1) - o2 * (lam / l2)
        ms = jnp.mean(o * o, axis=0, keepdims=True)
        y = ((o * lax.rsqrt(ms + EPS)).T * g_ref[...]) * (1.0 - LAM_INIT)
        o_ref[t * DA_TQ:(t + 1) * DA_TQ, :] = y.astype(BF16)


def _da_attn(proj_hm, tb, da_lambda, subln_g, w_gate, w_up, w_down, c_pad, w_ada, b_ada, ada_col0,
             *, batch, seq, heads, q_blk0, k_blk0, v_blk0):
    kern = lambda *a: _da_kernel(*a, seq=seq)
    head_blk = lambda blk0: pl.BlockSpec((None, seq, HEAD_DIM), lambda b, h: (blk0 + h, b, 0))
    n_steps = batch * heads

    def row_slab(w):
        assert w.shape[0] % (BF16_ROW_TILE * n_steps) == 0
        return pl.BlockSpec((w.shape[0] // n_steps, w.shape[1]), lambda b, h: (b * heads + h, 0))

    weights = (w_gate, w_up, w_down)
    D = c_pad.shape[1]
    n_ada = w_ada.shape[1] - ada_col0
    ada_tn = n_ada // n_steps
    assert ada_tn % LANES == 0 and ada_col0 % ada_tn == 0
    ada_blk0 = ada_col0 // ada_tn
    return pl.pallas_call(
        kern,
        out_shape=(jax.ShapeDtypeStruct((batch * seq, heads * HEAD_DIM), BF16),
                   *[jax.ShapeDtypeStruct(w.shape, BF16) for w in weights],
                   jax.ShapeDtypeStruct((c_pad.shape[0], n_ada), F32)),
        grid=(batch, heads),
        in_specs=[
            head_blk(q_blk0), head_blk(k_blk0), head_blk(v_blk0),
            pl.BlockSpec((DA_SLABS, DA_TQ, DA_TQ), lambda b, h: (h, 0, 0)),
            pl.BlockSpec(da_lambda.shape, lambda b, h: (0, 0)),
            pl.BlockSpec((1, HEAD_DIM), lambda b, h: (0, 0)),
            *[row_slab(w) for w in weights],
            pl.BlockSpec(c_pad.shape, lambda b, h: (0, 0)),
            pl.BlockSpec((D, ada_tn), lambda b, h: (0, ada_blk0 + b * heads + h)),
            pl.BlockSpec((1, ada_tn), lambda b, h: (0, ada_blk0 + b * heads + h)),
        ],
        out_specs=(pl.BlockSpec((seq, HEAD_DIM), lambda b, h: (b, h)),
                   *[row_slab(w) for w in weights],
                   pl.BlockSpec((c_pad.shape[0], ada_tn), lambda b, h: (0, b * heads + h))),
        scratch_shapes=[
            pltpu.VMEM((DA_VT_ROWS, seq), BF16),
            pltpu.VMEM((DA_S_BUFS, seq, 2 * DA_TQ), F32),
            pltpu.VMEM((2, seq, 2 * DA_TQ), BF16),
        ],
        compiler_params=pltpu.CompilerParams(
            dimension_semantics=("arbitrary", "arbitrary"), vmem_limit_bytes=48 * MIB),
        name="da_attn",
    )(proj_hm, proj_hm, proj_hm, tb, da_lambda, subln_g, *weights,
      c_pad, w_ada, b_ada)


def _outproj_kernel(na_ref, da_ref, nag_ref, w_ref, x_ref, g1_ref, o_ref, wbf_ref):
    kh = na_ref.shape[1]

    @pl.when(pl.program_id(0) == 0)
    def _():
        def body(r, carry):
            rows = pl.ds(pl.multiple_of(r * WCAST_ROW_CHUNK, WCAST_ROW_CHUNK), WCAST_ROW_CHUNK)
            wbf_ref[rows, :] = w_ref[rows, :].astype(BF16)
            return carry
        lax.fori_loop(0, w_ref.shape[0] // WCAST_ROW_CHUNK, body, 0)

    gain = nag_ref[...]
    for r in range(na_ref.shape[0] // OUTPROJ_ROW_CHUNK):
        rows = slice(r * OUTPROJ_ROW_CHUNK, (r + 1) * OUTPROJ_ROW_CHUNK)
        o = na_ref[rows, :].astype(F32)
        ms = jnp.mean(o * o, axis=-1, keepdims=True)
        na_n = (o * lax.rsqrt(ms + EPS) * gain).astype(BF16)
        mix = _dot(na_n, wbf_ref[:kh, :]) + _dot(da_ref[rows, :], wbf_ref[kh:, :])
        o_ref[rows, :] = x_ref[rows, :] + g1_ref[0] * mix


def _out_proj(na_o, da_o, na_out_g, w_out, x2d, mod3, *, seq):
    M, D = x2d.shape
    Kh = na_o.shape[1]
    tm = 512
    per_b = seq // tm
    return pl.pallas_call(
        _outproj_kernel,
        out_shape=jax.ShapeDtypeStruct((M, D), F32),
        grid=(M // tm,),
        in_specs=[
            pl.BlockSpec((tm, Kh), lambda i: (i, 0)),
            pl.BlockSpec((tm, Kh), lambda i: (i, 0)),
            pl.BlockSpec((1, Kh), lambda i: (0, 0)),
            pl.BlockSpec(w_out.shape, lambda i: (0, 0), pipeline_mode=pl.Buffered(1)),
            pl.BlockSpec((tm, D), lambda i: (i, 0)),
            pl.BlockSpec((1, 1, D), lambda i: ((i // per_b) * 4 + 0, 0, 0)),
        ],
        out_specs=pl.BlockSpec((tm, D), lambda i: (i, 0)),
        scratch_shapes=[pltpu.VMEM(w_out.shape, BF16)],
        compiler_params=pltpu.CompilerParams(
            dimension_semantics=("arbitrary",), vmem_limit_bytes=56 * MIB),
        name="out_proj",
    )(na_o, da_o, na_out_g, w_out, x2d, mod3)


def _ffn_kernel(x_ref, sh_ref, sc_ref, g2_ref, ng_ref, fg_ref, wg_ref, wu_ref, wd_ref, o_ref, h_ref,
                *, n_f):
    f = pl.program_id(1)
    tm = x_ref.shape[0]
    assert n_f >= 2

    def step(first, last):
        chunk = FFN_ROW_CHUNK if (first or last) else tm
        for r in range(tm // chunk):
            rows = slice(r * chunk, (r + 1) * chunk)
            if first:
                gain = ng_ref[...] * (1.0 + sc_ref[0])
                for sub in range(chunk // NORM_ROW_CHUNK):
                    nrows = slice(rows.start + sub * NORM_ROW_CHUNK,
                                  rows.start + (sub + 1) * NORM_ROW_CHUNK)
                    x = x_ref[nrows, :]
                    ms = jnp.mean(x * x, axis=-1, keepdims=True)
                    h_ref[nrows, :] = (x * lax.rsqrt(ms + EPS) * gain + sh_ref[0]).astype(BF16)
            h = h_ref[rows, :]
            g = _dot(h, wg_ref[...])
            u = _dot(h, wu_ref[...])
            a = (g * _sigmoid(g) * u).astype(BF16)
            part = _dot(a, wd_ref[...])
            acc = part if first else o_ref[rows, :] + part
            if last:
                x2 = x_ref[rows, :] + g2_ref[0] * acc
                ms = jnp.mean(x2 * x2, axis=-1, keepdims=True)
                acc = x2 * lax.rsqrt(ms + EPS) * fg_ref[...]
            o_ref[rows, :] = acc

    pl.when(f == 0)(lambda: step(True, False))
    pl.when(jnp.logical_and(f > 0, f < n_f - 1))(lambda: step(False, False))
    pl.when(f == n_f - 1)(lambda: step(False, True))


def _ffn(x1, mod3, norm_g, final_g, w_gate, w_up, w_down, *, seq):
    M, D = x1.shape
    F = w_gate.shape[1]
    tm, tf = 1024, 512
    per_b = seq // tm
    n_f = F // tf
    kern = lambda *a: _ffn_kernel(*a, n_f=n_f)
    return pl.pallas_call(
        kern,
        out_shape=jax.ShapeDtypeStruct((M, D), F32),
        grid=(M // tm, n_f),
        in_specs=[
            pl.BlockSpec((tm, D), lambda i, f: (i, 0)),
            pl.BlockSpec((1, 1, D), lambda i, f: ((i // per_b) * 4 + 1, 0, 0)),
            pl.BlockSpec((1, 1, D), lambda i, f: ((i // per_b) * 4 + 2, 0, 0)),
            pl.BlockSpec((1, 1, D), lambda i, f: ((i // per_b) * 4 + 3, 0, 0)),
            pl.BlockSpec((1, D), lambda i, f: (0, 0)),
            pl.BlockSpec((1, D), lambda i, f: (0, 0)),
            pl.BlockSpec((D, tf), lambda i, f: (0, f)),
            pl.BlockSpec((D, tf), lambda i, f: (0, f)),
            pl.BlockSpec((tf, D), lambda i, f: (f, 0)),
        ],
        out_specs=pl.BlockSpec((tm, D), lambda i, f: (i, 0)),
        scratch_shapes=[pltpu.VMEM((tm, D), BF16)],
        compiler_params=pltpu.CompilerParams(
            dimension_semantics=("arbitrary", "arbitrary"), vmem_limit_bytes=60 * MIB),
        name="ffn",
    )(x1, mod3, mod3, mod3, norm_g, final_g, w_gate, w_up, w_down)


def kernel(x, c, w_ada, b_ada, norm1_g, w_in, na_rpb, na_out_g, da_lambda, da_subln_g, t5_table,
           w_out, norm2_g, w_gate, w_up, w_down, final_g):
    B, S, D = x.shape
    na_heads = na_rpb.shape[1]
    da_heads = t5_table.shape[1]
    na_width = na_heads * HEAD_DIM
    assert w_ada.shape[0] == 1, "single layer"

    assert B <= SUBLANES
    c_pad = jnp.zeros((SUBLANES, D), F32).at[:B].set(c)
    b_ada2d = b_ada[0].reshape(1, -1)
    mod_a = _ada_mod(c_pad, w_ada[0], b_ada2d, 2 * D)[:B].reshape(B * 2, 1, D)
    x2d = x.reshape(B * S, D)

    proj_hm = _in_proj(x2d, mod_a, norm1_g, w_in[0], seq=S,
                       na_width=na_width, da_q_off=3 * na_width)
    na_o = _na_attn(proj_hm, _na_table(na_rpb[0]), batch=B, seq=S, heads=na_heads,
                    q_blk0=0, k_blk0=na_heads, v_blk0=2 * na_heads)
    da_o, wg_bf, wu_bf, wd_bf, mod_b = _da_attn(
        proj_hm, _da_table(t5_table), da_lambda[0], da_subln_g, w_gate[0], w_up[0], w_down[0],
        c_pad, w_ada[0], b_ada2d, 2 * D,
        batch=B, seq=S, heads=da_heads, q_blk0=3 * na_heads, k_blk0=3 * na_heads + da_heads,
        v_blk0=3 * na_heads + 2 * da_heads)
    mod_b = mod_b[:B].reshape(B * 4, 1, D)
    x1 = _out_proj(na_o, da_o, na_out_g, w_out[0], x2d, mod_b, seq=S)
    out = _ffn(x1, mod_b, norm2_g, final_g.reshape(1, D), wg_bf, wu_bf, wd_bf, seq=S)
    return out.reshape(B, S, D)
```
